```python
import jax, jax.numpy as jnp
from jax import lax
import numpy as np

D_MODEL = 1024
BATCH = 4
SEQ = 4096
DEPTH = 1

CONV_WIDTH = 512
CONV_GROUPS = 8
CONV_K = 3
GLA_HEADS = 4
GLA_DK = 64
GLA_DV = 128
GLA_KEY_WIDTH = GLA_HEADS * GLA_DK
GLA_VAL_WIDTH = GLA_HEADS * GLA_DV
GATE_RANK = 16
GATE_NORMALIZER = 16.0
CHUNK = 64
MIX_WIDTH = CONV_WIDTH + GLA_VAL_WIDTH
IN_SIZES = [CONV_WIDTH, CONV_WIDTH, CONV_WIDTH,
            GLA_KEY_WIDTH, GLA_KEY_WIDTH, GLA_VAL_WIDTH,
            GLA_VAL_WIDTH, GATE_RANK]
IN_COLS = sum(IN_SIZES)
IN_SPLITS = np.cumsum(IN_SIZES)[:-1].tolist()
D_FF = 2816
N_ADA = 9
EPS = 1e-6

kernel_name = "hybrid_conv_gla_macaron_adaln"


def rms_norm(x, gain):
    xf = x.astype(jnp.float32)
    y = xf * lax.rsqrt(jnp.mean(xf * xf, axis=-1, keepdims=True) + EPS)
    return (y * gain.astype(jnp.float32)).astype(x.dtype)


def modulate(h, shift, scale):
    return h * (1.0 + scale[:, None, :]) + shift[:, None, :]


def swiglu_ffn(h, w_in, w_out):
    gate, up = jnp.split(h @ w_in, 2, axis=-1)
    return (jax.nn.silu(gate) * up) @ w_out


def causal_short_conv(u, w):
    T = u.shape[1]
    up = jnp.pad(u, ((0, 0), (CONV_K - 1, 0), (0, 0)))
    y = up[:, 0:T, :] * w[0]
    for k in range(1, CONV_K):
        y = y + up[:, k:k + T, :] * w[k]
    return y


def gla_chunked(q, k, v, log_g):
    B, H, T, DK = q.shape
    DV = v.shape[-1]
    N = T // CHUNK

    def to_chunks(a):
        return a.reshape(B, H, N, CHUNK, a.shape[-1]).transpose(2, 0, 1, 3, 4)

    qc, kc, vc, gc = to_chunks(q), to_chunks(k), to_chunks(v), to_chunks(log_g)
    bc = jnp.cumsum(gc, axis=3)
    causal = jnp.tril(jnp.ones((CHUNK, CHUNK), dtype=bool))[:, :, None]

    def step(S, inp):
        q_, k_, v_, b_ = inp
        o_inter = jnp.einsum('bhik,bhkv->bhiv', q_ * jnp.exp(b_), S)
        rel = b_[:, :, :, None, :] - b_[:, :, None, :, :]
        decay = jnp.where(causal, jnp.exp(jnp.where(causal, rel, 0.0)), 0.0)
        scores = jnp.einsum('bhik,bhijk,bhjk->bhij', q_, decay, k_)
        o_intra = jnp.einsum('bhij,bhjv->bhiv', scores, v_)
        b_last = b_[:, :, -1:, :]
        S_new = (jnp.exp(b_last[:, :, 0, :])[..., None] * S
                 + jnp.einsum('bhjk,bhjv->bhkv', k_ * jnp.exp(b_last - b_), v_))
        return S_new, o_inter + o_intra

    S0 = jnp.zeros((B, H, DK, DV), jnp.float32)
    _, o = lax.scan(step, S0, (qc, kc, vc, bc))
    return o.transpose(1, 2, 0, 3, 4).reshape(B, H, T, DV)


def token_mixer(h, w_in, conv_w, w_gk2, b_gk, gla_norm, w_out):
    Bsz, T, _ = h.shape
    proj = h @ w_in
    cb, cc, cv, q, k, v, g_out, gk_low = jnp.split(proj, IN_SPLITS, axis=-1)
    y_conv = cb * causal_short_conv(cc * cv, conv_w)
    log_g = jax.nn.log_sigmoid((gk_low @ w_gk2 + b_gk).astype(jnp.float32)) / GATE_NORMALIZER

    def heads(a, d):
        return a.reshape(Bsz, T, GLA_HEADS, d).transpose(0, 2, 1, 3).astype(jnp.float32)

    o = gla_chunked(heads(q, GLA_DK) * (GLA_DK ** -0.5), heads(k, GLA_DK),
                    heads(v, GLA_DV), heads(log_g, GLA_DK))
    o = rms_norm(o, gla_norm)
    o = o.transpose(0, 2, 1, 3).reshape(Bsz, T, GLA_VAL_WIDTH).astype(h.dtype)
    y_gla = o * jax.nn.silu(g_out)
    return jnp.concatenate([y_conv, y_gla], axis=-1) @ w_out


def setup_inputs(seed: int = 0) -> dict:
    key = jax.random.key(seed)
    ks = jax.random.split(key, 20)
    f32 = jnp.float32
    nrm = lambda k, shape, s: jax.random.normal(k, shape, f32) * s
    L, D = DEPTH, D_MODEL
    return {
        "x": nrm(ks[0], (BATCH, SEQ, D), 1.0),
        "c": nrm(ks[1], (BATCH, D), 1.0),
        "w_ada": nrm(ks[2], (L, D, N_ADA * D), 0.5 * D ** -0.5),
        "b_ada": nrm(ks[3], (L, N_ADA * D), 0.01),
        "norm_ffn1": 1.0 + nrm(ks[4], (L, D), 0.02),
        "w_ffn1_in": nrm(ks[5], (L, D, 2 * D_FF), D ** -0.5),
        "w_ffn1_out": nrm(ks[6], (L, D_FF, D), D_FF ** -0.5),
        "norm_mix": 1.0 + nrm(ks[7], (L, D), 0.02),
        "w_mix_in": nrm(ks[8], (L, D, IN_COLS), D ** -0.5),
        "conv_w": nrm(ks[9], (L, CONV_K, CONV_WIDTH), CONV_K ** -0.5),
        "w_gk2": nrm(ks[10], (L, GATE_RANK, GLA_KEY_WIDTH), GATE_RANK ** -0.5),
        "b_gk": nrm(ks[11], (L, GLA_KEY_WIDTH), 0.01),
        "gla_norm": 1.0 + nrm(ks[12], (L, GLA_DV), 0.02),
        "w_mix_out": nrm(ks[13], (L, MIX_WIDTH, D), MIX_WIDTH ** -0.5),
        "norm_ffn2": 1.0 + nrm(ks[14], (L, D), 0.02),
        "w_ffn2_in": nrm(ks[15], (L, D, 2 * D_FF), D ** -0.5),
        "w_ffn2_out": nrm(ks[16], (L, D_FF, D), D_FF ** -0.5),
        "norm_final": 1.0 + nrm(ks[17], (D,), 0.02),
    }


def reference(x, c, w_ada, b_ada, norm_ffn1, w_ffn1_in, w_ffn1_out, norm_mix,
              w_mix_in, conv_w, w_gk2, b_gk, gla_norm, w_mix_out, norm_ffn2,
              w_ffn2_in, w_ffn2_out, norm_final):
    c_act = jax.nn.silu(c)
    for l in range(DEPTH):
        ada = c_act @ w_ada[l] + b_ada[l]
        sh1, sc1, g1, sh2, sc2, g2, sh3, sc3, g3 = jnp.split(ada, N_ADA, axis=-1)
        h = modulate(rms_norm(x, norm_ffn1[l]), sh1, sc1)
        x = x + 0.5 * g1[:, None, :] * swiglu_ffn(h, w_ffn1_in[l], w_ffn1_out[l])
        h = modulate(rms_norm(x, norm_mix[l]), sh2, sc2)
        x = x + g2[:, None, :] * token_mixer(h, w_mix_in[l], conv_w[l], w_gk2[l],
                                             b_gk[l], gla_norm[l], w_mix_out[l])
        h = modulate(rms_norm(x, norm_ffn2[l]), sh3, sc3)
        x = x + 0.5 * g3[:, None, :] * swiglu_ffn(h, w_ffn2_in[l], w_ffn2_out[l])
    return rms_norm(x, norm_final)
```

```python
import functools

import jax
import jax.numpy as jnp
from jax import lax
from jax.experimental import pallas as pl
from jax.experimental.pallas import tpu as pltpu

F32 = jnp.float32
BF16 = jnp.bfloat16

EPS = 1e-6
CONV_WIDTH = 512
CONV_K = 3
GLA_HEADS = 4
GLA_DK = 64
GLA_DV = 128
KEY_W = GLA_HEADS * GLA_DK
VAL_W = GLA_HEADS * GLA_DV
GATE_RANK = 16
GATE_NORMALIZER = 16.0
CHUNK = 64
SUB = 16
NSUB = CHUNK // SUB
PAD = SUB
N_ADA = 9
LANES = 128
PAIR_W = 2 * GLA_DK
MIX_MAIN = 3 * CONV_WIDTH + 2 * KEY_W + 2 * VAL_W
VMEM_LIMIT = 56 * 1024 * 1024


def _dot(a, b):
    return jnp.dot(a, b, preferred_element_type=F32)


def _norm_mod(x, gain, shift, scale):
    ms = jnp.mean(x * x, axis=-1, keepdims=True)
    y = x * lax.rsqrt(ms + EPS) * gain
    return y * (1.0 + scale) + shift


def _ada_kernel(c_ref, w_ref, b_ref, o_ref):
    c = c_ref[...]
    ca = (c * jax.nn.sigmoid(c)).astype(BF16)
    o_ref[...] = _dot(ca, w_ref[...].astype(BF16)) + b_ref[...]


def _ada_call(c_pad, w_ada, b_ada):
    rows, d = c_pad.shape
    n = w_ada.shape[1]
    tn = d
    return pl.pallas_call(
        _ada_kernel,
        grid=(n // tn,),
        in_specs=[
            pl.BlockSpec((rows, d), lambda j: (0, 0)),
            pl.BlockSpec((d, tn), lambda j: (0, j)),
            pl.BlockSpec((1, tn), lambda j: (0, j)),
        ],
        out_specs=pl.BlockSpec((rows, tn), lambda j: (0, j)),
        out_shape=jax.ShapeDtypeStruct((rows, n), F32),
        compiler_params=pltpu.CompilerParams(
            dimension_semantics=("arbitrary",), vmem_limit_bytes=VMEM_LIMIT),
        name="ada_proj",
    )(c_pad, w_ada, b_ada)


def _ffn_kernel(x_ref, ada_ref, nw_ref, win_ref, wout_ref, nf_ref, o_ref, a_ref,
                *, ada_base, chunks, d_ff, final_norm):
    x = x_ref[0]
    shift = ada_ref[0, ada_base:ada_base + 1, :]
    scale = ada_ref[0, ada_base + 1:ada_base + 2, :]
    gate = ada_ref[0, ada_base + 2:ada_base + 3, :]
    h = _norm_mod(x, nw_ref[...], shift, scale).astype(BF16)
    off = 0
    for w in chunks:
        g = _dot(h, win_ref[:, off:off + w])
        u = _dot(h, win_ref[:, d_ff + off:d_ff + off + w])
        a_ref[:, off:off + w] = (g * jax.nn.sigmoid(g) * u).astype(BF16)
        off += w
    y = _dot(a_ref[...], wout_ref[...])
    out = x + (0.5 * gate) * y
    if final_norm:
        ms = jnp.mean(out * out, axis=-1, keepdims=True)
        out = out * lax.rsqrt(ms + EPS) * nf_ref[...]
    o_ref[0] = out


def _ffn_call(x, ada3, norm_w, w_in, w_out, norm_f, *, ada_base, final_norm, tm):
    bsz, t, d = x.shape
    d_ff = w_out.shape[0]
    chunks = [512] * (d_ff // 512)
    if d_ff % 512:
        chunks.append(d_ff % 512)
    kern = functools.partial(_ffn_kernel, ada_base=ada_base, chunks=tuple(chunks),
                             d_ff=d_ff, final_norm=final_norm)
    const = lambda b, i: (0, 0)
    return pl.pallas_call(
        kern,
        grid=(bsz, t // tm),
        in_specs=[
            pl.BlockSpec((1, tm, d), lambda b, i: (b, i, 0)),
            pl.BlockSpec((1, N_ADA, d), lambda b, i: (b, 0, 0)),
            pl.BlockSpec((1, d), const),
            pl.BlockSpec((d, 2 * d_ff), const, pipeline_mode=pl.Buffered(1)),
            pl.BlockSpec((d_ff, d), const, pipeline_mode=pl.Buffered(1)),
            pl.BlockSpec((1, d), const),
        ],
        out_specs=pl.BlockSpec((1, tm, d), lambda b, i: (b, i, 0)),
        out_shape=jax.ShapeDtypeStruct(x.shape, F32),
        scratch_shapes=[pltpu.VMEM((tm, d_ff), BF16)],
        compiler_params=pltpu.CompilerParams(
            dimension_semantics=("arbitrary", "arbitrary"), vmem_limit_bytes=VMEM_LIMIT),
        name="ffn_final" if final_norm else "ffn",
    )(x, ada3, norm_w, w_in, w_out, norm_f)


def _mix_kernel(x_ref, ada_ref, nw_ref, win_ref, wgk1_ref, convw_ref, wgk2_ref, bgk_ref,
                gn_ref, wout_ref, o_ref,
                s_ref, u_ref, q3, k3, lg3, b3, v3, o3, *, tt):
    nc = tt // CHUNK
    t_idx = pl.program_id(1)

    @pl.when(t_idx == 0)
    def _():
        s_ref[...] = jnp.zeros(s_ref.shape, F32)
        u_ref[0:8, :] = jnp.zeros((8, CONV_WIDTH), F32)

    k3[:, 0:PAD, :] = jnp.zeros((nc, PAD, KEY_W), F32)
    b3[:, 0:PAD, :] = jnp.zeros((nc, PAD, KEY_W), F32)

    x = x_ref[0]
    shift = ada_ref[0, 3:4, :]
    scale = ada_ref[0, 4:5, :]
    gate = ada_ref[0, 5:6, :]
    h = _norm_mod(x, nw_ref[...], shift, scale).astype(BF16)

    def proj(lo, width):
        return _dot(h, win_ref[:, lo:lo + width])

    cb = proj(0, CONV_WIDTH)
    u = proj(CONV_WIDTH, CONV_WIDTH) * proj(2 * CONV_WIDTH, CONV_WIDTH)
    u_ref[8:8 + tt, :] = u
    cw = convw_ref[...]
    conv = (u_ref[6:6 + tt, :] * cw[0:1, :] + u_ref[7:7 + tt, :] * cw[1:2, :]
            + u * cw[2:3, :])
    y_conv = cb * conv
    u_ref[0:8, :] = u_ref[tt:tt + 8, :]

    base = 3 * CONV_WIDTH
    q3[...] = (proj(base, KEY_W) * (GLA_DK ** -0.5)).reshape(nc, CHUNK, KEY_W)
    k3[:, PAD:PAD + CHUNK, :] = proj(base + KEY_W, KEY_W).reshape(nc, CHUNK, KEY_W)
    v3[...] = proj(base + 2 * KEY_W, VAL_W).reshape(nc, CHUNK, VAL_W)
    g_out = proj(base + 2 * KEY_W + VAL_W, VAL_W)
    gk_low = _dot(h, wgk1_ref[...])
    z = _dot(gk_low.astype(BF16), wgk2_ref[...]) + bgk_ref[...]
    log_sig = jnp.minimum(z, 0.0) - jnp.log1p(jnp.exp(-jnp.abs(z)))
    lg3[...] = (log_sig * (1.0 / GATE_NORMALIZER)).reshape(nc, CHUNK, KEY_W)

    row = lax.broadcasted_iota(jnp.int32, (CHUNK, KEY_W), 0)
    col = lax.broadcasted_iota(jnp.int32, (CHUNK, KEY_W), 1)
    row_in_sub = row % SUB
    jloc = col % GLA_DK
    tri_r = lax.broadcasted_iota(jnp.int32, (CHUNK, CHUNK), 0)
    tri_c = lax.broadcasted_iota(jnp.int32, (CHUNK, CHUNK), 1)
    lower = (tri_c <= tri_r).astype(F32)
    hr = lax.broadcasted_iota(jnp.int32, (KEY_W, KEY_W), 0) // GLA_DK
    hc = lax.broadcasted_iota(jnp.int32, (KEY_W, KEY_W), 1) // GLA_DK
    ones_bd = (hr == hc).astype(BF16)
    pr = lax.broadcasted_iota(jnp.int32, (PAIR_W, PAIR_W), 0)
    pc = lax.broadcasted_iota(jnp.int32, (PAIR_W, PAIR_W), 1)
    pair_bd = (pr // GLA_DK) == (pc // GLA_DK)
    zero_v = jnp.zeros((CHUNK, GLA_DV), F32)

    def chunk_body(c, carry):
        q = q3[c]
        k = k3[c, PAD:PAD + CHUNK, :]
        v = v3[c]
        b = jnp.dot(lower, lg3[c], precision=lax.Precision.HIGHEST,
                    preferred_element_type=F32)
        b3[c, PAD:PAD + CHUNK, :] = b
        b_last = b[CHUNK - 1:CHUNK, :]
        b_ref_rows = jnp.concatenate(
            [jnp.broadcast_to(b[SUB * i:SUB * i + 1, :], (SUB, KEY_W)) for i in range(NSUB)],
            axis=0)
        qg = q * jnp.exp(b)
        kd = k * jnp.exp(b_last - b)
        qn = q * jnp.exp(b - b_ref_rows)

        s = jnp.zeros((CHUNK, KEY_W), F32)
        for d in range(SUB):
            if d == 0:
                e = q * k
            else:
                k_d = k3[c, PAD - d:PAD - d + CHUNK, :]
                b_d = b3[c, PAD - d:PAD - d + CHUNK, :]
                valid = row_in_sub >= d
                e = jnp.where(valid, q * k_d * jnp.exp(jnp.where(valid, b - b_d, 0.0)), 0.0)
            r = _dot(e.astype(BF16), ones_bd)
            s = s + jnp.where(jloc == row - d, r, 0.0)

        off_rows = [jnp.zeros((SUB, KEY_W), F32)]
        for i in range(1, NSUB):
            nj = SUB * i
            kn = k * jnp.exp(jnp.where(row < nj, b[nj:nj + 1, :] - b, 0.0))
            outs = []
            for p in range(2):
                kp = kn[:, p * PAIR_W:(p + 1) * PAIR_W]
                kbd = jnp.where(pair_bd & ((pr % GLA_DK) < nj),
                                jnp.concatenate([kp, kp], axis=0), 0.0)
                qp = qn[nj:nj + SUB, p * PAIR_W:(p + 1) * PAIR_W]
                outs.append(lax.dot_general(qp.astype(BF16), kbd.astype(BF16),
                                            (((1,), (1,)), ((), ())),
                                            preferred_element_type=F32))
            off_rows.append(jnp.concatenate(outs, axis=1))
        s = s + jnp.concatenate(off_rows, axis=0)

        bt = jnp.transpose(b[CHUNK - 8:CHUNK, :])
        decay_col = jnp.exp(bt[:, 7:8])
        for p in range(2):
            h0, h1 = 2 * p, 2 * p + 1
            v0 = v[:, h0 * GLA_DV:(h0 + 1) * GLA_DV]
            v1 = v[:, h1 * GLA_DV:(h1 + 1) * GLA_DV]
            s0 = s_ref[h0]
            s1 = s_ref[h1]
            lhs = jnp.concatenate([s[:, p * PAIR_W:(p + 1) * PAIR_W],
                                   qg[:, p * PAIR_W:(p + 1) * PAIR_W]], axis=1).astype(BF16)
            w = jnp.concatenate([
                jnp.concatenate([v0, zero_v], axis=1),
                jnp.concatenate([zero_v, v1], axis=1),
                jnp.concatenate([s0, zero_v], axis=1),
                jnp.concatenate([zero_v, s1], axis=1)], axis=0).astype(BF16)
            o3[c, :, p * 2 * GLA_DV:(p + 1) * 2 * GLA_DV] = _dot(lhs, w)
            kdt = jnp.transpose(kd[:, p * PAIR_W:(p + 1) * PAIR_W]).astype(BF16)
            ktv = _dot(kdt, v[:, p * 2 * GLA_DV:(p + 1) * 2 * GLA_DV].astype(BF16))
            dec = decay_col[p * PAIR_W:(p + 1) * PAIR_W, :]
            s_ref[h0] = s0 * dec[0:GLA_DK, :] + ktv[0:GLA_DK, 0:GLA_DV]
            s_ref[h1] = s1 * dec[GLA_DK:PAIR_W, :] + ktv[GLA_DK:PAIR_W, GLA_DV:2 * GLA_DV]
        return carry

    lax.fori_loop(0, nc, chunk_body, 0)

    o = o3[...].reshape(tt, VAL_W)
    gn = gn_ref[...]
    parts = []
    for hd in range(GLA_HEADS):
        oh = o[:, hd * GLA_DV:(hd + 1) * GLA_DV]
        ms = jnp.mean(oh * oh, axis=-1, keepdims=True)
        parts.append(oh * lax.rsqrt(ms + EPS) * gn)
    y_gla = jnp.concatenate(parts, axis=1) * (g_out * jax.nn.sigmoid(g_out))
    y = jnp.concatenate([y_conv, y_gla], axis=1).astype(BF16)
    o_ref[0] = x + gate * _dot(y, wout_ref[...])


def _mix_call(x, ada3, norm_w, w_in, w_gk1, conv_w, w_gk2, b_gk, gla_norm, w_out, *, tt):
    bsz, t, d = x.shape
    nc = tt // CHUNK
    kern = functools.partial(_mix_kernel, tt=tt)
    const = lambda b, i: (0, 0)
    single = pl.Buffered(1)
    return pl.pallas_call(
        kern,
        grid=(bsz, t // tt),
        in_specs=[
            pl.BlockSpec((1, tt, d), lambda b, i: (b, i, 0)),
            pl.BlockSpec((1, N_ADA, d), lambda b, i: (b, 0, 0)),
            pl.BlockSpec((1, d), const),
            pl.BlockSpec(w_in.shape, const, pipeline_mode=single),
            pl.BlockSpec(w_gk1.shape, const, pipeline_mode=single),
            pl.BlockSpec(conv_w.shape, const),
            pl.BlockSpec(w_gk2.shape, const),
            pl.BlockSpec(b_gk.shape, const),
            pl.BlockSpec(gla_norm.shape, const),
            pl.BlockSpec(w_out.shape, const, pipeline_mode=single),
        ],
        out_specs=pl.BlockSpec((1, tt, d), lambda b, i: (b, i, 0)),
        out_shape=jax.ShapeDtypeStruct(x.shape, F32),
        scratch_shapes=[
            pltpu.VMEM((GLA_HEADS, GLA_DK, GLA_DV), F32),
            pltpu.VMEM((tt + 8, CONV_WIDTH), F32),
            pltpu.VMEM((nc, CHUNK, KEY_W), F32),
            pltpu.VMEM((nc, PAD + CHUNK, KEY_W), F32),
            pltpu.VMEM((nc, CHUNK, KEY_W), F32),
            pltpu.VMEM((nc, PAD + CHUNK, KEY_W), F32),
            pltpu.VMEM((nc, CHUNK, VAL_W), F32),
            pltpu.VMEM((nc, CHUNK, VAL_W), F32),
        ],
        compiler_params=pltpu.CompilerParams(
            dimension_semantics=("arbitrary", "arbitrary"), vmem_limit_bytes=VMEM_LIMIT),
        name="token_mixer",
    )(x, ada3, norm_w, w_in, w_gk1, conv_w, w_gk2, b_gk, gla_norm, w_out)


def kernel(x, c, w_ada, b_ada, norm_ffn1, w_ffn1_in, w_ffn1_out, norm_mix, w_mix_in, conv_w,
           w_gk2, b_gk, gla_norm, w_mix_out, norm_ffn2, w_ffn2_in, w_ffn2_out, norm_final):
    bsz, t, d = x.shape
    depth = w_ada.shape[0]
    c_pad = jnp.zeros((8, d), F32).at[:bsz].set(c)
    norm_f = norm_final.reshape(1, d)
    for l in range(depth):
        last = l == depth - 1
        ada = _ada_call(c_pad, w_ada[l], b_ada[l].reshape(1, -1))
        ada3 = ada[:bsz].reshape(bsz, N_ADA, d)
        x = _ffn_call(x, ada3, norm_ffn1[l].reshape(1, d), w_ffn1_in[l].astype(BF16),
                      w_ffn1_out[l].astype(BF16), norm_f, ada_base=0, final_norm=False, tm=512)
        w_gk1 = jnp.zeros((d, LANES), F32).at[:, :GATE_RANK].set(w_mix_in[l][:, MIX_MAIN:])
        w_gk2p = jnp.zeros((LANES, KEY_W), F32).at[:GATE_RANK].set(w_gk2[l])
        x = _mix_call(x, ada3, norm_mix[l].reshape(1, d), w_mix_in[l][:, :MIX_MAIN].astype(BF16),
                      w_gk1.astype(BF16), conv_w[l], w_gk2p.astype(BF16),
                      b_gk[l].reshape(1, KEY_W), gla_norm[l].reshape(1, GLA_DV),
                      w_mix_out[l].astype(BF16), tt=512)
        x = _ffn_call(x, ada3, norm_ffn2[l].reshape(1, d), w_ffn2_in[l].astype(BF16),
                      w_ffn2_out[l].astype(BF16), norm_f, ada_base=6, final_norm=last, tm=512)
    return x
```

```python
import functools
import math

import jax
import jax.numpy as jnp
from jax import lax
from jax.experimental import pallas as pl
from jax.experimental.pallas import tpu as pltpu

F32 = jnp.float32
BF16 = jnp.bfloat16

EPS = 1e-6
CONV_WIDTH = 512
CONV_K = 3
GLA_HEADS = 4
GLA_DK = 64
GLA_DV = 128
KEY_W = GLA_HEADS * GLA_DK
VAL_W = GLA_HEADS * GLA_DV
GATE_RANK = 16
GATE_NORMALIZER = 16.0
CHUNK = 64
SUB = 8
NSUB = CHUNK // SUB
PAD = 8
N_ADA = 9
LANES = 128
PAIR_W = 2 * GLA_DK
PAIR_V = 2 * GLA_DV
MIX_MAIN = 3 * CONV_WIDTH + 2 * KEY_W + 2 * VAL_W
VMEM_LIMIT = 56 * 1024 * 1024
LOG2E = math.log2(math.e)


def _dot(a, b):
    return jnp.dot(a, b, preferred_element_type=F32)


def _norm_mod(x, gain, shift, scale):
    ms = jnp.mean(x * x, axis=-1, keepdims=True)
    y = x * lax.rsqrt(ms + EPS) * gain
    return y * (1.0 + scale) + shift


def _ada_kernel(c_ref, w_ref, b_ref, o_ref):
    c = c_ref[...]
    ca = (c * jax.nn.sigmoid(c)).astype(BF16)
    o_ref[...] = _dot(ca, w_ref[...].astype(BF16)) + b_ref[...]


def _ada_call(c_pad, w_ada, b_ada):
    rows, d = c_pad.shape
    n = w_ada.shape[1]
    tn = d
    return pl.pallas_call(
        _ada_kernel,
        grid=(n // tn,),
        in_specs=[
            pl.BlockSpec((rows, d), lambda j: (0, 0)),
            pl.BlockSpec((d, tn), lambda j: (0, j)),
            pl.BlockSpec((1, tn), lambda j: (0, j)),
        ],
        out_specs=pl.BlockSpec((rows, tn), lambda j: (0, j)),
        out_shape=jax.ShapeDtypeStruct((rows, n), F32),
        compiler_params=pltpu.CompilerParams(
            dimension_semantics=("arbitrary",), vmem_limit_bytes=VMEM_LIMIT),
        name="ada_proj",
    )(c_pad, w_ada, b_ada)


def _ffn_kernel(x_ref, ada_ref, nw_ref, win_ref, wout_ref, nf_ref, o_ref, a_ref,
                *, ada_base, chunks, d_ff, final_norm):
    x = x_ref[0]
    shift = ada_ref[0, ada_base:ada_base + 1, :]
    scale = ada_ref[0, ada_base + 1:ada_base + 2, :]
    gate = ada_ref[0, ada_base + 2:ada_base + 3, :]
    h = _norm_mod(x, nw_ref[...], shift, scale).astype(BF16)
    off = 0
    for w in chunks:
        g = _dot(h, win_ref[:, off:off + w])
        u = _dot(h, win_ref[:, d_ff + off:d_ff + off + w])
        a_ref[:, off:off + w] = (g * jax.nn.sigmoid(g) * u).astype(BF16)
        off += w
    y = _dot(a_ref[...], wout_ref[...])
    out = x + (0.5 * gate) * y
    if final_norm:
        ms = jnp.mean(out * out, axis=-1, keepdims=True)
        out = out * lax.rsqrt(ms + EPS) * nf_ref[...]
    o_ref[0] = out


def _ffn_call(x, ada3, norm_w, w_in, w_out, norm_f, *, ada_base, final_norm, tm):
    bsz, t, d = x.shape
    d_ff = w_out.shape[0]
    chunks = [512] * (d_ff // 512)
    if d_ff % 512:
        chunks.append(d_ff % 512)
    kern = functools.partial(_ffn_kernel, ada_base=ada_base, chunks=tuple(chunks),
                             d_ff=d_ff, final_norm=final_norm)
    const = lambda b, i: (0, 0)
    return pl.pallas_call(
        kern,
        grid=(bsz, t // tm),
        in_specs=[
            pl.BlockSpec((1, tm, d), lambda b, i: (b, i, 0)),
            pl.BlockSpec((1, N_ADA, d), lambda b, i: (b, 0, 0)),
            pl.BlockSpec((1, d), const),
            pl.BlockSpec((d, 2 * d_ff), const, pipeline_mode=pl.Buffered(1)),
            pl.BlockSpec((d_ff, d), const, pipeline_mode=pl.Buffered(1)),
            pl.BlockSpec((1, d), const),
        ],
        out_specs=pl.BlockSpec((1, tm, d), lambda b, i: (b, i, 0)),
        out_shape=jax.ShapeDtypeStruct(x.shape, F32),
        scratch_shapes=[pltpu.VMEM((tm, d_ff), BF16)],
        compiler_params=pltpu.CompilerParams(
            dimension_semantics=("arbitrary", "arbitrary"), vmem_limit_bytes=VMEM_LIMIT),
        name="ffn_final" if final_norm else "ffn",
    )(x, ada3, norm_w, w_in, w_out, norm_f)


def _band_masks():
    i = jnp.arange(CHUNK)[None, :, None]
    j = (jnp.arange(KEY_W) % GLA_DK)[None, None, :]
    d = jnp.arange(SUB)[:, None, None]
    return ((j == i - d) & (i % SUB >= d)).astype(F32)


def _mix_kernel(x_ref, ada_ref, nw_ref, win_ref, wgk1_ref, convw_ref, wgk2_ref, bgk_ref,
                gn_ref, wout_ref, band_ref, o_ref,
                s_ref, u_ref, q_s, k_s, lgs3, b_s, v_s, o_s, sc3, sb_s, *, tt):
    nc = tt // CHUNK
    t_idx = pl.program_id(1)

    @pl.when(t_idx == 0)
    def _():
        s_ref[...] = jnp.zeros(s_ref.shape, F32)
        u_ref[0:8, :] = jnp.zeros((8, CONV_WIDTH), F32)
        k_s[0:PAD, :] = jnp.zeros((PAD, KEY_W), F32)
        b_s[0:PAD, :] = jnp.zeros((PAD, KEY_W), F32)
        lgs3[:, 3 * CHUNK:4 * CHUNK, :] = jnp.zeros((nc, CHUNK, KEY_W), BF16)

    x = x_ref[0]
    shift = ada_ref[0, 3:4, :]
    scale = ada_ref[0, 4:5, :]
    gate = ada_ref[0, 5:6, :]
    h = _norm_mod(x, nw_ref[...], shift, scale).astype(BF16)

    def proj(lo, width):
        return _dot(h, win_ref[:, lo:lo + width])

    tri_r = lax.broadcasted_iota(jnp.int32, (CHUNK, 4 * CHUNK), 0)
    tri_c = lax.broadcasted_iota(jnp.int32, (CHUNK, 4 * CHUNK), 1)
    lower4 = ((tri_c % CHUNK) <= tri_r).astype(BF16)
    hr = lax.broadcasted_iota(jnp.int32, (KEY_W, KEY_W), 0) // GLA_DK
    hc = lax.broadcasted_iota(jnp.int32, (KEY_W, KEY_W), 1) // GLA_DK
    ones_bd = (hr == hc).astype(BF16)
    pr = lax.broadcasted_iota(jnp.int32, (PAIR_W, PAIR_W), 0)
    pc = lax.broadcasted_iota(jnp.int32, (PAIR_W, PAIR_W), 1)
    pair_bd = (pr // GLA_DK) == (pc // GLA_DK)
    col_blk = (lax.broadcasted_iota(jnp.int32, (1, PAIR_W), 1) % GLA_DK) // SUB
    zero_v = jnp.zeros((CHUNK, GLA_DV), BF16)

    base = 3 * CONV_WIDTH
    q_s[...] = proj(base, KEY_W) * (GLA_DK ** -0.5)
    k_s[PAD:PAD + tt, :] = proj(base + KEY_W, KEY_W)
    v_s[...] = proj(base + 2 * KEY_W, VAL_W).astype(BF16)
    gk_low = _dot(h, wgk1_ref[...])
    z = _dot(gk_low.astype(BF16), wgk2_ref[...]) + bgk_ref[...]
    log_sig = jnp.minimum(z, 0.0) - jnp.log1p(jnp.exp(-jnp.abs(z)))
    lg = log_sig * (LOG2E / GATE_NORMALIZER)
    lg_hi = lg.astype(BF16)
    rem = lg - lg_hi.astype(F32)
    lg_mid = rem.astype(BF16)
    lg_lo = (rem - lg_mid.astype(F32)).astype(BF16)
    lgs3[:, 0:CHUNK, :] = lg_hi.reshape(nc, CHUNK, KEY_W)
    lgs3[:, CHUNK:2 * CHUNK, :] = lg_mid.reshape(nc, CHUNK, KEY_W)
    lgs3[:, 2 * CHUNK:3 * CHUNK, :] = lg_lo.reshape(nc, CHUNK, KEY_W)
    for c in range(nc):
        b_s[PAD + c * CHUNK:PAD + (c + 1) * CHUNK, :] = _dot(lower4, lgs3[c])

    cb = proj(0, CONV_WIDTH)
    u = proj(CONV_WIDTH, CONV_WIDTH) * proj(2 * CONV_WIDTH, CONV_WIDTH)
    u_ref[8:8 + tt, :] = u
    cw = convw_ref[...]
    conv = (u_ref[6:6 + tt, :] * cw[0:1, :] + u_ref[7:7 + tt, :] * cw[1:2, :]
            + u * cw[2:3, :])
    y_conv = cb * conv
    u_ref[0:8, :] = u_ref[tt:tt + 8, :]

    q = q_s[...]
    k = k_s[PAD:PAD + tt, :]
    b = b_s[PAD:PAD + tt, :]
    b3 = b.reshape(nc, CHUNK, KEY_W)
    q3 = q.reshape(nc, CHUNK, KEY_W)
    b_last = jnp.broadcast_to(b3[:, CHUNK - 1:CHUNK, :], (nc, CHUNK, KEY_W)).reshape(tt, KEY_W)
    b8 = b.reshape(tt // SUB, SUB, KEY_W)
    b_end = jnp.broadcast_to(b8[:, SUB - 1:SUB, :], (tt // SUB, SUB, KEY_W)).reshape(tt, KEY_W)
    qg = (q * jnp.exp2(b)).astype(BF16)
    kd = k * jnp.exp2(b_last - b)
    kk = k * jnp.exp2(b_end - b)

    s = None
    for d in range(SUB):
        if d == 0:
            e = q * k
        else:
            e = q * k_s[PAD - d:PAD - d + tt, :] * jnp.exp2(b - b_s[PAD - d:PAD - d + tt, :])
        term = _dot(e.astype(BF16), ones_bd).reshape(nc, CHUNK, KEY_W) * band_ref[d]
        s = term if s is None else s + term
    sc3[...] = s

    g_out = proj(base + 2 * KEY_W + VAL_W, VAL_W)

    qj_all = []
    for j in range(NSUB - 1):
        r0 = SUB * (j + 1)
        qj_all.append((q3[:, r0:, :] * jnp.exp2(b3[:, r0:, :] - b3[:, r0 - 1:r0, :])).astype(BF16))
    for c in range(nc):
        rows = slice(c * CHUNK, (c + 1) * CHUNK)
        for p in range(2):
            lanes = slice(p * PAIR_W, (p + 1) * PAIR_W)
            kp = kk[rows, lanes]
            kbd = jnp.where(pair_bd, jnp.concatenate([kp, kp], axis=0), 0.0).astype(BF16)
            lhs = jnp.concatenate([qj[c, :, lanes] for qj in qj_all], axis=0)
            out = lax.dot_general(lhs, kbd, (((1,), (1,)), ((), ())),
                                  preferred_element_type=F32)
            off = 0
            for j in range(NSUB - 1):
                r0 = SUB * (j + 1)
                n = CHUNK - r0
                sc3[c, r0:, lanes] = jnp.where(col_blk == j, out[off:off + n, :],
                                               sc3[c, r0:, lanes])
                off += n

    states = [s_ref[hd] for hd in range(GLA_HEADS)]
    for c in range(nc):
        rows = slice(c * CHUNK, (c + 1) * CHUNK)
        bt = jnp.transpose(b[(c + 1) * CHUNK - 8:(c + 1) * CHUNK, :])
        decay_col = jnp.exp2(bt[:, 7:8])
        for hd in range(GLA_HEADS):
            sb_s[c, hd] = states[hd].astype(BF16)
        for p in range(2):
            lanes = slice(p * PAIR_W, (p + 1) * PAIR_W)
            kdt = jnp.transpose(kd[rows, lanes]).astype(BF16)
            ktv = _dot(kdt, v_s[rows, p * PAIR_V:(p + 1) * PAIR_V])
            dec = decay_col[lanes, :]
            h0, h1 = 2 * p, 2 * p + 1
            states[h0] = states[h0] * dec[0:GLA_DK, :] + ktv[0:GLA_DK, 0:GLA_DV]
            states[h1] = states[h1] * dec[GLA_DK:PAIR_W, :] + ktv[GLA_DK:PAIR_W, GLA_DV:PAIR_V]
    for hd in range(GLA_HEADS):
        s_ref[hd] = states[hd]

    for c in range(nc):
        rows = slice(c * CHUNK, (c + 1) * CHUNK)
        for p in range(2):
            lanes = slice(p * PAIR_W, (p + 1) * PAIR_W)
            h0, h1 = 2 * p, 2 * p + 1
            v0 = v_s[rows, h0 * GLA_DV:(h0 + 1) * GLA_DV]
            v1 = v_s[rows, h1 * GLA_DV:(h1 + 1) * GLA_DV]
            lhs = jnp.concatenate([sc3[c, :, lanes].astype(BF16), qg[rows, lanes]], axis=1)
            w = jnp.concatenate([
                jnp.concatenate([v0, zero_v], axis=1),
                jnp.concatenate([zero_v, v1], axis=1),
                jnp.concatenate([sb_s[c, h0], zero_v], axis=1),
                jnp.concatenate([zero_v, sb_s[c, h1]], axis=1)], axis=0)
            o_s[rows, p * PAIR_V:(p + 1) * PAIR_V] = _dot(lhs, w)

    o = o_s[...]
    gn = gn_ref[...]
    parts = []
    for hd in range(GLA_HEADS):
        oh = o[:, hd * GLA_DV:(hd + 1) * GLA_DV]
        ms = jnp.mean(oh * oh, axis=-1, keepdims=True)
        parts.append(oh * lax.rsqrt(ms + EPS) * gn)
    y_gla = jnp.concatenate(parts, axis=1) * (g_out * jax.nn.sigmoid(g_out))
    y = jnp.concatenate([y_conv, y_gla], axis=1).astype(BF16)
    o_ref[0] = x + gate * _dot(y, wout_ref[...])


def _mix_call(x, ada3, norm_w, w_in, w_gk1, conv_w, w_gk2, b_gk, gla_norm, w_out, *, tt):
    bsz, t, d = x.shape
    nc = tt // CHUNK
    band = _band_masks()
    kern = functools.partial(_mix_kernel, tt=tt)
    const = lambda b, i: (0, 0)
    single = pl.Buffered(1)
    return pl.pallas_call(
        kern,
        grid=(bsz, t // tt),
        in_specs=[
            pl.BlockSpec((1, tt, d), lambda b, i: (b, i, 0)),
            pl.BlockSpec((1, N_ADA, d), lambda b, i: (b, 0, 0)),
            pl.BlockSpec((1, d), const),
            pl.BlockSpec(w_in.shape, const, pipeline_mode=single),
            pl.BlockSpec(w_gk1.shape, const, pipeline_mode=single),
            pl.BlockSpec(conv_w.shape, const),
            pl.BlockSpec(w_gk2.shape, const),
            pl.BlockSpec(b_gk.shape, const),
            pl.BlockSpec(gla_norm.shape, const),
            pl.BlockSpec(w_out.shape, const, pipeline_mode=single),
            pl.BlockSpec(band.shape, lambda b, i: (0, 0, 0), pipeline_mode=single),
        ],
        out_specs=pl.BlockSpec((1, tt, d), lambda b, i: (b, i, 0)),
        out_shape=jax.ShapeDtypeStruct(x.shape, F32),
        scratch_shapes=[
            pltpu.VMEM((GLA_HEADS, GLA_DK, GLA_DV), F32),
            pltpu.VMEM((tt + 8, CONV_WIDTH), F32),
            pltpu.VMEM((tt, KEY_W), F32),
            pltpu.VMEM((PAD + tt, KEY_W), F32),
            pltpu.VMEM((nc, 4 * CHUNK, KEY_W), BF16),
            pltpu.VMEM((PAD + tt, KEY_W), F32),
            pltpu.VMEM((tt, VAL_W), BF16),
            pltpu.VMEM((tt, VAL_W), F32),
            pltpu.VMEM((nc, CHUNK, KEY_W), F32),
            pltpu.VMEM((nc, GLA_HEADS, GLA_DK, GLA_DV), BF16),
        ],
        compiler_params=pltpu.CompilerParams(
            dimension_semantics=("arbitrary", "arbitrary"), vmem_limit_bytes=VMEM_LIMIT),
        name="token_mixer",
    )(x, ada3, norm_w, w_in, w_gk1, conv_w, w_gk2, b_gk, gla_norm, w_out, band)


def kernel(x, c, w_ada, b_ada, norm_ffn1, w_ffn1_in, w_ffn1_out, norm_mix, w_mix_in, conv_w,
           w_gk2, b_gk, gla_norm, w_mix_out, norm_ffn2, w_ffn2_in, w_ffn2_out, norm_final):
    bsz, t, d = x.shape
    depth = w_ada.shape[0]
    c_pad = jnp.zeros((8, d), F32).at[:bsz].set(c)
    norm_f = norm_final.reshape(1, d)
    for l in range(depth):
        last = l == depth - 1
        ada = _ada_call(c_pad, w_ada[l], b_ada[l].reshape(1, -1))
        ada3 = ada[:bsz].reshape(bsz, N_ADA, d)
        x = _ffn_call(x, ada3, norm_ffn1[l].reshape(1, d), w_ffn1_in[l].astype(BF16),
                      w_ffn1_out[l].astype(BF16), norm_f, ada_base=0, final_norm=False, tm=512)
        w_gk1 = jnp.zeros((d, LANES), F32).at[:, :GATE_RANK].set(w_mix_in[l][:, MIX_MAIN:])
        w_gk2p = jnp.zeros((LANES, KEY_W), F32).at[:GATE_RANK].set(w_gk2[l])
        x = _mix_call(x, ada3, norm_mix[l].reshape(1, d), w_mix_in[l][:, :MIX_MAIN].astype(BF16),
                      w_gk1.astype(BF16), conv_w[l], w_gk2p.astype(BF16),
                      b_gk[l].reshape(1, KEY_W), gla_norm[l].reshape(1, GLA_DV),
                      w_mix_out[l].astype(BF16), tt=512)
        x = _ffn_call(x, ada3, norm_ffn2[l].reshape(1, d), w_ffn2_in[l].astype(BF16),
                      w_ffn2_out[l].astype(BF16), norm_f, ada_base=6, final_norm=last, tm=512)
    return x
```

```python
import functools
import math

import jax
import jax.numpy as jnp
from jax import lax
from jax.experimental import pallas as pl
from jax.experimental.pallas import tpu as pltpu

F32 = jnp.float32
BF16 = jnp.bfloat16

EPS = 1e-6
CONV_WIDTH = 512
CONV_K = 3
GLA_HEADS = 4
GLA_DK = 64
GLA_DV = 128
KEY_W = GLA_HEADS * GLA_DK
VAL_W = GLA_HEADS * GLA_DV
GATE_RANK = 16
GATE_NORMALIZER = 16.0
CHUNK = 64
SUB = 8
NSUB = CHUNK // SUB
PAD = 8
N_ADA = 9
LANES = 128
PAIR_W = 2 * GLA_DK
PAIR_V = 2 * GLA_DV
MIX_MAIN = 3 * CONV_WIDTH + 2 * KEY_W + 2 * VAL_W
VMEM_LIMIT = 56 * 1024 * 1024
LOG2E = math.log2(math.e)


def _dot(a, b):
    return jnp.dot(a, b, preferred_element_type=F32)


def _norm_mod(x, gain, shift, scale):
    ms = jnp.mean(x * x, axis=-1, keepdims=True)
    y = x * lax.rsqrt(ms + EPS) * gain
    return y * (1.0 + scale) + shift


def _ada_kernel(c_ref, w_ref, b_ref, o_ref):
    c = c_ref[...]
    ca = (c * jax.nn.sigmoid(c)).astype(BF16)
    o_ref[...] = _dot(ca, w_ref[...].astype(BF16)) + b_ref[...]


def _ada_call(c_pad, w_ada, b_ada):
    rows, d = c_pad.shape
    n = w_ada.shape[1]
    tn = d
    return pl.pallas_call(
        _ada_kernel,
        grid=(n // tn,),
        in_specs=[
            pl.BlockSpec((rows, d), lambda j: (0, 0)),
            pl.BlockSpec((d, tn), lambda j: (0, j)),
            pl.BlockSpec((1, tn), lambda j: (0, j)),
        ],
        out_specs=pl.BlockSpec((rows, tn), lambda j: (0, j)),
        out_shape=jax.ShapeDtypeStruct((rows, n), F32),
        compiler_params=pltpu.CompilerParams(
            dimension_semantics=("arbitrary",), vmem_limit_bytes=VMEM_LIMIT),
        name="ada_proj",
    )(c_pad, w_ada, b_ada)


def _ffn_kernel(x_ref, ada_ref, nw_ref, win_ref, wout_ref, nf_ref, o_ref, a_ref,
                *, ada_base, chunks, d_ff, final_norm, n_sub):
    shift = ada_ref[0, ada_base:ada_base + 1, :]
    scale = ada_ref[0, ada_base + 1:ada_base + 2, :]
    gate = ada_ref[0, ada_base + 2:ada_base + 3, :]
    rows = x_ref.shape[1] // n_sub

    def normed(r):
        return _norm_mod(x_ref[0, r * rows:(r + 1) * rows, :], nw_ref[...], shift,
                         scale).astype(BF16)

    h_next = normed(0)
    for r in range(n_sub):
        rs = slice(r * rows, (r + 1) * rows)
        h = h_next
        off = 0
        for idx, w in enumerate(chunks):
            g = _dot(h, win_ref[:, off:off + w].astype(BF16))
            u = _dot(h, win_ref[:, d_ff + off:d_ff + off + w].astype(BF16))
            a_ref[rs, off:off + w] = (g * jax.nn.sigmoid(g) * u).astype(BF16)
            off += w
            if idx == 0 and r + 1 < n_sub:
                h_next = normed(r + 1)
        y = _dot(a_ref[rs, :], wout_ref[...].astype(BF16))
        out = x_ref[0, rs, :] + (0.5 * gate) * y
        if final_norm:
            ms = jnp.mean(out * out, axis=-1, keepdims=True)
            out = out * lax.rsqrt(ms + EPS) * nf_ref[...]
        o_ref[0, rs, :] = out


def _ffn_call(x, ada3, norm_w, w_in, w_out, norm_f, *, ada_base, final_norm, tm):
    bsz, t, d = x.shape
    d_ff = w_out.shape[0]
    chunks = [512] * (d_ff // 512)
    if d_ff % 512:
        chunks.append(d_ff % 512)
    kern = functools.partial(_ffn_kernel, ada_base=ada_base, chunks=tuple(chunks),
                             d_ff=d_ff, final_norm=final_norm, n_sub=1)
    const = lambda b, i: (0, 0)
    return pl.pallas_call(
        kern,
        grid=(bsz, t // tm),
        in_specs=[
            pl.BlockSpec((1, tm, d), lambda b, i: (b, i, 0)),
            pl.BlockSpec((1, N_ADA, d), lambda b, i: (b, 0, 0)),
            pl.BlockSpec((1, d), const),
            pl.BlockSpec((d, 2 * d_ff), const, pipeline_mode=pl.Buffered(1)),
            pl.BlockSpec((d_ff, d), const, pipeline_mode=pl.Buffered(1)),
            pl.BlockSpec((1, d), const),
        ],
        out_specs=pl.BlockSpec((1, tm, d), lambda b, i: (b, i, 0)),
        out_shape=jax.ShapeDtypeStruct(x.shape, F32),
        scratch_shapes=[pltpu.VMEM((tm, d_ff), BF16)],
        compiler_params=pltpu.CompilerParams(
            dimension_semantics=("arbitrary", "arbitrary"), vmem_limit_bytes=VMEM_LIMIT),
        name="ffn_final" if final_norm else "ffn",
    )(x, ada3, norm_w, w_in, w_out, norm_f)


def _band_masks():
    i = jnp.arange(CHUNK)[None, :, None]
    j = (jnp.arange(KEY_W) % GLA_DK)[None, None, :]
    d = jnp.arange(SUB)[:, None, None]
    return ((j == i - d) & (i % SUB >= d)).astype(F32)


def _mix_kernel(x_ref, ada_ref, nw_ref, win_ref, convw_ref, wgk2_ref, bgk_ref,
                gn_ref, wout_ref, band_ref, o_ref,
                s_ref, u_ref, q_s, k_s, lgs3, b_s, v_s, o_s, sc3, sb_s, *, tt):
    nc = tt // CHUNK
    t_idx = pl.program_id(1)

    @pl.when(t_idx == 0)
    def _():
        s_ref[...] = jnp.zeros(s_ref.shape, F32)
        u_ref[0:8, :] = jnp.zeros((8, CONV_WIDTH), F32)
        k_s[0:PAD, :] = jnp.zeros((PAD, KEY_W), F32)
        b_s[0:PAD, :] = jnp.zeros((PAD, KEY_W), F32)
        lgs3[:, 3 * CHUNK:4 * CHUNK, :] = jnp.zeros((nc, CHUNK, KEY_W), BF16)

    x = x_ref[0]
    shift = ada_ref[0, 3:4, :]
    scale = ada_ref[0, 4:5, :]
    gate = ada_ref[0, 5:6, :]
    h = _norm_mod(x, nw_ref[...], shift, scale).astype(BF16)

    def proj(lo, width):
        return lax.dot_general(h, win_ref[lo:lo + width, :].astype(BF16),
                               (((1,), (1,)), ((), ())), preferred_element_type=F32)

    tri_r = lax.broadcasted_iota(jnp.int32, (CHUNK, 4 * CHUNK), 0)
    tri_c = lax.broadcasted_iota(jnp.int32, (CHUNK, 4 * CHUNK), 1)
    lower4 = ((tri_c % CHUNK) <= tri_r).astype(BF16)
    hr = lax.broadcasted_iota(jnp.int32, (KEY_W, KEY_W), 0) // GLA_DK
    hc = lax.broadcasted_iota(jnp.int32, (KEY_W, KEY_W), 1) // GLA_DK
    ones_bd = (hr == hc).astype(BF16)
    pr = lax.broadcasted_iota(jnp.int32, (PAIR_W, PAIR_W), 0)
    pc = lax.broadcasted_iota(jnp.int32, (PAIR_W, PAIR_W), 1)
    pair_bd = (pr // GLA_DK) == (pc // GLA_DK)
    col_blk = (lax.broadcasted_iota(jnp.int32, (1, PAIR_W), 1) % GLA_DK) // SUB
    zero_v = jnp.zeros((CHUNK, GLA_DV), BF16)

    base = 3 * CONV_WIDTH
    q_s[...] = proj(base, KEY_W) * (GLA_DK ** -0.5)
    k_s[PAD:PAD + tt, :] = proj(base + KEY_W, KEY_W)
    v_s[...] = proj(base + 2 * KEY_W, VAL_W).astype(BF16)
    gk_low = proj(MIX_MAIN, GATE_RANK)
    z = _dot(gk_low.astype(BF16), wgk2_ref[...].astype(BF16)) + bgk_ref[...]
    log_sig = jnp.minimum(z, 0.0) - jnp.log1p(jnp.exp(-jnp.abs(z)))
    lg = log_sig * (LOG2E / GATE_NORMALIZER)
    lg_hi = lg.astype(BF16)
    rem = lg - lg_hi.astype(F32)
    lg_mid = rem.astype(BF16)
    lg_lo = (rem - lg_mid.astype(F32)).astype(BF16)
    lgs3[:, 0:CHUNK, :] = lg_hi.reshape(nc, CHUNK, KEY_W)
    lgs3[:, CHUNK:2 * CHUNK, :] = lg_mid.reshape(nc, CHUNK, KEY_W)
    lgs3[:, 2 * CHUNK:3 * CHUNK, :] = lg_lo.reshape(nc, CHUNK, KEY_W)
    for c in range(nc):
        b_s[PAD + c * CHUNK:PAD + (c + 1) * CHUNK, :] = _dot(lower4, lgs3[c])

    cb = proj(0, CONV_WIDTH)
    u = proj(CONV_WIDTH, CONV_WIDTH) * proj(2 * CONV_WIDTH, CONV_WIDTH)
    u_ref[8:8 + tt, :] = u
    cw = convw_ref[...]
    conv = (u_ref[6:6 + tt, :] * cw[0:1, :] + u_ref[7:7 + tt, :] * cw[1:2, :]
            + u * cw[2:3, :])
    y_conv = cb * conv
    u_ref[0:8, :] = u_ref[tt:tt + 8, :]

    q = q_s[...]
    k = k_s[PAD:PAD + tt, :]
    b = b_s[PAD:PAD + tt, :]
    b3 = b.reshape(nc, CHUNK, KEY_W)
    q3 = q.reshape(nc, CHUNK, KEY_W)
    b_last = jnp.broadcast_to(b3[:, CHUNK - 1:CHUNK, :], (nc, CHUNK, KEY_W)).reshape(tt, KEY_W)
    b8 = b.reshape(tt // SUB, SUB, KEY_W)
    b_end = jnp.broadcast_to(b8[:, SUB - 1:SUB, :], (tt // SUB, SUB, KEY_W)).reshape(tt, KEY_W)
    qg = (q * jnp.exp2(b)).astype(BF16)
    kd = k * jnp.exp2(b_last - b)
    kk = k * jnp.exp2(b_end - b)

    s = None
    for d in range(SUB):
        if d == 0:
            e = q * k
        else:
            e = q * k_s[PAD - d:PAD - d + tt, :] * jnp.exp2(b - b_s[PAD - d:PAD - d + tt, :])
        term = _dot(e.astype(BF16), ones_bd).reshape(nc, CHUNK, KEY_W) * band_ref[d]
        s = term if s is None else s + term
    sc3[...] = s

    g_out = proj(base + 2 * KEY_W + VAL_W, VAL_W)

    qj_all = []
    for j in range(NSUB - 1):
        r0 = SUB * (j + 1)
        qj_all.append((q3[:, r0:, :] * jnp.exp2(b3[:, r0:, :] - b3[:, r0 - 1:r0, :])).astype(BF16))
    for c in range(nc):
        rows = slice(c * CHUNK, (c + 1) * CHUNK)
        for p in range(2):
            lanes = slice(p * PAIR_W, (p + 1) * PAIR_W)
            kp = kk[rows, lanes]
            kbd = jnp.where(pair_bd, jnp.concatenate([kp, kp], axis=0), 0.0).astype(BF16)
            lhs = jnp.concatenate([qj[c, :, lanes] for qj in qj_all], axis=0)
            out = lax.dot_general(lhs, kbd, (((1,), (1,)), ((), ())),
                                  preferred_element_type=F32)
            off = 0
            for j in range(NSUB - 1):
                r0 = SUB * (j + 1)
                n = CHUNK - r0
                sc3[c, r0:, lanes] = jnp.where(col_blk == j, out[off:off + n, :],
                                               sc3[c, r0:, lanes])
                off += n

    states = [s_ref[hd] for hd in range(GLA_HEADS)]
    for c in range(nc):
        rows = slice(c * CHUNK, (c + 1) * CHUNK)
        bt = jnp.transpose(b[(c + 1) * CHUNK - 8:(c + 1) * CHUNK, :])
        decay_col = jnp.exp2(bt[:, 7:8])
        for hd in range(GLA_HEADS):
            sb_s[c, hd] = states[hd].astype(BF16)
        for p in range(2):
            lanes = slice(p * PAIR_W, (p + 1) * PAIR_W)
            kdt = jnp.transpose(kd[rows, lanes]).astype(BF16)
            ktv = _dot(kdt, v_s[rows, p * PAIR_V:(p + 1) * PAIR_V])
            dec = decay_col[lanes, :]
            h0, h1 = 2 * p, 2 * p + 1
            states[h0] = states[h0] * dec[0:GLA_DK, :] + ktv[0:GLA_DK, 0:GLA_DV]
            states[h1] = states[h1] * dec[GLA_DK:PAIR_W, :] + ktv[GLA_DK:PAIR_W, GLA_DV:PAIR_V]
    for hd in range(GLA_HEADS):
        s_ref[hd] = states[hd]

    for c in range(nc):
        rows = slice(c * CHUNK, (c + 1) * CHUNK)
        for p in range(2):
            lanes = slice(p * PAIR_W, (p + 1) * PAIR_W)
            h0, h1 = 2 * p, 2 * p + 1
            v0 = v_s[rows, h0 * GLA_DV:(h0 + 1) * GLA_DV]
            v1 = v_s[rows, h1 * GLA_DV:(h1 + 1) * GLA_DV]
            lhs = jnp.concatenate([sc3[c, :, lanes].astype(BF16), qg[rows, lanes]], axis=1)
            w = jnp.concatenate([
                jnp.concatenate([v0, zero_v], axis=1),
                jnp.concatenate([zero_v, v1], axis=1),
                jnp.concatenate([sb_s[c, h0], zero_v], axis=1),
                jnp.concatenate([zero_v, sb_s[c, h1]], axis=1)], axis=0)
            o_s[rows, p * PAIR_V:(p + 1) * PAIR_V] = _dot(lhs, w)

    o = o_s[...]
    gn = gn_ref[...]
    parts = []
    for hd in range(GLA_HEADS):
        oh = o[:, hd * GLA_DV:(hd + 1) * GLA_DV]
        ms = jnp.mean(oh * oh, axis=-1, keepdims=True)
        parts.append(oh * lax.rsqrt(ms + EPS) * gn)
    y_gla = jnp.concatenate(parts, axis=1) * (g_out * jax.nn.sigmoid(g_out))
    y = jnp.concatenate([y_conv, y_gla], axis=1).astype(BF16)
    o_ref[0] = x + gate * _dot(y, wout_ref[...].astype(BF16))


def _mix_call(x, ada3, norm_w, w_in_t, conv_w, w_gk2, b_gk, gla_norm, w_out, *, tt):
    bsz, t, d = x.shape
    nc = tt // CHUNK
    band = _band_masks()
    kern = functools.partial(_mix_kernel, tt=tt)
    const = lambda b, i: (0, 0)
    single = pl.Buffered(1)
    return pl.pallas_call(
        kern,
        grid=(bsz, t // tt),
        in_specs=[
            pl.BlockSpec((1, tt, d), lambda b, i: (b, i, 0)),
            pl.BlockSpec((1, N_ADA, d), lambda b, i: (b, 0, 0)),
            pl.BlockSpec((1, d), const),
            pl.BlockSpec(w_in_t.shape, const, pipeline_mode=single),
            pl.BlockSpec(conv_w.shape, const),
            pl.BlockSpec(w_gk2.shape, const),
            pl.BlockSpec(b_gk.shape, const),
            pl.BlockSpec(gla_norm.shape, const),
            pl.BlockSpec(w_out.shape, const, pipeline_mode=single),
            pl.BlockSpec(band.shape, lambda b, i: (0, 0, 0), pipeline_mode=single),
        ],
        out_specs=pl.BlockSpec((1, tt, d), lambda b, i: (b, i, 0)),
        out_shape=jax.ShapeDtypeStruct(x.shape, F32),
        scratch_shapes=[
            pltpu.VMEM((GLA_HEADS, GLA_DK, GLA_DV), F32),
            pltpu.VMEM((tt + 8, CONV_WIDTH), F32),
            pltpu.VMEM((tt, KEY_W), F32),
            pltpu.VMEM((PAD + tt, KEY_W), F32),
            pltpu.VMEM((nc, 4 * CHUNK, KEY_W), BF16),
            pltpu.VMEM((PAD + tt, KEY_W), F32),
            pltpu.VMEM((tt, VAL_W), BF16),
            pltpu.VMEM((tt, VAL_W), F32),
            pltpu.VMEM((nc, CHUNK, KEY_W), F32),
            pltpu.VMEM((nc, GLA_HEADS, GLA_DK, GLA_DV), BF16),
        ],
        compiler_params=pltpu.CompilerParams(
            dimension_semantics=("arbitrary", "arbitrary"), vmem_limit_bytes=VMEM_LIMIT),
        name="token_mixer",
    )(x, ada3, norm_w, w_in_t, conv_w, w_gk2, b_gk, gla_norm, w_out, band)


def kernel(x, c, w_ada, b_ada, norm_ffn1, w_ffn1_in, w_ffn1_out, norm_mix, w_mix_in, conv_w,
           w_gk2, b_gk, gla_norm, w_mix_out, norm_ffn2, w_ffn2_in, w_ffn2_out, norm_final):
    bsz, t, d = x.shape
    depth = w_ada.shape[0]
    c_pad = jnp.zeros((8, d), F32).at[:bsz].set(c)
    norm_f = norm_final.reshape(1, d)
    for l in range(depth):
        last = l == depth - 1
        ada = _ada_call(c_pad, w_ada[l], b_ada[l].reshape(1, -1))
        ada3 = ada[:bsz].reshape(bsz, N_ADA, d)
        x = _ffn_call(x, ada3, norm_ffn1[l].reshape(1, d), w_ffn1_in[l], w_ffn1_out[l],
                      norm_f, ada_base=0, final_norm=False, tm=512)
        x = _mix_call(x, ada3, norm_mix[l].reshape(1, d), jnp.transpose(w_mix_in[l]),
                      conv_w[l], w_gk2[l], b_gk[l].reshape(1, KEY_W),
                      gla_norm[l].reshape(1, GLA_DV), w_mix_out[l], tt=512)
        x = _ffn_call(x, ada3, norm_ffn2[l].reshape(1, d), w_ffn2_in[l], w_ffn2_out[l],
                      norm_f, ada_base=6, final_norm=last, tm=512)
    return x
```

```python
import functools
import math

import jax
import jax.numpy as jnp
from jax import lax
from jax.experimental import pallas as pl
from jax.experimental.pallas import tpu as pltpu

F32 = jnp.float32
BF16 = jnp.bfloat16

EPS = 1e-6
CONV_WIDTH = 512
CONV_K = 3
GLA_HEADS = 4
GLA_DK = 64
GLA_DV = 128
KEY_W = GLA_HEADS * GLA_DK
VAL_W = GLA_HEADS * GLA_DV
GATE_RANK = 16
GATE_NORMALIZER = 16.0
CHUNK = 64
SUB = 8
NSUB = CHUNK // SUB
N_ADA = 9
LANES = 128
PAIR_W = 2 * GLA_DK
PAIR_V = 2 * GLA_DV
MIX_MAIN = 3 * CONV_WIDTH + 2 * KEY_W + 2 * VAL_W
VMEM_LIMIT = 56 * 1024 * 1024
LOG2E = math.log2(math.e)


def _dot(a, b):
    return jnp.dot(a, b, preferred_element_type=F32)


def _norm_mod(x, gain, shift, scale):
    ms = jnp.mean(x * x, axis=-1, keepdims=True)
    y = x * lax.rsqrt(ms + EPS) * gain
    return y * (1.0 + scale) + shift


def _ada_kernel(c_ref, w_ref, b_ref, o_ref):
    c = c_ref[...]
    ca = (c * jax.nn.sigmoid(c)).astype(BF16)
    o_ref[...] = _dot(ca, w_ref[...].astype(BF16)) + b_ref[...]


def _ada_call(c_pad, w_ada, b_ada):
    rows, d = c_pad.shape
    n = w_ada.shape[1]
    tn = d
    return pl.pallas_call(
        _ada_kernel,
        grid=(n // tn,),
        in_specs=[
            pl.BlockSpec((rows, d), lambda j: (0, 0)),
            pl.BlockSpec((d, tn), lambda j: (0, j)),
            pl.BlockSpec((1, tn), lambda j: (0, j)),
        ],
        out_specs=pl.BlockSpec((rows, tn), lambda j: (0, j)),
        out_shape=jax.ShapeDtypeStruct((rows, n), F32),
        compiler_params=pltpu.CompilerParams(
            dimension_semantics=("arbitrary",), vmem_limit_bytes=VMEM_LIMIT),
        name="ada_proj",
    )(c_pad, w_ada, b_ada)


def _ffn_kernel(x_ref, ada_ref, nw_ref, win_hbm, wout_hbm, nf_ref, o_ref,
                a_ref, win_ref, wout_ref, sem, *, ada_base, chunks, d_ff, final_norm):
    first = jnp.logical_and(pl.program_id(0) == 0, pl.program_id(1) == 0)
    offs = [sum(chunks[:j]) for j in range(len(chunks))]

    def weight_copies(j):
        off, w = offs[j], chunks[j]
        return (
            pltpu.make_async_copy(win_hbm.at[:, off:off + w], win_ref.at[:, off:off + w],
                                  sem.at[0, j]),
            pltpu.make_async_copy(win_hbm.at[:, d_ff + off:d_ff + off + w],
                                  win_ref.at[:, d_ff + off:d_ff + off + w], sem.at[1, j]),
            pltpu.make_async_copy(wout_hbm.at[off:off + w, :], wout_ref.at[off:off + w, :],
                                  sem.at[2, j]),
        )

    def body(wait_for_weights):
        x = x_ref[0]
        shift = ada_ref[0, ada_base:ada_base + 1, :]
        scale = ada_ref[0, ada_base + 1:ada_base + 2, :]
        gate = ada_ref[0, ada_base + 2:ada_base + 3, :]
        h = _norm_mod(x, nw_ref[...], shift, scale).astype(BF16)
        for j, (off, w) in enumerate(zip(offs, chunks)):
            if wait_for_weights:
                weight_copies(j)[0].wait()
                weight_copies(j)[1].wait()
            g = _dot(h, win_ref[:, off:off + w].astype(BF16))
            u = _dot(h, win_ref[:, d_ff + off:d_ff + off + w].astype(BF16))
            a_ref[:, off:off + w] = (g * jax.nn.sigmoid(g) * u).astype(BF16)
        if wait_for_weights:
            for j in range(len(chunks)):
                weight_copies(j)[2].wait()
        y = _dot(a_ref[...], wout_ref[...].astype(BF16))
        out = x + (0.5 * gate) * y
        if final_norm:
            ms = jnp.mean(out * out, axis=-1, keepdims=True)
            out = out * lax.rsqrt(ms + EPS) * nf_ref[...]
        o_ref[0] = out

    @pl.when(first)
    def _():
        for j in range(len(chunks)):
            weight_copies(j)[0].start()
            weight_copies(j)[1].start()
        for j in range(len(chunks)):
            weight_copies(j)[2].start()
        body(True)

    @pl.when(jnp.logical_not(first))
    def _():
        body(False)


def _ffn_call(x, ada3, norm_w, w_in, w_out, norm_f, *, ada_base, final_norm, tm):
    bsz, t, d = x.shape
    d_ff = w_out.shape[0]
    chunks = [512] * (d_ff // 512)
    if d_ff % 512:
        chunks.append(d_ff % 512)
    kern = functools.partial(_ffn_kernel, ada_base=ada_base, chunks=tuple(chunks),
                             d_ff=d_ff, final_norm=final_norm)
    const = lambda b, i: (0, 0)
    return pl.pallas_call(
        kern,
        grid=(bsz, t // tm),
        in_specs=[
            pl.BlockSpec((1, tm, d), lambda b, i: (b, i, 0)),
            pl.BlockSpec((1, N_ADA, d), lambda b, i: (b, 0, 0)),
            pl.BlockSpec((1, d), const),
            pl.BlockSpec(memory_space=pl.ANY),
            pl.BlockSpec(memory_space=pl.ANY),
            pl.BlockSpec((1, d), const),
        ],
        out_specs=pl.BlockSpec((1, tm, d), lambda b, i: (b, i, 0)),
        out_shape=jax.ShapeDtypeStruct(x.shape, F32),
        scratch_shapes=[
            pltpu.VMEM((tm, d_ff), BF16),
            pltpu.VMEM(w_in.shape, F32),
            pltpu.VMEM(w_out.shape, F32),
            pltpu.SemaphoreType.DMA((3, len(chunks))),
        ],
        compiler_params=pltpu.CompilerParams(
            dimension_semantics=("arbitrary", "arbitrary"), vmem_limit_bytes=VMEM_LIMIT),
        name="ffn_final" if final_norm else "ffn",
    )(x, ada3, norm_w, w_in, w_out, norm_f)


def _band_masks():
    i = jnp.arange(CHUNK)[None, :, None]
    j = (jnp.arange(KEY_W) % GLA_DK)[None, None, :]
    d = jnp.arange(SUB)[:, None, None]
    return ((j == i - d) & (i % SUB >= d)).astype(F32)


def _mix_kernel(x_ref, ada_ref, nw_ref, win_ref, convw_ref, wgk2_ref, bgk_ref,
                gn_ref, wout_ref, band_ref, o_ref,
                s_ref, u_ref, q_s, k_s, lgs3, b_s, v_s, o_s, sc3, sb_s, *, tt):
    nc = tt // CHUNK
    t_idx = pl.program_id(1)

    @pl.when(t_idx == 0)
    def _():
        s_ref[...] = jnp.zeros(s_ref.shape, F32)
        u_ref[0:8, :] = jnp.zeros((8, CONV_WIDTH), F32)
        lgs3[:, 3 * CHUNK:4 * CHUNK, :] = jnp.zeros((nc, CHUNK, KEY_W), BF16)

    x = x_ref[0]
    shift = ada_ref[0, 3:4, :]
    scale = ada_ref[0, 4:5, :]
    gate = ada_ref[0, 5:6, :]
    h = _norm_mod(x, nw_ref[...], shift, scale).astype(BF16)

    def proj(lo, width):
        return lax.dot_general(h, win_ref[lo:lo + width, :].astype(BF16),
                               (((1,), (1,)), ((), ())), preferred_element_type=F32)

    tri_r = lax.broadcasted_iota(jnp.int32, (CHUNK, 4 * CHUNK), 0)
    tri_c = lax.broadcasted_iota(jnp.int32, (CHUNK, 4 * CHUNK), 1)
    lower4 = ((tri_c % CHUNK) <= tri_r).astype(BF16)
    hr = lax.broadcasted_iota(jnp.int32, (KEY_W, KEY_W), 0) // GLA_DK
    hc = lax.broadcasted_iota(jnp.int32, (KEY_W, KEY_W), 1) // GLA_DK
    ones_bd = (hr == hc).astype(BF16)
    pr = lax.broadcasted_iota(jnp.int32, (PAIR_W, PAIR_W), 0)
    pc = lax.broadcasted_iota(jnp.int32, (PAIR_W, PAIR_W), 1)
    pair_bd = (pr // GLA_DK) == (pc // GLA_DK)
    col_blk = (lax.broadcasted_iota(jnp.int32, (1, PAIR_W), 1) % GLA_DK) // SUB
    zero_v = jnp.zeros((CHUNK, GLA_DV), BF16)

    base = 3 * CONV_WIDTH
    gk_low = proj(MIX_MAIN, GATE_RANK)
    z = _dot(gk_low.astype(BF16), wgk2_ref[...].astype(BF16)) + bgk_ref[...]
    log_sig = jnp.minimum(z, 0.0) - jnp.log1p(jnp.exp(-jnp.abs(z)))
    lg = log_sig * (LOG2E / GATE_NORMALIZER)
    lg_hi = lg.astype(BF16)
    rem = lg - lg_hi.astype(F32)
    lg_mid = rem.astype(BF16)
    lg_lo = (rem - lg_mid.astype(F32)).astype(BF16)
    lgs3[:, 0:CHUNK, :] = lg_hi.reshape(nc, CHUNK, KEY_W)
    lgs3[:, CHUNK:2 * CHUNK, :] = lg_mid.reshape(nc, CHUNK, KEY_W)
    lgs3[:, 2 * CHUNK:3 * CHUNK, :] = lg_lo.reshape(nc, CHUNK, KEY_W)

    q_s[...] = proj(base, KEY_W) * (GLA_DK ** -0.5)
    k_s[...] = proj(base + KEY_W, KEY_W)
    v_s[...] = proj(base + 2 * KEY_W, VAL_W).astype(BF16)
    for c in range(nc):
        b_s[c * CHUNK:(c + 1) * CHUNK, :] = _dot(lower4, lgs3[c])

    cb = proj(0, CONV_WIDTH)
    u = proj(CONV_WIDTH, CONV_WIDTH) * proj(2 * CONV_WIDTH, CONV_WIDTH)
    u_ref[8:8 + tt, :] = u
    cw = convw_ref[...]
    conv = (u_ref[6:6 + tt, :] * cw[0:1, :] + u_ref[7:7 + tt, :] * cw[1:2, :]
            + u * cw[2:3, :])
    y_conv = (cb * conv).astype(BF16)
    u_ref[0:8, :] = u_ref[tt:tt + 8, :]
    g_out = proj(base + 2 * KEY_W + VAL_W, VAL_W)

    q = q_s[...]
    k = k_s[...]
    b = b_s[...]
    b3 = b.reshape(nc, CHUNK, KEY_W)
    q3 = q.reshape(nc, CHUNK, KEY_W)
    b_last = jnp.broadcast_to(b3[:, CHUNK - 1:CHUNK, :], (nc, CHUNK, KEY_W)).reshape(tt, KEY_W)
    b8 = b.reshape(tt // SUB, SUB, KEY_W)
    b_end = jnp.broadcast_to(b8[:, SUB - 1:SUB, :], (tt // SUB, SUB, KEY_W)).reshape(tt, KEY_W)
    qg = (q * jnp.exp2(b)).astype(BF16)
    kd = k * jnp.exp2(b_last - b)
    kk = k * jnp.exp2(b_end - b)

    q8 = q.reshape(tt // SUB, SUB, KEY_W)
    k8 = k.reshape(tt // SUB, SUB, KEY_W)
    g8 = jnp.exp2(lg).reshape(tt // SUB, SUB, KEY_W)
    s = None
    decay = None
    for d in range(SUB):
        if d == 0:
            e = q8 * k8
        else:
            step = g8 if d == 1 else pltpu.roll(g8, d - 1, axis=1)
            decay = step if decay is None else decay * step
            e = q8 * pltpu.roll(k8, d, axis=1) * decay
        term = (_dot(e.reshape(tt, KEY_W).astype(BF16), ones_bd).reshape(nc, CHUNK, KEY_W)
                * band_ref[d])
        s = term if s is None else s + term
    sc3[...] = s

    qj_all = []
    for j in range(NSUB - 1):
        r0 = SUB * (j + 1)
        qj_all.append((q3[:, r0:, :] * jnp.exp2(b3[:, r0:, :] - b3[:, r0 - 1:r0, :])).astype(BF16))
    for c in range(nc):
        rows = slice(c * CHUNK, (c + 1) * CHUNK)
        for p in range(2):
            lanes = slice(p * PAIR_W, (p + 1) * PAIR_W)
            kp = kk[rows, lanes]
            kbd = jnp.where(pair_bd, jnp.concatenate([kp, kp], axis=0), 0.0).astype(BF16)
            lhs = jnp.concatenate([qj[c, :, lanes] for qj in qj_all], axis=0)
            out = lax.dot_general(lhs, kbd, (((1,), (1,)), ((), ())),
                                  preferred_element_type=F32)
            off = 0
            for j in range(NSUB - 1):
                r0 = SUB * (j + 1)
                n = CHUNK - r0
                sc3[c, r0:, lanes] = jnp.where(col_blk == j, out[off:off + n, :],
                                               sc3[c, r0:, lanes])
                off += n

    states = [s_ref[hd] for hd in range(GLA_HEADS)]
    for c in range(nc):
        rows = slice(c * CHUNK, (c + 1) * CHUNK)
        bt = jnp.transpose(b[(c + 1) * CHUNK - 8:(c + 1) * CHUNK, :])
        decay_col = jnp.exp2(bt[:, 7:8])
        for hd in range(GLA_HEADS):
            sb_s[c, hd] = states[hd].astype(BF16)
        for p in range(2):
            lanes = slice(p * PAIR_W, (p + 1) * PAIR_W)
            kdt = jnp.transpose(kd[rows, lanes]).astype(BF16)
            ktv = _dot(kdt, v_s[rows, p * PAIR_V:(p + 1) * PAIR_V])
            dec = decay_col[lanes, :]
            h0, h1 = 2 * p, 2 * p + 1
            states[h0] = states[h0] * dec[0:GLA_DK, :] + ktv[0:GLA_DK, 0:GLA_DV]
            states[h1] = states[h1] * dec[GLA_DK:PAIR_W, :] + ktv[GLA_DK:PAIR_W, GLA_DV:PAIR_V]
    for hd in range(GLA_HEADS):
        s_ref[hd] = states[hd]

    for c in range(nc):
        rows = slice(c * CHUNK, (c + 1) * CHUNK)
        for p in range(2):
            lanes = slice(p * PAIR_W, (p + 1) * PAIR_W)
            h0, h1 = 2 * p, 2 * p + 1
            v0 = v_s[rows, h0 * GLA_DV:(h0 + 1) * GLA_DV]
            v1 = v_s[rows, h1 * GLA_DV:(h1 + 1) * GLA_DV]
            lhs = jnp.concatenate([sc3[c, :, lanes].astype(BF16), qg[rows, lanes]], axis=1)
            w = jnp.concatenate([
                jnp.concatenate([v0, zero_v], axis=1),
                jnp.concatenate([zero_v, v1], axis=1),
                jnp.concatenate([sb_s[c, h0], zero_v], axis=1),
                jnp.concatenate([zero_v, sb_s[c, h1]], axis=1)], axis=0)
            o_s[rows, p * PAIR_V:(p + 1) * PAIR_V] = _dot(lhs, w)

    wo = wout_ref
    out_conv = _dot(y_conv, wo[0:CONV_WIDTH, :].astype(BF16))
    o = o_s[...]
    gn = gn_ref[...]
    parts = []
    for hd in range(GLA_HEADS):
        oh = o[:, hd * GLA_DV:(hd + 1) * GLA_DV]
        ms = jnp.mean(oh * oh, axis=-1, keepdims=True)
        parts.append(oh * lax.rsqrt(ms + EPS) * gn)
    y_gla = (jnp.concatenate(parts, axis=1) * (g_out * jax.nn.sigmoid(g_out))).astype(BF16)
    out_gla = _dot(y_gla, wo[CONV_WIDTH:CONV_WIDTH + VAL_W, :].astype(BF16))
    o_ref[0] = x + gate * (out_conv + out_gla)


def _mix_call(x, ada3, norm_w, w_in_t, conv_w, w_gk2, b_gk, gla_norm, w_out, *, tt):
    bsz, t, d = x.shape
    nc = tt // CHUNK
    band = _band_masks()
    kern = functools.partial(_mix_kernel, tt=tt)
    const = lambda b, i: (0, 0)
    single = pl.Buffered(1)
    return pl.pallas_call(
        kern,
        grid=(bsz, t // tt),
        in_specs=[
            pl.BlockSpec((1, tt, d), lambda b, i: (b, i, 0)),
            pl.BlockSpec((1, N_ADA, d), lambda b, i: (b, 0, 0)),
            pl.BlockSpec((1, d), const),
            pl.BlockSpec(w_in_t.shape, const, pipeline_mode=single),
            pl.BlockSpec(conv_w.shape, const),
            pl.BlockSpec(w_gk2.shape, const),
            pl.BlockSpec(b_gk.shape, const),
            pl.BlockSpec(gla_norm.shape, const),
            pl.BlockSpec(w_out.shape, const, pipeline_mode=single),
            pl.BlockSpec(band.shape, lambda b, i: (0, 0, 0), pipeline_mode=single),
        ],
        out_specs=pl.BlockSpec((1, tt, d), lambda b, i: (b, i, 0)),
        out_shape=jax.ShapeDtypeStruct(x.shape, F32),
        scratch_shapes=[
            pltpu.VMEM((GLA_HEADS, GLA_DK, GLA_DV), F32),
            pltpu.VMEM((tt + 8, CONV_WIDTH), F32),
            pltpu.VMEM((tt, KEY_W), F32),
            pltpu.VMEM((tt, KEY_W), F32),
            pltpu.VMEM((nc, 4 * CHUNK, KEY_W), BF16),
            pltpu.VMEM((tt, KEY_W), F32),
            pltpu.VMEM((tt, VAL_W), BF16),
            pltpu.VMEM((tt, VAL_W), F32),
            pltpu.VMEM((nc, CHUNK, KEY_W), F32),
            pltpu.VMEM((nc, GLA_HEADS, GLA_DK, GLA_DV), BF16),
        ],
        compiler_params=pltpu.CompilerParams(
            dimension_semantics=("arbitrary", "arbitrary"), vmem_limit_bytes=VMEM_LIMIT),
        name="token_mixer",
    )(x, ada3, norm_w, w_in_t, conv_w, w_gk2, b_gk, gla_norm, w_out, band)


def kernel(x, c, w_ada, b_ada, norm_ffn1, w_ffn1_in, w_ffn1_out, norm_mix, w_mix_in, conv_w,
           w_gk2, b_gk, gla_norm, w_mix_out, norm_ffn2, w_ffn2_in, w_ffn2_out, norm_final):
    bsz, t, d = x.shape
    depth = w_ada.shape[0]
    c_pad = jnp.zeros((8, d), F32).at[:bsz].set(c)
    norm_f = norm_final.reshape(1, d)
    for l in range(depth):
        last = l == depth - 1
        ada = _ada_call(c_pad, w_ada[l], b_ada[l].reshape(1, -1))
        ada3 = ada[:bsz].reshape(bsz, N_ADA, d)
        x = _ffn_call(x, ada3, norm_ffn1[l].reshape(1, d), w_ffn1_in[l], w_ffn1_out[l],
                      norm_f, ada_base=0, final_norm=False, tm=512)
        x = _mix_call(x, ada3, norm_mix[l].reshape(1, d), jnp.transpose(w_mix_in[l]),
                      conv_w[l], w_gk2[l], b_gk[l].reshape(1, KEY_W),
                      gla_norm[l].reshape(1, GLA_DV), w_mix_out[l], tt=512)
        x = _ffn_call(x, ada3, norm_ffn2[l].reshape(1, d), w_ffn2_in[l], w_ffn2_out[l],
                      norm_f, ada_base=6, final_norm=last, tm=512)
    return x
```

```python
import functools
import math

import jax
import jax.numpy as jnp
from jax import lax
from jax.experimental import pallas as pl
from jax.experimental.pallas import tpu as pltpu

F32 = jnp.float32
BF16 = jnp.bfloat16

EPS = 1e-6
CONV_WIDTH = 512
CONV_K = 3
GLA_HEADS = 4
GLA_DK = 64
GLA_DV = 128
KEY_W = GLA_HEADS * GLA_DK
VAL_W = GLA_HEADS * GLA_DV
GATE_RANK = 16
GATE_NORMALIZER = 16.0
CHUNK = 64
SUB = 8
NSUB = CHUNK // SUB
PAD = 8
N_ADA = 9
LANES = 128
PAIR_W = 2 * GLA_DK
PAIR_V = 2 * GLA_DV
MIX_MAIN = 3 * CONV_WIDTH + 2 * KEY_W + 2 * VAL_W
VMEM_LIMIT = 56 * 1024 * 1024
LOG2E = math.log2(math.e)
FFN_CHUNK = 256


def _dot(a, b):
    return jnp.dot(a, b, preferred_element_type=F32)


def _norm_mod(x, gain, shift, scale):
    ms = jnp.mean(x * x, axis=-1, keepdims=True)
    y = x * lax.rsqrt(ms + EPS) * gain
    return y * (1.0 + scale) + shift


def _ada_kernel(c_ref, w_ref, b_ref, o_ref):
    c = c_ref[...]
    ca = (c * jax.nn.sigmoid(c)).astype(BF16)
    o_ref[...] = _dot(ca, w_ref[...].astype(BF16)) + b_ref[...]


def _ada_call(c_pad, w_ada, b_ada):
    rows, d = c_pad.shape
    n = w_ada.shape[1]
    tn = d
    return pl.pallas_call(
        _ada_kernel,
        grid=(n // tn,),
        in_specs=[
            pl.BlockSpec((rows, d), lambda j: (0, 0)),
            pl.BlockSpec((d, tn), lambda j: (0, j)),
            pl.BlockSpec((1, tn), lambda j: (0, j)),
        ],
        out_specs=pl.BlockSpec((rows, tn), lambda j: (0, j)),
        out_shape=jax.ShapeDtypeStruct((rows, n), F32),
        compiler_params=pltpu.CompilerParams(
            dimension_semantics=("arbitrary",), vmem_limit_bytes=VMEM_LIMIT),
        name="ada_proj",
    )(c_pad, w_ada, b_ada)


def _ffn_kernel(x_ref, ada_ref, nw_ref, win_ref, wout_ref, nf_ref, o_ref, a_ref,
                *, ada_base, chunks, d_ff, final_norm):
    x = x_ref[0]
    shift = ada_ref[0, ada_base:ada_base + 1, :]
    scale = ada_ref[0, ada_base + 1:ada_base + 2, :]
    gate = ada_ref[0, ada_base + 2:ada_base + 3, :]
    h = _norm_mod(x, nw_ref[...], shift, scale).astype(BF16)
    off = 0
    for w in chunks:
        g = _dot(h, win_ref[:, off:off + w].astype(BF16))
        u = _dot(h, win_ref[:, d_ff + off:d_ff + off + w].astype(BF16))
        a_ref[:, off:off + w] = (g * jax.nn.sigmoid(g) * u).astype(BF16)
        off += w
    y = _dot(a_ref[...], wout_ref[...].astype(BF16))
    out = x + (0.5 * gate) * y
    if final_norm:
        ms = jnp.mean(out * out, axis=-1, keepdims=True)
        out = out * lax.rsqrt(ms + EPS) * nf_ref[...]
    o_ref[0] = out


def _ffn_call(x, ada3, norm_w, w_in, w_out, norm_f, *, ada_base, final_norm, tm):
    bsz, t, d = x.shape
    d_ff = w_out.shape[0]
    chunks = [FFN_CHUNK] * (d_ff // FFN_CHUNK)
    if d_ff % FFN_CHUNK:
        chunks.append(d_ff % FFN_CHUNK)
    kern = functools.partial(_ffn_kernel, ada_base=ada_base, chunks=tuple(chunks),
                             d_ff=d_ff, final_norm=final_norm)
    const = lambda b, i: (0, 0)
    return pl.pallas_call(
        kern,
        grid=(bsz, t // tm),
        in_specs=[
            pl.BlockSpec((1, tm, d), lambda b, i: (b, i, 0)),
            pl.BlockSpec((1, N_ADA, d), lambda b, i: (b, 0, 0)),
            pl.BlockSpec((1, d), const),
            pl.BlockSpec((d, 2 * d_ff), const, pipeline_mode=pl.Buffered(1)),
            pl.BlockSpec((d_ff, d), const, pipeline_mode=pl.Buffered(1)),
            pl.BlockSpec((1, d), const),
        ],
        out_specs=pl.BlockSpec((1, tm, d), lambda b, i: (b, i, 0)),
        out_shape=jax.ShapeDtypeStruct(x.shape, F32),
        scratch_shapes=[pltpu.VMEM((tm, d_ff), BF16)],
        compiler_params=pltpu.CompilerParams(
            dimension_semantics=("arbitrary", "arbitrary"), vmem_limit_bytes=VMEM_LIMIT),
        name="ffn_final" if final_norm else "ffn",
    )(x, ada3, norm_w, w_in, w_out, norm_f)


def _band_masks():
    i = jnp.arange(CHUNK)[None, :, None]
    j = (jnp.arange(KEY_W) % GLA_DK)[None, None, :]
    d = jnp.arange(SUB)[:, None, None]
    return ((j == i - d) & (i % SUB >= d)).astype(F32)


def _mix_kernel(x_ref, ada_ref, nw_ref, win_ref, convw_ref, wgk2_ref, bgk_ref,
                gn_ref, wout_ref, band_ref, o_ref,
                s_ref, u_ref, q_s, k_s, lgs3, b_s, v_s, o_s, sc3, sb_s, *, tt):
    nc = tt // CHUNK
    t_idx = pl.program_id(1)

    @pl.when(t_idx == 0)
    def _():
        s_ref[...] = jnp.zeros(s_ref.shape, F32)
        u_ref[0:8, :] = jnp.zeros((8, CONV_WIDTH), F32)
        k_s[0:PAD, :] = jnp.zeros((PAD, KEY_W), F32)
        b_s[0:PAD, :] = jnp.zeros((PAD, KEY_W), F32)
        lgs3[:, 3 * CHUNK:4 * CHUNK, :] = jnp.zeros((nc, CHUNK, KEY_W), BF16)

    x = x_ref[0]
    shift = ada_ref[0, 3:4, :]
    scale = ada_ref[0, 4:5, :]
    gate = ada_ref[0, 5:6, :]
    h = _norm_mod(x, nw_ref[...], shift, scale).astype(BF16)

    def proj(lo, width):
        return lax.dot_general(h, win_ref[lo:lo + width, :].astype(BF16),
                               (((1,), (1,)), ((), ())), preferred_element_type=F32)

    tri_r = lax.broadcasted_iota(jnp.int32, (CHUNK, 4 * CHUNK), 0)
    tri_c = lax.broadcasted_iota(jnp.int32, (CHUNK, 4 * CHUNK), 1)
    lower4 = ((tri_c % CHUNK) <= tri_r).astype(BF16)
    hr = lax.broadcasted_iota(jnp.int32, (KEY_W, KEY_W), 0) // GLA_DK
    hc = lax.broadcasted_iota(jnp.int32, (KEY_W, KEY_W), 1) // GLA_DK
    head_bd = hr == hc
    ones_bd = head_bd.astype(BF16)
    col_blk = (lax.broadcasted_iota(jnp.int32, (1, KEY_W), 1) % GLA_DK) // SUB
    zero_v = jnp.zeros((CHUNK, GLA_DV), BF16)

    base = 3 * CONV_WIDTH
    q_s[...] = proj(base, KEY_W) * (GLA_DK ** -0.5)
    k_s[PAD:PAD + tt, :] = proj(base + KEY_W, KEY_W)
    v_s[...] = proj(base + 2 * KEY_W, VAL_W).astype(BF16)
    gk_low = proj(MIX_MAIN, GATE_RANK)
    z = _dot(gk_low.astype(BF16), wgk2_ref[...].astype(BF16)) + bgk_ref[...]
    log_sig = jnp.minimum(z, 0.0) - jnp.log1p(jnp.exp(-jnp.abs(z)))
    lg = log_sig * (LOG2E / GATE_NORMALIZER)
    lg_hi = lg.astype(BF16)
    rem = lg - lg_hi.astype(F32)
    lg_mid = rem.astype(BF16)
    lg_lo = (rem - lg_mid.astype(F32)).astype(BF16)
    lgs3[:, 0:CHUNK, :] = lg_hi.reshape(nc, CHUNK, KEY_W)
    lgs3[:, CHUNK:2 * CHUNK, :] = lg_mid.reshape(nc, CHUNK, KEY_W)
    lgs3[:, 2 * CHUNK:3 * CHUNK, :] = lg_lo.reshape(nc, CHUNK, KEY_W)
    for c in range(nc):
        b_s[PAD + c * CHUNK:PAD + (c + 1) * CHUNK, :] = _dot(lower4, lgs3[c])

    cb = proj(0, CONV_WIDTH)
    u = proj(CONV_WIDTH, CONV_WIDTH) * proj(2 * CONV_WIDTH, CONV_WIDTH)
    u_ref[8:8 + tt, :] = u
    cw = convw_ref[...]
    conv = (u_ref[6:6 + tt, :] * cw[0:1, :] + u_ref[7:7 + tt, :] * cw[1:2, :]
            + u * cw[2:3, :])
    y_conv = cb * conv
    u_ref[0:8, :] = u_ref[tt:tt + 8, :]

    q = q_s[...]
    k = k_s[PAD:PAD + tt, :]
    b = b_s[PAD:PAD + tt, :]
    b3 = b.reshape(nc, CHUNK, KEY_W)
    q3 = q.reshape(nc, CHUNK, KEY_W)
    b_last = jnp.broadcast_to(b3[:, CHUNK - 1:CHUNK, :], (nc, CHUNK, KEY_W)).reshape(tt, KEY_W)
    b8 = b.reshape(tt // SUB, SUB, KEY_W)
    b_end = jnp.broadcast_to(b8[:, SUB - 1:SUB, :], (tt // SUB, SUB, KEY_W)).reshape(tt, KEY_W)
    qg = (q * jnp.exp2(b)).astype(BF16)
    kd = k * jnp.exp2(b_last - b)
    kk = k * jnp.exp2(b_end - b)

    s = None
    for d in range(SUB):
        if d == 0:
            e = q * k
        else:
            e = q * k_s[PAD - d:PAD - d + tt, :] * jnp.exp2(b - b_s[PAD - d:PAD - d + tt, :])
        term = _dot(e.astype(BF16), ones_bd).reshape(nc, CHUNK, KEY_W) * band_ref[d]
        s = term if s is None else s + term
    sc3[...] = s

    g_out = proj(base + 2 * KEY_W + VAL_W, VAL_W)

    qj_all = []
    for j in range(NSUB - 1):
        r0 = SUB * (j + 1)
        qj_all.append((q3[:, r0:, :] * jnp.exp2(b3[:, r0:, :] - b3[:, r0 - 1:r0, :])).astype(BF16))
    for c in range(nc):
        kc = kk[c * CHUNK:(c + 1) * CHUNK, :]
        kbd = jnp.where(head_bd, jnp.concatenate([kc] * GLA_HEADS, axis=0), 0.0).astype(BF16)
        lhs = jnp.concatenate([qj[c] for qj in qj_all], axis=0)
        out = lax.dot_general(lhs, kbd, (((1,), (1,)), ((), ())), preferred_element_type=F32)
        off = 0
        for j in range(NSUB - 1):
            r0 = SUB * (j + 1)
            n = CHUNK - r0
            sc3[c, r0:, :] = jnp.where(col_blk == j, out[off:off + n, :], sc3[c, r0:, :])
            off += n

    states = [s_ref[hd] for hd in range(GLA_HEADS)]
    for c in range(nc):
        rows = slice(c * CHUNK, (c + 1) * CHUNK)
        bt = jnp.transpose(b[(c + 1) * CHUNK - 8:(c + 1) * CHUNK, :])
        decay_col = jnp.exp2(bt[:, 7:8])
        for hd in range(GLA_HEADS):
            sb_s[c, hd] = states[hd].astype(BF16)
        for p in range(2):
            lanes = slice(p * PAIR_W, (p + 1) * PAIR_W)
            kdt = jnp.transpose(kd[rows, lanes]).astype(BF16)
            ktv = _dot(kdt, v_s[rows, p * PAIR_V:(p + 1) * PAIR_V])
            dec = decay_col[lanes, :]
            h0, h1 = 2 * p, 2 * p + 1
            states[h0] = states[h0] * dec[0:GLA_DK, :] + ktv[0:GLA_DK, 0:GLA_DV]
            states[h1] = states[h1] * dec[GLA_DK:PAIR_W, :] + ktv[GLA_DK:PAIR_W, GLA_DV:PAIR_V]
    for hd in range(GLA_HEADS):
        s_ref[hd] = states[hd]

    for c in range(nc):
        rows = slice(c * CHUNK, (c + 1) * CHUNK)
        for p in range(2):
            lanes = slice(p * PAIR_W, (p + 1) * PAIR_W)
            h0, h1 = 2 * p, 2 * p + 1
            v0 = v_s[rows, h0 * GLA_DV:(h0 + 1) * GLA_DV]
            v1 = v_s[rows, h1 * GLA_DV:(h1 + 1) * GLA_DV]
            lhs = jnp.concatenate([sc3[c, :, lanes].astype(BF16), qg[rows, lanes]], axis=1)
            w = jnp.concatenate([
                jnp.concatenate([v0, zero_v], axis=1),
                jnp.concatenate([zero_v, v1], axis=1),
                jnp.concatenate([sb_s[c, h0], zero_v], axis=1),
                jnp.concatenate([zero_v, sb_s[c, h1]], axis=1)], axis=0)
            o_s[rows, p * PAIR_V:(p + 1) * PAIR_V] = _dot(lhs, w)

    o = o_s[...]
    gn = gn_ref[...]
    parts = []
    for hd in range(GLA_HEADS):
        oh = o[:, hd * GLA_DV:(hd + 1) * GLA_DV]
        ms = jnp.mean(oh * oh, axis=-1, keepdims=True)
        parts.append(oh * lax.rsqrt(ms + EPS) * gn)
    y_gla = jnp.concatenate(parts, axis=1) * (g_out * jax.nn.sigmoid(g_out))
    y = jnp.concatenate([y_conv, y_gla], axis=1).astype(BF16)
    o_ref[0] = x + gate * _dot(y, wout_ref[...].astype(BF16))


def _mix_call(x, ada3, norm_w, w_in_t, conv_w, w_gk2, b_gk, gla_norm, w_out, *, tt):
    bsz, t, d = x.shape
    nc = tt // CHUNK
    band = _band_masks()
    kern = functools.partial(_mix_kernel, tt=tt)
    const = lambda b, i: (0, 0)
    single = pl.Buffered(1)
    return pl.pallas_call(
        kern,
        grid=(bsz, t // tt),
        in_specs=[
            pl.BlockSpec((1, tt, d), lambda b, i: (b, i, 0)),
            pl.BlockSpec((1, N_ADA, d), lambda b, i: (b, 0, 0)),
            pl.BlockSpec((1, d), const),
            pl.BlockSpec(w_in_t.shape, const, pipeline_mode=single),
            pl.BlockSpec(conv_w.shape, const),
            pl.BlockSpec(w_gk2.shape, const),
            pl.BlockSpec(b_gk.shape, const),
            pl.BlockSpec(gla_norm.shape, const),
            pl.BlockSpec(w_out.shape, const, pipeline_mode=single),
            pl.BlockSpec(band.shape, lambda b, i: (0, 0, 0), pipeline_mode=single),
        ],
        out_specs=pl.BlockSpec((1, tt, d), lambda b, i: (b, i, 0)),
        out_shape=jax.ShapeDtypeStruct(x.shape, F32),
        scratch_shapes=[
            pltpu.VMEM((GLA_HEADS, GLA_DK, GLA_DV), F32),
            pltpu.VMEM((tt + 8, CONV_WIDTH), F32),
            pltpu.VMEM((tt, KEY_W), F32),
            pltpu.VMEM((PAD + tt, KEY_W), F32),
            pltpu.VMEM((nc, 4 * CHUNK, KEY_W), BF16),
            pltpu.VMEM((PAD + tt, KEY_W), F32),
            pltpu.VMEM((tt, VAL_W), BF16),
            pltpu.VMEM((tt, VAL_W), F32),
            pltpu.VMEM((nc, CHUNK, KEY_W), F32),
            pltpu.VMEM((nc, GLA_HEADS, GLA_DK, GLA_DV), BF16),
        ],
        compiler_params=pltpu.CompilerParams(
            dimension_semantics=("arbitrary", "arbitrary"), vmem_limit_bytes=VMEM_LIMIT),
        name="token_mixer",
    )(x, ada3, norm_w, w_in_t, conv_w, w_gk2, b_gk, gla_norm, w_out, band)


def kernel(x, c, w_ada, b_ada, norm_ffn1, w_ffn1_in, w_ffn1_out, norm_mix, w_mix_in, conv_w,
           w_gk2, b_gk, gla_norm, w_mix_out, norm_ffn2, w_ffn2_in, w_ffn2_out, norm_final):
    bsz, t, d = x.shape
    depth = w_ada.shape[0]
    c_pad = jnp.zeros((8, d), F32).at[:bsz].set(c)
    norm_f = norm_final.reshape(1, d)
    for l in range(depth):
        last = l == depth - 1
        ada = _ada_call(c_pad, w_ada[l], b_ada[l].reshape(1, -1))
        ada3 = ada[:bsz].reshape(bsz, N_ADA, d)
        x = _ffn_call(x, ada3, norm_ffn1[l].reshape(1, d), w_ffn1_in[l], w_ffn1_out[l],
                      norm_f, ada_base=0, final_norm=False, tm=512)
        x = _mix_call(x, ada3, norm_mix[l].reshape(1, d), jnp.transpose(w_mix_in[l]),
                      conv_w[l], w_gk2[l], b_gk[l].reshape(1, KEY_W),
                      gla_norm[l].reshape(1, GLA_DV), w_mix_out[l], tt=512)
        x = _ffn_call(x, ada3, norm_ffn2[l].reshape(1, d), w_ffn2_in[l], w_ffn2_out[l],
                      norm_f, ada_base=6, final_norm=last, tm=512)
    return x
```

```python
import functools
import math

import jax
import jax.numpy as jnp
from jax import lax
from jax.experimental import pallas as pl
from jax.experimental.pallas import tpu as pltpu

F32 = jnp.float32
BF16 = jnp.bfloat16

EPS = 1e-6
CONV_WIDTH = 512
CONV_K = 3
GLA_HEADS = 4
GLA_DK = 64
GLA_DV = 128
KEY_W = GLA_HEADS * GLA_DK
VAL_W = GLA_HEADS * GLA_DV
GATE_RANK = 16
GATE_NORMALIZER = 16.0
CHUNK = 64
SUB = 8
NSUB = CHUNK // SUB
PAD = 8
N_ADA = 9
LANES = 128
PAIR_W = 2 * GLA_DK
PAIR_V = 2 * GLA_DV
MIX_MAIN = 3 * CONV_WIDTH + 2 * KEY_W + 2 * VAL_W
VMEM_LIMIT = 56 * 1024 * 1024
LOG2E = math.log2(math.e)
FFN_CHUNK = 256


def _dot(a, b):
    return jnp.dot(a, b, preferred_element_type=F32)


def _norm_mod(x, gain, shift, scale):
    ms = jnp.mean(x * x, axis=-1, keepdims=True)
    y = x * lax.rsqrt(ms + EPS) * gain
    return y * (1.0 + scale) + shift


def _ada_kernel(c_ref, w_ref, b_ref, o_ref):
    c = c_ref[...]
    ca = (c * jax.nn.sigmoid(c)).astype(BF16)
    o_ref[...] = _dot(ca, w_ref[...].astype(BF16)) + b_ref[...]


def _ada_call(c_pad, w_ada, b_ada):
    rows, d = c_pad.shape
    n = w_ada.shape[1]
    tn = 3 * d
    return pl.pallas_call(
        _ada_kernel,
        grid=(n // tn,),
        in_specs=[
            pl.BlockSpec((rows, d), lambda j: (0, 0)),
            pl.BlockSpec((d, tn), lambda j: (0, j)),
            pl.BlockSpec((1, tn), lambda j: (0, j)),
        ],
        out_specs=pl.BlockSpec((rows, tn), lambda j: (0, j)),
        out_shape=jax.ShapeDtypeStruct((rows, n), F32),
        compiler_params=pltpu.CompilerParams(
            dimension_semantics=("arbitrary",), vmem_limit_bytes=VMEM_LIMIT),
        name="ada_proj",
    )(c_pad, w_ada, b_ada)


def _ffn_kernel(x_ref, ada_ref, nw_ref, win_ref, wout_ref, nf_ref, o_ref, a_ref,
                *, ada_base, chunks, d_ff, final_norm):
    x = x_ref[0]
    shift = ada_ref[0, ada_base:ada_base + 1, :]
    scale = ada_ref[0, ada_base + 1:ada_base + 2, :]
    gate = ada_ref[0, ada_base + 2:ada_base + 3, :]
    h = _norm_mod(x, nw_ref[...], shift, scale).astype(BF16)
    off = 0
    for w in chunks:
        g = _dot(h, win_ref[:, off:off + w].astype(BF16))
        u = _dot(h, win_ref[:, d_ff + off:d_ff + off + w].astype(BF16))
        a_ref[:, off:off + w] = (g * jax.nn.sigmoid(g) * u).astype(BF16)
        off += w
    y = _dot(a_ref[...], wout_ref[...].astype(BF16))
    out = x + (0.5 * gate) * y
    if final_norm:
        ms = jnp.mean(out * out, axis=-1, keepdims=True)
        out = out * lax.rsqrt(ms + EPS) * nf_ref[...]
    o_ref[0] = out


def _ffn_call(x, ada3, norm_w, w_in, w_out, norm_f, *, ada_base, final_norm, tm):
    bsz, t, d = x.shape
    d_ff = w_out.shape[0]
    chunks = [FFN_CHUNK] * (d_ff // FFN_CHUNK)
    if d_ff % FFN_CHUNK:
        chunks.append(d_ff % FFN_CHUNK)
    kern = functools.partial(_ffn_kernel, ada_base=ada_base, chunks=tuple(chunks),
                             d_ff=d_ff, final_norm=final_norm)
    const = lambda b, i: (0, 0)
    return pl.pallas_call(
        kern,
        grid=(bsz, t // tm),
        in_specs=[
            pl.BlockSpec((1, tm, d), lambda b, i: (b, i, 0)),
            pl.BlockSpec((1, N_ADA, d), lambda b, i: (b, 0, 0)),
            pl.BlockSpec((1, d), const),
            pl.BlockSpec((d, 2 * d_ff), const, pipeline_mode=pl.Buffered(1)),
            pl.BlockSpec((d_ff, d), const, pipeline_mode=pl.Buffered(1)),
            pl.BlockSpec((1, d), const),
        ],
        out_specs=pl.BlockSpec((1, tm, d), lambda b, i: (b, i, 0)),
        out_shape=jax.ShapeDtypeStruct(x.shape, F32),
        scratch_shapes=[pltpu.VMEM((tm, d_ff), BF16)],
        compiler_params=pltpu.CompilerParams(
            dimension_semantics=("arbitrary", "arbitrary"), vmem_limit_bytes=VMEM_LIMIT),
        name="ffn_final" if final_norm else "ffn",
    )(x, ada3, norm_w, w_in, w_out, norm_f)


def _band_masks():
    i = jnp.arange(CHUNK)[None, :, None]
    j = (jnp.arange(KEY_W) % GLA_DK)[None, None, :]
    d = jnp.arange(SUB)[:, None, None]
    return ((j == i - d) & (i % SUB >= d)).astype(F32)


def _mix_kernel(x_ref, ada_ref, nw_ref, win_ref, convw_ref, wgk2_ref, bgk_ref,
                gn_ref, wout_ref, band_ref, o_ref,
                s_ref, u_ref, q_s, k_s, lgs3, b_s, v_s, o_s, sc3, sb_s, *, tt):
    nc = tt // CHUNK
    t_idx = pl.program_id(1)

    @pl.when(t_idx == 0)
    def _():
        s_ref[...] = jnp.zeros(s_ref.shape, F32)
        u_ref[0:8, :] = jnp.zeros((8, CONV_WIDTH), F32)
        k_s[:, 0:PAD, :] = jnp.zeros((nc, PAD, KEY_W), F32)
        b_s[:, 0:PAD, :] = jnp.zeros((nc, PAD, KEY_W), F32)
        lgs3[:, 3 * CHUNK:4 * CHUNK, :] = jnp.zeros((nc, CHUNK, KEY_W), BF16)

    x = x_ref[0]
    shift = ada_ref[0, 3:4, :]
    scale = ada_ref[0, 4:5, :]
    gate = ada_ref[0, 5:6, :]
    h = _norm_mod(x, nw_ref[...], shift, scale).astype(BF16)

    def proj(lo, width):
        return lax.dot_general(h, win_ref[lo:lo + width, :].astype(BF16),
                               (((1,), (1,)), ((), ())), preferred_element_type=F32)

    tri_r = lax.broadcasted_iota(jnp.int32, (CHUNK, 4 * CHUNK), 0)
    tri_c = lax.broadcasted_iota(jnp.int32, (CHUNK, 4 * CHUNK), 1)
    lower4 = ((tri_c % CHUNK) <= tri_r).astype(BF16)
    hr = lax.broadcasted_iota(jnp.int32, (KEY_W, KEY_W), 0) // GLA_DK
    hc = lax.broadcasted_iota(jnp.int32, (KEY_W, KEY_W), 1) // GLA_DK
    head_bd = hr == hc
    ones_bd = head_bd.astype(BF16)
    col_blk = (lax.broadcasted_iota(jnp.int32, (1, KEY_W), 1) % GLA_DK) // SUB
    zero_v = jnp.zeros((CHUNK, GLA_DV), BF16)

    base = 3 * CONV_WIDTH
    q_s[...] = proj(base, KEY_W) * (GLA_DK ** -0.5)
    k_s[:, PAD:PAD + CHUNK, :] = proj(base + KEY_W, KEY_W).reshape(nc, CHUNK, KEY_W)
    v_s[...] = proj(base + 2 * KEY_W, VAL_W).astype(BF16)
    gk_low = proj(MIX_MAIN, GATE_RANK)
    z = _dot(gk_low.astype(BF16), wgk2_ref[...].astype(BF16)) + bgk_ref[...]
    log_sig = jnp.minimum(z, 0.0) - jnp.log1p(jnp.exp(-jnp.abs(z)))
    lg = log_sig * (LOG2E / GATE_NORMALIZER)
    lg_hi = lg.astype(BF16)
    rem = lg - lg_hi.astype(F32)
    lg_mid = rem.astype(BF16)
    lg_lo = (rem - lg_mid.astype(F32)).astype(BF16)
    lgs3[:, 0:CHUNK, :] = lg_hi.reshape(nc, CHUNK, KEY_W)
    lgs3[:, CHUNK:2 * CHUNK, :] = lg_mid.reshape(nc, CHUNK, KEY_W)
    lgs3[:, 2 * CHUNK:3 * CHUNK, :] = lg_lo.reshape(nc, CHUNK, KEY_W)
    for c in range(nc):
        b_s[c, PAD:PAD + CHUNK, :] = _dot(lower4, lgs3[c])

    cb = proj(0, CONV_WIDTH)
    u = proj(CONV_WIDTH, CONV_WIDTH) * proj(2 * CONV_WIDTH, CONV_WIDTH)
    u_ref[8:8 + tt, :] = u
    cw = convw_ref[...]
    conv = (u_ref[6:6 + tt, :] * cw[0:1, :] + u_ref[7:7 + tt, :] * cw[1:2, :]
            + u * cw[2:3, :])
    y_conv = cb * conv
    u_ref[0:8, :] = u_ref[tt:tt + 8, :]

    q = q_s[...]
    k = k_s[:, PAD:PAD + CHUNK, :].reshape(tt, KEY_W)
    b = b_s[:, PAD:PAD + CHUNK, :].reshape(tt, KEY_W)
    b3 = b.reshape(nc, CHUNK, KEY_W)
    q3 = q.reshape(nc, CHUNK, KEY_W)
    b_last = jnp.broadcast_to(b3[:, CHUNK - 1:CHUNK, :], (nc, CHUNK, KEY_W)).reshape(tt, KEY_W)
    b8 = b.reshape(tt // SUB, SUB, KEY_W)
    b_end = jnp.broadcast_to(b8[:, SUB - 1:SUB, :], (tt // SUB, SUB, KEY_W)).reshape(tt, KEY_W)
    qg = (q * jnp.exp2(b)).astype(BF16)
    kd = k * jnp.exp2(b_last - b)
    kk = k * jnp.exp2(b_end - b)

    s = None
    for d in range(SUB):
        if d == 0:
            e = q * k
        else:
            k_d = k_s[:, PAD - d:PAD - d + CHUNK, :].reshape(tt, KEY_W)
            b_d = b_s[:, PAD - d:PAD - d + CHUNK, :].reshape(tt, KEY_W)
            e = q * k_d * jnp.exp2(b - b_d)
        term = _dot(e.astype(BF16), ones_bd).reshape(nc, CHUNK, KEY_W) * band_ref[d]
        s = term if s is None else s + term
    sc3[...] = s

    g_out = proj(base + 2 * KEY_W + VAL_W, VAL_W)

    qj_all = []
    for j in range(NSUB - 1):
        r0 = SUB * (j + 1)
        qj_all.append((q3[:, r0:, :] * jnp.exp2(b3[:, r0:, :] - b3[:, r0 - 1:r0, :])).astype(BF16))
    for c in range(nc):
        kc = kk[c * CHUNK:(c + 1) * CHUNK, :]
        kbd = jnp.where(head_bd, jnp.concatenate([kc] * GLA_HEADS, axis=0), 0.0).astype(BF16)
        lhs = jnp.concatenate([qj[c] for qj in qj_all], axis=0)
        out = lax.dot_general(lhs, kbd, (((1,), (1,)), ((), ())), preferred_element_type=F32)
        off = 0
        for j in range(NSUB - 1):
            r0 = SUB * (j + 1)
            n = CHUNK - r0
            sc3[c, r0:, :] = jnp.where(col_blk == j, out[off:off + n, :], sc3[c, r0:, :])
            off += n

    states = [s_ref[hd] for hd in range(GLA_HEADS)]
    for c in range(nc):
        rows = slice(c * CHUNK, (c + 1) * CHUNK)
        bt = jnp.transpose(b[(c + 1) * CHUNK - 8:(c + 1) * CHUNK, :])
        decay_col = jnp.exp2(bt[:, 7:8])
        for hd in range(GLA_HEADS):
            sb_s[c, hd] = states[hd].astype(BF16)
        for p in range(2):
            lanes = slice(p * PAIR_W, (p + 1) * PAIR_W)
            kdt = jnp.transpose(kd[rows, lanes]).astype(BF16)
            ktv = _dot(kdt, v_s[rows, p * PAIR_V:(p + 1) * PAIR_V])
            dec = decay_col[lanes, :]
            h0, h1 = 2 * p, 2 * p + 1
            states[h0] = states[h0] * dec[0:GLA_DK, :] + ktv[0:GLA_DK, 0:GLA_DV]
            states[h1] = states[h1] * dec[GLA_DK:PAIR_W, :] + ktv[GLA_DK:PAIR_W, GLA_DV:PAIR_V]
    for hd in range(GLA_HEADS):
        s_ref[hd] = states[hd]

    for c in range(nc):
        rows = slice(c * CHUNK, (c + 1) * CHUNK)
        for p in range(2):
            lanes = slice(p * PAIR_W, (p + 1) * PAIR_W)
            h0, h1 = 2 * p, 2 * p + 1
            v0 = v_s[rows, h0 * GLA_DV:(h0 + 1) * GLA_DV]
            v1 = v_s[rows, h1 * GLA_DV:(h1 + 1) * GLA_DV]
            lhs = jnp.concatenate([sc3[c, :, lanes].astype(BF16), qg[rows, lanes]], axis=1)
            w = jnp.concatenate([
                jnp.concatenate([v0, zero_v], axis=1),
                jnp.concatenate([zero_v, v1], axis=1),
                jnp.concatenate([sb_s[c, h0], zero_v], axis=1),
                jnp.concatenate([zero_v, sb_s[c, h1]], axis=1)], axis=0)
            o_s[rows, p * PAIR_V:(p + 1) * PAIR_V] = _dot(lhs, w)

    o = o_s[...]
    gn = gn_ref[...]
    parts = []
    for hd in range(GLA_HEADS):
        oh = o[:, hd * GLA_DV:(hd + 1) * GLA_DV]
        ms = jnp.mean(oh * oh, axis=-1, keepdims=True)
        parts.append(oh * lax.rsqrt(ms + EPS) * gn)
    y_gla = jnp.concatenate(parts, axis=1) * (g_out * jax.nn.sigmoid(g_out))
    y = jnp.concatenate([y_conv, y_gla], axis=1).astype(BF16)
    o_ref[0] = x + gate * _dot(y, wout_ref[...].astype(BF16))


def _mix_call(x, ada3, norm_w, w_in_t, conv_w, w_gk2, b_gk, gla_norm, w_out, *, tt):
    bsz, t, d = x.shape
    nc = tt // CHUNK
    band = _band_masks()
    kern = functools.partial(_mix_kernel, tt=tt)
    const = lambda b, i: (0, 0)
    single = pl.Buffered(1)
    return pl.pallas_call(
        kern,
        grid=(bsz, t // tt),
        in_specs=[
            pl.BlockSpec((1, tt, d), lambda b, i: (b, i, 0)),
            pl.BlockSpec((1, N_ADA, d), lambda b, i: (b, 0, 0)),
            pl.BlockSpec((1, d), const),
            pl.BlockSpec(w_in_t.shape, const, pipeline_mode=single),
            pl.BlockSpec(conv_w.shape, const),
            pl.BlockSpec(w_gk2.shape, const),
            pl.BlockSpec(b_gk.shape, const),
            pl.BlockSpec(gla_norm.shape, const),
            pl.BlockSpec(w_out.shape, const, pipeline_mode=single),
            pl.BlockSpec(band.shape, lambda b, i: (0, 0, 0), pipeline_mode=single),
        ],
        out_specs=pl.BlockSpec((1, tt, d), lambda b, i: (b, i, 0)),
        out_shape=jax.ShapeDtypeStruct(x.shape, F32),
        scratch_shapes=[
            pltpu.VMEM((GLA_HEADS, GLA_DK, GLA_DV), F32),
            pltpu.VMEM((tt + 8, CONV_WIDTH), F32),
            pltpu.VMEM((tt, KEY_W), F32),
            pltpu.VMEM((nc, PAD + CHUNK, KEY_W), F32),
            pltpu.VMEM((nc, 4 * CHUNK, KEY_W), BF16),
            pltpu.VMEM((nc, PAD + CHUNK, KEY_W), F32),
            pltpu.VMEM((tt, VAL_W), BF16),
            pltpu.VMEM((tt, VAL_W), F32),
            pltpu.VMEM((nc, CHUNK, KEY_W), F32),
            pltpu.VMEM((nc, GLA_HEADS, GLA_DK, GLA_DV), BF16),
        ],
        compiler_params=pltpu.CompilerParams(
            dimension_semantics=("arbitrary", "arbitrary"), vmem_limit_bytes=VMEM_LIMIT),
        name="token_mixer",
    )(x, ada3, norm_w, w_in_t, conv_w, w_gk2, b_gk, gla_norm, w_out, band)


def kernel(x, c, w_ada, b_ada, norm_ffn1, w_ffn1_in, w_ffn1_out, norm_mix, w_mix_in, conv_w,
           w_gk2, b_gk, gla_norm, w_mix_out, norm_ffn2, w_ffn2_in, w_ffn2_out, norm_final):
    bsz, t, d = x.shape
    depth = w_ada.shape[0]
    c_pad = jnp.zeros((8, d), F32).at[:bsz].set(c)
    norm_f = norm_final.reshape(1, d)
    for l in range(depth):
        last = l == depth - 1
        ada = _ada_call(c_pad, w_ada[l], b_ada[l].reshape(1, -1))
        ada3 = ada[:bsz].reshape(bsz, N_ADA, d)
        x = _ffn_call(x, ada3, norm_ffn1[l].reshape(1, d), w_ffn1_in[l], w_ffn1_out[l],
                      norm_f, ada_base=0, final_norm=False, tm=512)
        x = _mix_call(x, ada3, norm_mix[l].reshape(1, d), jnp.transpose(w_mix_in[l]),
                      conv_w[l], w_gk2[l], b_gk[l].reshape(1, KEY_W),
                      gla_norm[l].reshape(1, GLA_DV), w_mix_out[l], tt=512)
        x = _ffn_call(x, ada3, norm_ffn2[l].reshape(1, d), w_ffn2_in[l], w_ffn2_out[l],
                      norm_f, ada_base=6, final_norm=last, tm=512)
    return x
```

```python
import functools
import math

import jax
import jax.numpy as jnp
from jax import lax
from jax.experimental import pallas as pl
from jax.experimental.pallas import tpu as pltpu

F32 = jnp.float32
BF16 = jnp.bfloat16

EPS = 1e-6
CONV_WIDTH = 512
CONV_K = 3
GLA_HEADS = 4
GLA_DK = 64
GLA_DV = 128
KEY_W = GLA_HEADS * GLA_DK
VAL_W = GLA_HEADS * GLA_DV
GATE_RANK = 16
GATE_NORMALIZER = 16.0
CHUNK = 64
SUB = 8
NSUB = CHUNK // SUB
PAD = 8
N_ADA = 9
LANES = 128
PAIR_W = 2 * GLA_DK
PAIR_V = 2 * GLA_DV
MIX_MAIN = 3 * CONV_WIDTH + 2 * KEY_W + 2 * VAL_W
VMEM_LIMIT = 56 * 1024 * 1024
LOG2E = math.log2(math.e)
FFN_CHUNK = 256
FFN_STAGE_ROWS = 128


def _dot(a, b):
    return jnp.dot(a, b, preferred_element_type=F32)


def _norm_mod(x, gain, shift, scale):
    ms = jnp.mean(x * x, axis=-1, keepdims=True)
    y = x * lax.rsqrt(ms + EPS) * gain
    return y * (1.0 + scale) + shift


def _ada_kernel(c_ref, w_ref, b_ref, o_ref):
    c = c_ref[...]
    ca = (c * jax.nn.sigmoid(c)).astype(BF16)
    o_ref[...] = _dot(ca, w_ref[...].astype(BF16)) + b_ref[...]


def _ada_call(c_pad, w_ada, b_ada):
    rows, d = c_pad.shape
    n = w_ada.shape[1]
    tn = 3 * d
    return pl.pallas_call(
        _ada_kernel,
        grid=(n // tn,),
        in_specs=[
            pl.BlockSpec((rows, d), lambda j: (0, 0)),
            pl.BlockSpec((d, tn), lambda j: (0, j)),
            pl.BlockSpec((1, tn), lambda j: (0, j)),
        ],
        out_specs=pl.BlockSpec((rows, tn), lambda j: (0, j)),
        out_shape=jax.ShapeDtypeStruct((rows, n), F32),
        compiler_params=pltpu.CompilerParams(
            dimension_semantics=("arbitrary",), vmem_limit_bytes=VMEM_LIMIT),
        name="ada_proj",
    )(c_pad, w_ada, b_ada)


def _load_as_bf16(src_hbm, dst_ref, stage, sems):
    rows = stage.shape[1]
    n_chunks = src_hbm.shape[0] // rows

    def chunk_copy(j):
        return pltpu.make_async_copy(src_hbm.at[j * rows:(j + 1) * rows, :], stage.at[j % 2],
                                     sems.at[j % 2])

    chunk_copy(0).start()
    for j in range(n_chunks):
        if j + 1 < n_chunks:
            chunk_copy(j + 1).start()
        chunk_copy(j).wait()
        dst_ref[j * rows:(j + 1) * rows, :] = stage[j % 2].astype(BF16)


def _ffn_kernel(x_ref, ada_ref, nw_ref, win_hbm, wout_hbm, nf_ref, o_ref,
                a_ref, win_ref, wout_ref, stage_in, stage_out, sem,
                *, ada_base, chunks, d_ff, final_norm):
    @pl.when(jnp.logical_and(pl.program_id(0) == 0, pl.program_id(1) == 0))
    def _():
        _load_as_bf16(win_hbm, win_ref, stage_in, sem.at[0])
        _load_as_bf16(wout_hbm, wout_ref, stage_out, sem.at[1])

    x = x_ref[0]
    shift = ada_ref[0, ada_base:ada_base + 1, :]
    scale = ada_ref[0, ada_base + 1:ada_base + 2, :]
    gate = ada_ref[0, ada_base + 2:ada_base + 3, :]
    h = _norm_mod(x, nw_ref[...], shift, scale).astype(BF16)
    off = 0
    for w in chunks:
        g = _dot(h, win_ref[:, off:off + w])
        u = _dot(h, win_ref[:, d_ff + off:d_ff + off + w])
        a_ref[:, off:off + w] = (g * jax.nn.sigmoid(g) * u).astype(BF16)
        off += w
    y = _dot(a_ref[...], wout_ref[...])
    out = x + (0.5 * gate) * y
    if final_norm:
        ms = jnp.mean(out * out, axis=-1, keepdims=True)
        out = out * lax.rsqrt(ms + EPS) * nf_ref[...]
    o_ref[0] = out


def _ffn_call(x, ada3, norm_w, w_in, w_out, norm_f, *, ada_base, final_norm, tm):
    bsz, t, d = x.shape
    d_ff = w_out.shape[0]
    chunks = [FFN_CHUNK] * (d_ff // FFN_CHUNK)
    if d_ff % FFN_CHUNK:
        chunks.append(d_ff % FFN_CHUNK)
    kern = functools.partial(_ffn_kernel, ada_base=ada_base, chunks=tuple(chunks),
                             d_ff=d_ff, final_norm=final_norm)
    const = lambda b, i: (0, 0)
    return pl.pallas_call(
        kern,
        grid=(bsz, t // tm),
        in_specs=[
            pl.BlockSpec((1, tm, d), lambda b, i: (b, i, 0)),
            pl.BlockSpec((1, N_ADA, d), lambda b, i: (b, 0, 0)),
            pl.BlockSpec((1, d), const),
            pl.BlockSpec(memory_space=pl.ANY),
            pl.BlockSpec(memory_space=pl.ANY),
            pl.BlockSpec((1, d), const),
        ],
        out_specs=pl.BlockSpec((1, tm, d), lambda b, i: (b, i, 0)),
        out_shape=jax.ShapeDtypeStruct(x.shape, F32),
        scratch_shapes=[
            pltpu.VMEM((tm, d_ff), BF16),
            pltpu.VMEM(w_in.shape, BF16),
            pltpu.VMEM(w_out.shape, BF16),
            pltpu.VMEM((2, FFN_STAGE_ROWS, w_in.shape[1]), F32),
            pltpu.VMEM((2, FFN_STAGE_ROWS, w_out.shape[1]), F32),
            pltpu.SemaphoreType.DMA((2, 2)),
        ],
        compiler_params=pltpu.CompilerParams(
            dimension_semantics=("arbitrary", "arbitrary"), vmem_limit_bytes=VMEM_LIMIT),
        name="ffn_final" if final_norm else "ffn",
    )(x, ada3, norm_w, w_in, w_out, norm_f)


def _band_masks():
    i = jnp.arange(CHUNK)[None, :, None]
    j = (jnp.arange(KEY_W) % GLA_DK)[None, None, :]
    d = jnp.arange(SUB)[:, None, None]
    return ((j == i - d) & (i % SUB >= d)).astype(F32)


def _mix_kernel(x_ref, ada_ref, nw_ref, win_ref, convw_ref, wgk2_ref, bgk_ref,
                gn_ref, wout_ref, band_ref, o_ref,
                s_ref, u_ref, q_s, k_s, lgs3, b_s, v_s, o_s, sc3, sb_s, *, tt):
    nc = tt // CHUNK
    t_idx = pl.program_id(1)

    @pl.when(t_idx == 0)
    def _():
        s_ref[...] = jnp.zeros(s_ref.shape, F32)
        u_ref[0:8, :] = jnp.zeros((8, CONV_WIDTH), F32)
        k_s[:, 0:PAD, :] = jnp.zeros((nc, PAD, KEY_W), F32)
        b_s[:, 0:PAD, :] = jnp.zeros((nc, PAD, KEY_W), F32)
        lgs3[:, 3 * CHUNK:4 * CHUNK, :] = jnp.zeros((nc, CHUNK, KEY_W), BF16)

    x = x_ref[0]
    shift = ada_ref[0, 3:4, :]
    scale = ada_ref[0, 4:5, :]
    gate = ada_ref[0, 5:6, :]
    h = _norm_mod(x, nw_ref[...], shift, scale).astype(BF16)

    def proj(lo, width):
        return lax.dot_general(h, win_ref[lo:lo + width, :].astype(BF16),
                               (((1,), (1,)), ((), ())), preferred_element_type=F32)

    tri_r = lax.broadcasted_iota(jnp.int32, (CHUNK, 4 * CHUNK), 0)
    tri_c = lax.broadcasted_iota(jnp.int32, (CHUNK, 4 * CHUNK), 1)
    lower4 = ((tri_c % CHUNK) <= tri_r).astype(BF16)
    hr = lax.broadcasted_iota(jnp.int32, (KEY_W, KEY_W), 0) // GLA_DK
    hc = lax.broadcasted_iota(jnp.int32, (KEY_W, KEY_W), 1) // GLA_DK
    head_bd = hr == hc
    ones_bd = head_bd.astype(BF16)
    col_blk = (lax.broadcasted_iota(jnp.int32, (1, KEY_W), 1) % GLA_DK) // SUB
    zero_v = jnp.zeros((CHUNK, GLA_DV), BF16)

    base = 3 * CONV_WIDTH
    q_s[...] = proj(base, KEY_W) * (GLA_DK ** -0.5)
    k_s[:, PAD:PAD + CHUNK, :] = proj(base + KEY_W, KEY_W).reshape(nc, CHUNK, KEY_W)
    v_s[...] = proj(base + 2 * KEY_W, VAL_W).astype(BF16)
    gk_low = proj(MIX_MAIN, GATE_RANK)
    z = _dot(gk_low.astype(BF16), wgk2_ref[...].astype(BF16)) + bgk_ref[...]
    log_sig = jnp.minimum(z, 0.0) - jnp.log1p(jnp.exp(-jnp.abs(z)))
    lg = log_sig * (LOG2E / GATE_NORMALIZER)
    lg_hi = lg.astype(BF16)
    rem = lg - lg_hi.astype(F32)
    lg_mid = rem.astype(BF16)
    lg_lo = (rem - lg_mid.astype(F32)).astype(BF16)
    lgs3[:, 0:CHUNK, :] = lg_hi.reshape(nc, CHUNK, KEY_W)
    lgs3[:, CHUNK:2 * CHUNK, :] = lg_mid.reshape(nc, CHUNK, KEY_W)
    lgs3[:, 2 * CHUNK:3 * CHUNK, :] = lg_lo.reshape(nc, CHUNK, KEY_W)
    for c in range(nc):
        b_s[c, PAD:PAD + CHUNK, :] = _dot(lower4, lgs3[c])

    cb = proj(0, CONV_WIDTH)
    u = proj(CONV_WIDTH, CONV_WIDTH) * proj(2 * CONV_WIDTH, CONV_WIDTH)
    u_ref[8:8 + tt, :] = u
    cw = convw_ref[...]
    conv = (u_ref[6:6 + tt, :] * cw[0:1, :] + u_ref[7:7 + tt, :] * cw[1:2, :]
            + u * cw[2:3, :])
    y_conv = cb * conv
    u_ref[0:8, :] = u_ref[tt:tt + 8, :]

    q = q_s[...]
    k = k_s[:, PAD:PAD + CHUNK, :].reshape(tt, KEY_W)
    b = b_s[:, PAD:PAD + CHUNK, :].reshape(tt, KEY_W)
    b3 = b.reshape(nc, CHUNK, KEY_W)
    q3 = q.reshape(nc, CHUNK, KEY_W)
    b_last = jnp.broadcast_to(b3[:, CHUNK - 1:CHUNK, :], (nc, CHUNK, KEY_W)).reshape(tt, KEY_W)
    b8 = b.reshape(tt // SUB, SUB, KEY_W)
    b_end = jnp.broadcast_to(b8[:, SUB - 1:SUB, :], (tt // SUB, SUB, KEY_W)).reshape(tt, KEY_W)
    qg = (q * jnp.exp2(b)).astype(BF16)
    kd = k * jnp.exp2(b_last - b)
    kk = k * jnp.exp2(b_end - b)

    s = None
    for d in range(SUB):
        if d == 0:
            e = q * k
        else:
            k_d = k_s[:, PAD - d:PAD - d + CHUNK, :].reshape(tt, KEY_W)
            b_d = b_s[:, PAD - d:PAD - d + CHUNK, :].reshape(tt, KEY_W)
            e = q * k_d * jnp.exp2(b - b_d)
        term = _dot(e.astype(BF16), ones_bd).reshape(nc, CHUNK, KEY_W) * band_ref[d]
        s = term if s is None else s + term
    sc3[...] = s

    g_out = proj(base + 2 * KEY_W + VAL_W, VAL_W)

    qj_all = []
    for j in range(NSUB - 1):
        r0 = SUB * (j + 1)
        qj_all.append((q3[:, r0:, :] * jnp.exp2(b3[:, r0:, :] - b3[:, r0 - 1:r0, :])).astype(BF16))
    for c in range(nc):
        kc = kk[c * CHUNK:(c + 1) * CHUNK, :]
        kbd = jnp.where(head_bd, jnp.concatenate([kc] * GLA_HEADS, axis=0), 0.0).astype(BF16)
        lhs = jnp.concatenate([qj[c] for qj in qj_all], axis=0)
        out = lax.dot_general(lhs, kbd, (((1,), (1,)), ((), ())), preferred_element_type=F32)
        off = 0
        for j in range(NSUB - 1):
            r0 = SUB * (j + 1)
            n = CHUNK - r0
            sc3[c, r0:, :] = jnp.where(col_blk == j, out[off:off + n, :], sc3[c, r0:, :])
            off += n

    states = [s_ref[hd] for hd in range(GLA_HEADS)]
    for c in range(nc):
        rows = slice(c * CHUNK, (c + 1) * CHUNK)
        bt = jnp.transpose(b[(c + 1) * CHUNK - 8:(c + 1) * CHUNK, :])
        decay_col = jnp.exp2(bt[:, 7:8])
        for hd in range(GLA_HEADS):
            sb_s[c, hd] = states[hd].astype(BF16)
        for p in range(2):
            lanes = slice(p * PAIR_W, (p + 1) * PAIR_W)
            kdt = jnp.transpose(kd[rows, lanes]).astype(BF16)
            ktv = _dot(kdt, v_s[rows, p * PAIR_V:(p + 1) * PAIR_V])
            dec = decay_col[lanes, :]
            h0, h1 = 2 * p, 2 * p + 1
            states[h0] = states[h0] * dec[0:GLA_DK, :] + ktv[0:GLA_DK, 0:GLA_DV]
            states[h1] = states[h1] * dec[GLA_DK:PAIR_W, :] + ktv[GLA_DK:PAIR_W, GLA_DV:PAIR_V]
    for hd in range(GLA_HEADS):
        s_ref[hd] = states[hd]

    for c in range(nc):
        rows = slice(c * CHUNK, (c + 1) * CHUNK)
        for p in range(2):
            lanes = slice(p * PAIR_W, (p + 1) * PAIR_W)
            h0, h1 = 2 * p, 2 * p + 1
            v0 = v_s[rows, h0 * GLA_DV:(h0 + 1) * GLA_DV]
            v1 = v_s[rows, h1 * GLA_DV:(h1 + 1) * GLA_DV]
            lhs = jnp.concatenate([sc3[c, :, lanes].astype(BF16), qg[rows, lanes]], axis=1)
            w = jnp.concatenate([
                jnp.concatenate([v0, zero_v], axis=1),
                jnp.concatenate([zero_v, v1], axis=1),
                jnp.concatenate([sb_s[c, h0], zero_v], axis=1),
                jnp.concatenate([zero_v, sb_s[c, h1]], axis=1)], axis=0)
            o_s[rows, p * PAIR_V:(p + 1) * PAIR_V] = _dot(lhs, w)

    o = o_s[...]
    gn = gn_ref[...]
    parts = []
    for hd in range(GLA_HEADS):
        oh = o[:, hd * GLA_DV:(hd + 1) * GLA_DV]
        ms = jnp.mean(oh * oh, axis=-1, keepdims=True)
        parts.append(oh * lax.rsqrt(ms + EPS) * gn)
    y_gla = jnp.concatenate(parts, axis=1) * (g_out * jax.nn.sigmoid(g_out))
    y = jnp.concatenate([y_conv, y_gla], axis=1).astype(BF16)
    o_ref[0] = x + gate * _dot(y, wout_ref[...].astype(BF16))


def _mix_call(x, ada3, norm_w, w_in_t, conv_w, w_gk2, b_gk, gla_norm, w_out, *, tt):
    bsz, t, d = x.shape
    nc = tt // CHUNK
    band = _band_masks()
    kern = functools.partial(_mix_kernel, tt=tt)
    const = lambda b, i: (0, 0)
    single = pl.Buffered(1)
    return pl.pallas_call(
        kern,
        grid=(bsz, t // tt),
        in_specs=[
            pl.BlockSpec((1, tt, d), lambda b, i: (b, i, 0)),
            pl.BlockSpec((1, N_ADA, d), lambda b, i: (b, 0, 0)),
            pl.BlockSpec((1, d), const),
            pl.BlockSpec(w_in_t.shape, const, pipeline_mode=single),
            pl.BlockSpec(conv_w.shape, const),
            pl.BlockSpec(w_gk2.shape, const),
            pl.BlockSpec(b_gk.shape, const),
            pl.BlockSpec(gla_norm.shape, const),
            pl.BlockSpec(w_out.shape, const, pipeline_mode=single),
            pl.BlockSpec(band.shape, lambda b, i: (0, 0, 0), pipeline_mode=single),
        ],
        out_specs=pl.BlockSpec((1, tt, d), lambda b, i: (b, i, 0)),
        out_shape=jax.ShapeDtypeStruct(x.shape, F32),
        scratch_shapes=[
            pltpu.VMEM((GLA_HEADS, GLA_DK, GLA_DV), F32),
            pltpu.VMEM((tt + 8, CONV_WIDTH), F32),
            pltpu.VMEM((tt, KEY_W), F32),
            pltpu.VMEM((nc, PAD + CHUNK, KEY_W), F32),
            pltpu.VMEM((nc, 4 * CHUNK, KEY_W), BF16),
            pltpu.VMEM((nc, PAD + CHUNK, KEY_W), F32),
            pltpu.VMEM((tt, VAL_W), BF16),
            pltpu.VMEM((tt, VAL_W), F32),
            pltpu.VMEM((nc, CHUNK, KEY_W), F32),
            pltpu.VMEM((nc, GLA_HEADS, GLA_DK, GLA_DV), BF16),
        ],
        compiler_params=pltpu.CompilerParams(
            dimension_semantics=("arbitrary", "arbitrary"), vmem_limit_bytes=VMEM_LIMIT),
        name="token_mixer",
    )(x, ada3, norm_w, w_in_t, conv_w, w_gk2, b_gk, gla_norm, w_out, band)


def kernel(x, c, w_ada, b_ada, norm_ffn1, w_ffn1_in, w_ffn1_out, norm_mix, w_mix_in, conv_w,
           w_gk2, b_gk, gla_norm, w_mix_out, norm_ffn2, w_ffn2_in, w_ffn2_out, norm_final):
    bsz, t, d = x.shape
    depth = w_ada.shape[0]
    c_pad = jnp.zeros((8, d), F32).at[:bsz].set(c)
    norm_f = norm_final.reshape(1, d)
    for l in range(depth):
        last = l == depth - 1
        ada = _ada_call(c_pad, w_ada[l], b_ada[l].reshape(1, -1))
        ada3 = ada[:bsz].reshape(bsz, N_ADA, d)
        x = _ffn_call(x, ada3, norm_ffn1[l].reshape(1, d), w_ffn1_in[l], w_ffn1_out[l],
                      norm_f, ada_base=0, final_norm=False, tm=1024)
        x = _mix_call(x, ada3, norm_mix[l].reshape(1, d), jnp.transpose(w_mix_in[l]),
                      conv_w[l], w_gk2[l], b_gk[l].reshape(1, KEY_W),
                      gla_norm[l].reshape(1, GLA_DV), w_mix_out[l], tt=512)
        x = _ffn_call(x, ada3, norm_ffn2[l].reshape(1, d), w_ffn2_in[l], w_ffn2_out[l],
                      norm_f, ada_base=6, final_norm=last, tm=1024)
    return x
```

```python
import functools
import math

import jax
import jax.numpy as jnp
from jax import lax
from jax.experimental import pallas as pl
from jax.experimental.pallas import tpu as pltpu

F32 = jnp.float32
BF16 = jnp.bfloat16

EPS = 1e-6
CONV_WIDTH = 512
CONV_K = 3
GLA_HEADS = 4
GLA_DK = 64
GLA_DV = 128
KEY_W = GLA_HEADS * GLA_DK
VAL_W = GLA_HEADS * GLA_DV
GATE_RANK = 16
GATE_NORMALIZER = 16.0
CHUNK = 64
SUB = 8
NSUB = CHUNK // SUB
PAD = 8
N_ADA = 9
LANES = 128
PAIR_W = 2 * GLA_DK
PAIR_V = 2 * GLA_DV
MIX_MAIN = 3 * CONV_WIDTH + 2 * KEY_W + 2 * VAL_W
VMEM_LIMIT = 56 * 1024 * 1024
LOG2E = math.log2(math.e)
FFN_CHUNK = 256
FFN_STAGE_ROWS = 64
FFN_STAGE_AHEAD = 3
FFN_STAGE_SPLIT = 2


def _dot(a, b):
    return jnp.dot(a, b, preferred_element_type=F32)


def _norm_mod(x, gain, shift, scale):
    ms = jnp.mean(x * x, axis=-1, keepdims=True)
    y = x * lax.rsqrt(ms + EPS) * gain
    return y * (1.0 + scale) + shift


def _ada_kernel(c_ref, w_ref, b_ref, o_ref):
    c = c_ref[...]
    ca = (c * jax.nn.sigmoid(c)).astype(BF16)
    o_ref[...] = _dot(ca, w_ref[...].astype(BF16)) + b_ref[...]


def _ada_call(c_pad, w_ada, b_ada):
    rows, d = c_pad.shape
    n = w_ada.shape[1]
    tn = 3 * d
    return pl.pallas_call(
        _ada_kernel,
        grid=(n // tn,),
        in_specs=[
            pl.BlockSpec((rows, d), lambda j: (0, 0)),
            pl.BlockSpec((d, tn), lambda j: (0, j)),
            pl.BlockSpec((1, tn), lambda j: (0, j)),
        ],
        out_specs=pl.BlockSpec((rows, tn), lambda j: (0, j)),
        out_shape=jax.ShapeDtypeStruct((rows, n), F32),
        compiler_params=pltpu.CompilerParams(
            dimension_semantics=("arbitrary",), vmem_limit_bytes=VMEM_LIMIT),
        name="ada_proj",
    )(c_pad, w_ada, b_ada)


def _load_as_bf16(src_hbm, dst_ref, stage, sems):
    slots, rows = stage.shape[0], stage.shape[1]
    part = rows // FFN_STAGE_SPLIT
    n_chunks = src_hbm.shape[0] // rows

    def chunk_copies(j):
        s = j % slots
        return [pltpu.make_async_copy(src_hbm.at[j * rows + p * part:j * rows + (p + 1) * part, :],
                                      stage.at[s, p * part:(p + 1) * part, :], sems.at[s, p])
                for p in range(FFN_STAGE_SPLIT)]

    for j in range(min(FFN_STAGE_AHEAD, n_chunks)):
        for cp in chunk_copies(j):
            cp.start()
    for j in range(n_chunks):
        if j + FFN_STAGE_AHEAD < n_chunks:
            for cp in chunk_copies(j + FFN_STAGE_AHEAD):
                cp.start()
        for cp in chunk_copies(j):
            cp.wait()
        dst_ref[j * rows:(j + 1) * rows, :] = stage[j % slots].astype(BF16)


def _ffn_kernel(x_ref, ada_ref, nw_ref, win_hbm, wout_hbm, nf_ref, o_ref,
                a_ref, win_ref, wout_ref, stage_in, stage_out, sem,
                *, ada_base, chunks, d_ff, final_norm):
    @pl.when(jnp.logical_and(pl.program_id(0) == 0, pl.program_id(1) == 0))
    def _():
        _load_as_bf16(win_hbm, win_ref, stage_in, sem.at[0])
        _load_as_bf16(wout_hbm, wout_ref, stage_out, sem.at[1])

    x = x_ref[0]
    shift = ada_ref[0, ada_base:ada_base + 1, :]
    scale = ada_ref[0, ada_base + 1:ada_base + 2, :]
    gate = ada_ref[0, ada_base + 2:ada_base + 3, :]
    h = _norm_mod(x, nw_ref[...], shift, scale).astype(BF16)
    off = 0
    for w in chunks:
        g = _dot(h, win_ref[:, off:off + w])
        u = _dot(h, win_ref[:, d_ff + off:d_ff + off + w])
        a_ref[:, off:off + w] = (g * jax.nn.sigmoid(g) * u).astype(BF16)
        off += w
    y = _dot(a_ref[...], wout_ref[...])
    out = x + (0.5 * gate) * y
    if final_norm:
        ms = jnp.mean(out * out, axis=-1, keepdims=True)
        out = out * lax.rsqrt(ms + EPS) * nf_ref[...]
    o_ref[0] = out


def _ffn_call(x, ada3, norm_w, w_in, w_out, norm_f, *, ada_base, final_norm, tm):
    bsz, t, d = x.shape
    d_ff = w_out.shape[0]
    chunks = [FFN_CHUNK] * (d_ff // FFN_CHUNK)
    if d_ff % FFN_CHUNK:
        chunks.append(d_ff % FFN_CHUNK)
    kern = functools.partial(_ffn_kernel, ada_base=ada_base, chunks=tuple(chunks),
                             d_ff=d_ff, final_norm=final_norm)
    const = lambda b, i: (0, 0)
    return pl.pallas_call(
        kern,
        grid=(bsz, t // tm),
        in_specs=[
            pl.BlockSpec((1, tm, d), lambda b, i: (b, i, 0)),
            pl.BlockSpec((1, N_ADA, d), lambda b, i: (b, 0, 0)),
            pl.BlockSpec((1, d), const),
            pl.BlockSpec(memory_space=pl.ANY),
            pl.BlockSpec(memory_space=pl.ANY),
            pl.BlockSpec((1, d), const),
        ],
        out_specs=pl.BlockSpec((1, tm, d), lambda b, i: (b, i, 0)),
        out_shape=jax.ShapeDtypeStruct(x.shape, F32),
        scratch_shapes=[
            pltpu.VMEM((tm, d_ff), BF16),
            pltpu.VMEM(w_in.shape, BF16),
            pltpu.VMEM(w_out.shape, BF16),
            pltpu.VMEM((FFN_STAGE_AHEAD + 1, FFN_STAGE_ROWS, w_in.shape[1]), F32),
            pltpu.VMEM((FFN_STAGE_AHEAD + 1, 4 * FFN_STAGE_ROWS, w_out.shape[1]), F32),
            pltpu.SemaphoreType.DMA((2, FFN_STAGE_AHEAD + 1, FFN_STAGE_SPLIT)),
        ],
        compiler_params=pltpu.CompilerParams(
            dimension_semantics=("arbitrary", "arbitrary"), vmem_limit_bytes=VMEM_LIMIT),
        name="ffn_final" if final_norm else "ffn",
    )(x, ada3, norm_w, w_in, w_out, norm_f)


def _band_masks():
    i = jnp.arange(CHUNK)[None, :, None]
    j = (jnp.arange(KEY_W) % GLA_DK)[None, None, :]
    d = jnp.arange(SUB)[:, None, None]
    return ((j == i - d) & (i % SUB >= d)).astype(F32)


def _mix_kernel(x_ref, ada_ref, nw_ref, win_ref, convw_ref, wgk2_ref, bgk_ref,
                gn_ref, wout_ref, band_ref, o_ref,
                s_ref, u_ref, q_s, k_s, lgs3, b_s, v_s, o_s, sc3, sb_s, *, tt):
    nc = tt // CHUNK
    t_idx = pl.program_id(1)

    @pl.when(t_idx == 0)
    def _():
        s_ref[...] = jnp.zeros(s_ref.shape, F32)
        u_ref[0:8, :] = jnp.zeros((8, CONV_WIDTH), F32)
        k_s[:, 0:PAD, :] = jnp.zeros((nc, PAD, KEY_W), F32)
        b_s[:, 0:PAD, :] = jnp.zeros((nc, PAD, KEY_W), F32)
        lgs3[:, 3 * CHUNK:4 * CHUNK, :] = jnp.zeros((nc, CHUNK, KEY_W), BF16)

    x = x_ref[0]
    shift = ada_ref[0, 3:4, :]
    scale = ada_ref[0, 4:5, :]
    gate = ada_ref[0, 5:6, :]
    h = _norm_mod(x, nw_ref[...], shift, scale).astype(BF16)

    def proj(lo, width):
        return lax.dot_general(h, win_ref[lo:lo + width, :].astype(BF16),
                               (((1,), (1,)), ((), ())), preferred_element_type=F32)

    tri_r = lax.broadcasted_iota(jnp.int32, (CHUNK, 4 * CHUNK), 0)
    tri_c = lax.broadcasted_iota(jnp.int32, (CHUNK, 4 * CHUNK), 1)
    lower4 = ((tri_c % CHUNK) <= tri_r).astype(BF16)
    hr = lax.broadcasted_iota(jnp.int32, (KEY_W, KEY_W), 0) // GLA_DK
    hc = lax.broadcasted_iota(jnp.int32, (KEY_W, KEY_W), 1) // GLA_DK
    head_bd = hr == hc
    ones_bd = head_bd.astype(BF16)
    col_blk = (lax.broadcasted_iota(jnp.int32, (1, KEY_W), 1) % GLA_DK) // SUB
    zero_v = jnp.zeros((CHUNK, GLA_DV), BF16)

    base = 3 * CONV_WIDTH
    q_s[...] = proj(base, KEY_W) * (GLA_DK ** -0.5)
    k_s[:, PAD:PAD + CHUNK, :] = proj(base + KEY_W, KEY_W).reshape(nc, CHUNK, KEY_W)
    v_s[...] = proj(base + 2 * KEY_W, VAL_W).astype(BF16)
    gk_low = proj(MIX_MAIN, GATE_RANK)
    z = _dot(gk_low.astype(BF16), wgk2_ref[...].astype(BF16)) + bgk_ref[...]
    log_sig = jnp.minimum(z, 0.0) - jnp.log1p(jnp.exp(-jnp.abs(z)))
    lg = log_sig * (LOG2E / GATE_NORMALIZER)
    lg_hi = lg.astype(BF16)
    rem = lg - lg_hi.astype(F32)
    lg_mid = rem.astype(BF16)
    lg_lo = (rem - lg_mid.astype(F32)).astype(BF16)
    lgs3[:, 0:CHUNK, :] = lg_hi.reshape(nc, CHUNK, KEY_W)
    lgs3[:, CHUNK:2 * CHUNK, :] = lg_mid.reshape(nc, CHUNK, KEY_W)
    lgs3[:, 2 * CHUNK:3 * CHUNK, :] = lg_lo.reshape(nc, CHUNK, KEY_W)
    for c in range(nc):
        b_s[c, PAD:PAD + CHUNK, :] = _dot(lower4, lgs3[c])

    cb = proj(0, CONV_WIDTH)
    u = proj(CONV_WIDTH, CONV_WIDTH) * proj(2 * CONV_WIDTH, CONV_WIDTH)
    u_ref[8:8 + tt, :] = u
    cw = convw_ref[...]
    conv = (u_ref[6:6 + tt, :] * cw[0:1, :] + u_ref[7:7 + tt, :] * cw[1:2, :]
            + u * cw[2:3, :])
    y_conv = cb * conv
    u_ref[0:8, :] = u_ref[tt:tt + 8, :]

    q = q_s[...]
    k = k_s[:, PAD:PAD + CHUNK, :].reshape(tt, KEY_W)
    b = b_s[:, PAD:PAD + CHUNK, :].reshape(tt, KEY_W)
    b3 = b.reshape(nc, CHUNK, KEY_W)
    q3 = q.reshape(nc, CHUNK, KEY_W)
    b_last = jnp.broadcast_to(b3[:, CHUNK - 1:CHUNK, :], (nc, CHUNK, KEY_W)).reshape(tt, KEY_W)
    b8 = b.reshape(tt // SUB, SUB, KEY_W)
    b_end = jnp.broadcast_to(b8[:, SUB - 1:SUB, :], (tt // SUB, SUB, KEY_W)).reshape(tt, KEY_W)
    qg = (q * jnp.exp2(b)).astype(BF16)
    kd = k * jnp.exp2(b_last - b)
    kk = k * jnp.exp2(b_end - b)

    s = None
    for d in range(SUB):
        if d == 0:
            e = q * k
        else:
            k_d = k_s[:, PAD - d:PAD - d + CHUNK, :].reshape(tt, KEY_W)
            b_d = b_s[:, PAD - d:PAD - d + CHUNK, :].reshape(tt, KEY_W)
            e = q * k_d * jnp.exp2(b - b_d)
        term = _dot(e.astype(BF16), ones_bd).reshape(nc, CHUNK, KEY_W) * band_ref[d]
        s = term if s is None else s + term
    sc3[...] = s

    g_out = proj(base + 2 * KEY_W + VAL_W, VAL_W)

    qj_all = []
    for j in range(NSUB - 1):
        r0 = SUB * (j + 1)
        qj_all.append((q3[:, r0:, :] * jnp.exp2(b3[:, r0:, :] - b3[:, r0 - 1:r0, :])).astype(BF16))
    for c in range(nc):
        kc = kk[c * CHUNK:(c + 1) * CHUNK, :]
        kbd = jnp.where(head_bd, jnp.concatenate([kc] * GLA_HEADS, axis=0), 0.0).astype(BF16)
        lhs = jnp.concatenate([qj[c] for qj in qj_all], axis=0)
        out = lax.dot_general(lhs, kbd, (((1,), (1,)), ((), ())), preferred_element_type=F32)
        off = 0
        for j in range(NSUB - 1):
            r0 = SUB * (j + 1)
            n = CHUNK - r0
            sc3[c, r0:, :] = jnp.where(col_blk == j, out[off:off + n, :], sc3[c, r0:, :])
            off += n

    states = [s_ref[hd] for hd in range(GLA_HEADS)]
    for c in range(nc):
        rows = slice(c * CHUNK, (c + 1) * CHUNK)
        bt = jnp.transpose(b[(c + 1) * CHUNK - 8:(c + 1) * CHUNK, :])
        decay_col = jnp.exp2(bt[:, 7:8])
        for hd in range(GLA_HEADS):
            sb_s[c, hd] = states[hd].astype(BF16)
        for p in range(2):
            lanes = slice(p * PAIR_W, (p + 1) * PAIR_W)
            kdt = jnp.transpose(kd[rows, lanes]).astype(BF16)
            ktv = _dot(kdt, v_s[rows, p * PAIR_V:(p + 1) * PAIR_V])
            dec = decay_col[lanes, :]
            h0, h1 = 2 * p, 2 * p + 1
            states[h0] = states[h0] * dec[0:GLA_DK, :] + ktv[0:GLA_DK, 0:GLA_DV]
            states[h1] = states[h1] * dec[GLA_DK:PAIR_W, :] + ktv[GLA_DK:PAIR_W, GLA_DV:PAIR_V]
    for hd in range(GLA_HEADS):
        s_ref[hd] = states[hd]

    for c in range(nc):
        rows = slice(c * CHUNK, (c + 1) * CHUNK)
        for p in range(2):
            lanes = slice(p * PAIR_W, (p + 1) * PAIR_W)
            h0, h1 = 2 * p, 2 * p + 1
            v0 = v_s[rows, h0 * GLA_DV:(h0 + 1) * GLA_DV]
            v1 = v_s[rows, h1 * GLA_DV:(h1 + 1) * GLA_DV]
            lhs = jnp.concatenate([sc3[c, :, lanes].astype(BF16), qg[rows, lanes]], axis=1)
            w = jnp.concatenate([
                jnp.concatenate([v0, zero_v], axis=1),
                jnp.concatenate([zero_v, v1], axis=1),
                jnp.concatenate([sb_s[c, h0], zero_v], axis=1),
                jnp.concatenate([zero_v, sb_s[c, h1]], axis=1)], axis=0)
            o_s[rows, p * PAIR_V:(p + 1) * PAIR_V] = _dot(lhs, w)

    o = o_s[...]
    gn = gn_ref[...]
    parts = []
    for hd in range(GLA_HEADS):
        oh = o[:, hd * GLA_DV:(hd + 1) * GLA_DV]
        ms = jnp.mean(oh * oh, axis=-1, keepdims=True)
        parts.append(oh * lax.rsqrt(ms + EPS) * gn)
    y_gla = jnp.concatenate(parts, axis=1) * (g_out * jax.nn.sigmoid(g_out))
    y = jnp.concatenate([y_conv, y_gla], axis=1).astype(BF16)
    o_ref[0] = x + gate * _dot(y, wout_ref[...].astype(BF16))


def _mix_call(x, ada3, norm_w, w_in_t, conv_w, w_gk2, b_gk, gla_norm, w_out, *, tt):
    bsz, t, d = x.shape
    nc = tt // CHUNK
    band = _band_masks()
    kern = functools.partial(_mix_kernel, tt=tt)
    const = lambda b, i: (0, 0)
    single = pl.Buffered(1)
    return pl.pallas_call(
        kern,
        grid=(bsz, t // tt),
        in_specs=[
            pl.BlockSpec((1, tt, d), lambda b, i: (b, i, 0)),
            pl.BlockSpec((1, N_ADA, d), lambda b, i: (b, 0, 0)),
            pl.BlockSpec((1, d), const),
            pl.BlockSpec(w_in_t.shape, const, pipeline_mode=single),
            pl.BlockSpec(conv_w.shape, const),
            pl.BlockSpec(w_gk2.shape, const),
            pl.BlockSpec(b_gk.shape, const),
            pl.BlockSpec(gla_norm.shape, const),
            pl.BlockSpec(w_out.shape, const, pipeline_mode=single),
            pl.BlockSpec(band.shape, lambda b, i: (0, 0, 0), pipeline_mode=single),
        ],
        out_specs=pl.BlockSpec((1, tt, d), lambda b, i: (b, i, 0)),
        out_shape=jax.ShapeDtypeStruct(x.shape, F32),
        scratch_shapes=[
            pltpu.VMEM((GLA_HEADS, GLA_DK, GLA_DV), F32),
            pltpu.VMEM((tt + 8, CONV_WIDTH), F32),
            pltpu.VMEM((tt, KEY_W), F32),
            pltpu.VMEM((nc, PAD + CHUNK, KEY_W), F32),
            pltpu.VMEM((nc, 4 * CHUNK, KEY_W), BF16),
            pltpu.VMEM((nc, PAD + CHUNK, KEY_W), F32),
            pltpu.VMEM((tt, VAL_W), BF16),
            pltpu.VMEM((tt, VAL_W), F32),
            pltpu.VMEM((nc, CHUNK, KEY_W), F32),
            pltpu.VMEM((nc, GLA_HEADS, GLA_DK, GLA_DV), BF16),
        ],
        compiler_params=pltpu.CompilerParams(
            dimension_semantics=("arbitrary", "arbitrary"), vmem_limit_bytes=VMEM_LIMIT),
        name="token_mixer",
    )(x, ada3, norm_w, w_in_t, conv_w, w_gk2, b_gk, gla_norm, w_out, band)


def kernel(x, c, w_ada, b_ada, norm_ffn1, w_ffn1_in, w_ffn1_out, norm_mix, w_mix_in, conv_w,
           w_gk2, b_gk, gla_norm, w_mix_out, norm_ffn2, w_ffn2_in, w_ffn2_out, norm_final):
    bsz, t, d = x.shape
    depth = w_ada.shape[0]
    c_pad = jnp.zeros((8, d), F32).at[:bsz].set(c)
    norm_f = norm_final.reshape(1, d)
    for l in range(depth):
        last = l == depth - 1
        ada = _ada_call(c_pad, w_ada[l], b_ada[l].reshape(1, -1))
        ada3 = ada[:bsz].reshape(bsz, N_ADA, d)
        x = _ffn_call(x, ada3, norm_ffn1[l].reshape(1, d), w_ffn1_in[l], w_ffn1_out[l],
                      norm_f, ada_base=0, final_norm=False, tm=1024)
        x = _mix_call(x, ada3, norm_mix[l].reshape(1, d), jnp.transpose(w_mix_in[l]),
                      conv_w[l], w_gk2[l], b_gk[l].reshape(1, KEY_W),
                      gla_norm[l].reshape(1, GLA_DV), w_mix_out[l], tt=512)
        x = _ffn_call(x, ada3, norm_ffn2[l].reshape(1, d), w_ffn2_in[l], w_ffn2_out[l],
                      norm_f, ada_base=6, final_norm=last, tm=1024)
    return x
```

```python
import functools
import math

import jax
import jax.numpy as jnp
from jax import lax
from jax.experimental import pallas as pl
from jax.experimental.pallas import tpu as pltpu

F32 = jnp.float32
BF16 = jnp.bfloat16

EPS = 1e-6
CONV_WIDTH = 512
GLA_HEADS = 4
GLA_DK = 64
GLA_DV = 128
KEY_W = GLA_HEADS * GLA_DK
VAL_W = GLA_HEADS * GLA_DV
GATE_RANK = 16
GATE_NORMALIZER = 16.0
CHUNK = 64
SUB = 8
NSUB = CHUNK // SUB
PAD = 8
N_ADA = 9
PAIR_W = 2 * GLA_DK
PAIR_V = 2 * GLA_DV
MIX_MAIN = 3 * CONV_WIDTH + 2 * KEY_W + 2 * VAL_W
VMEM_LIMIT = 56 * 1024 * 1024
LOG2E = math.log2(math.e)
FFN_CHUNK = 256
TOKEN_TILE = 512


def _dot(a, b):
    return jnp.dot(a, b, preferred_element_type=F32)


def _norm_mod(x, gain, shift, scale):
    ms = jnp.mean(x * x, axis=-1, keepdims=True)
    y = x * lax.rsqrt(ms + EPS) * gain
    return y * (1.0 + scale) + shift


def _ada_kernel(c_ref, w_ref, b_ref, o_ref):
    c = c_ref[...]
    ca = (c * jax.nn.sigmoid(c)).astype(BF16)
    o_ref[...] = _dot(ca, w_ref[...].astype(BF16)) + b_ref[...]


def _ada_call(c_pad, w_ada, b_ada):
    rows, d = c_pad.shape
    n = w_ada.shape[1]
    tn = 3 * d
    return pl.pallas_call(
        _ada_kernel,
        grid=(n // tn,),
        in_specs=[
            pl.BlockSpec((rows, d), lambda j: (0, 0)),
            pl.BlockSpec((d, tn), lambda j: (0, j)),
            pl.BlockSpec((1, tn), lambda j: (0, j)),
        ],
        out_specs=pl.BlockSpec((rows, tn), lambda j: (0, j)),
        out_shape=jax.ShapeDtypeStruct((rows, n), F32),
        compiler_params=pltpu.CompilerParams(
            dimension_semantics=("arbitrary",), vmem_limit_bytes=VMEM_LIMIT),
        name="ada_proj",
    )(c_pad, w_ada, b_ada)


def _ffn_kernel(x_ref, ada_ref, nw_ref, win_ref, wout_ref, nf_ref, o_ref, a_ref,
                *, ada_base, chunks, d_ff, final_norm):
    x = x_ref[0]
    shift = ada_ref[0, ada_base:ada_base + 1, :]
    scale = ada_ref[0, ada_base + 1:ada_base + 2, :]
    gate = ada_ref[0, ada_base + 2:ada_base + 3, :]
    h = _norm_mod(x, nw_ref[...], shift, scale).astype(BF16)
    off = 0
    for w in chunks:
        g = _dot(h, win_ref[:, off:off + w].astype(BF16))
        u = _dot(h, win_ref[:, d_ff + off:d_ff + off + w].astype(BF16))
        a_ref[:, off:off + w] = (g * jax.nn.sigmoid(g) * u).astype(BF16)
        off += w
    y = _dot(a_ref[...], wout_ref[...].astype(BF16))
    out = x + (0.5 * gate) * y
    if final_norm:
        ms = jnp.mean(out * out, axis=-1, keepdims=True)
        out = out * lax.rsqrt(ms + EPS) * nf_ref[...]
    o_ref[0] = out


def _ffn_call(x, ada3, norm_w, w_in, w_out, norm_f, *, ada_base, final_norm, tm):
    bsz, t, d = x.shape
    d_ff = w_out.shape[0]
    chunks = [FFN_CHUNK] * (d_ff // FFN_CHUNK)
    if d_ff % FFN_CHUNK:
        chunks.append(d_ff % FFN_CHUNK)
    kern = functools.partial(_ffn_kernel, ada_base=ada_base, chunks=tuple(chunks),
                             d_ff=d_ff, final_norm=final_norm)
    const = lambda b, i: (0, 0)
    return pl.pallas_call(
        kern,
        grid=(bsz, t // tm),
        in_specs=[
            pl.BlockSpec((1, tm, d), lambda b, i: (b, i, 0)),
            pl.BlockSpec((1, N_ADA, d), lambda b, i: (b, 0, 0)),
            pl.BlockSpec((1, d), const),
            pl.BlockSpec((d, 2 * d_ff), const, pipeline_mode=pl.Buffered(1)),
            pl.BlockSpec((d_ff, d), const, pipeline_mode=pl.Buffered(1)),
            pl.BlockSpec((1, d), const),
        ],
        out_specs=pl.BlockSpec((1, tm, d), lambda b, i: (b, i, 0)),
        out_shape=jax.ShapeDtypeStruct(x.shape, F32),
        scratch_shapes=[pltpu.VMEM((tm, d_ff), BF16)],
        compiler_params=pltpu.CompilerParams(
            dimension_semantics=("arbitrary", "arbitrary"), vmem_limit_bytes=VMEM_LIMIT),
        name="ffn_final" if final_norm else "ffn",
    )(x, ada3, norm_w, w_in, w_out, norm_f)


def _band_masks():
    i = jnp.arange(CHUNK)[None, :, None]
    j = (jnp.arange(KEY_W) % GLA_DK)[None, None, :]
    d = jnp.arange(SUB)[:, None, None]
    return ((j == i - d) & (i % SUB >= d)).astype(F32)


def _mix_kernel(x_ref, ada_ref, nw_ref, win_ref, convw_ref, wgk2_ref, bgk_ref,
                gn_ref, wout_ref, band_ref, o_ref,
                s_ref, u_ref, q_s, k_s, lgs3, b_s, v_s, o_s, sc3, sb_s, *, tt):
    nc = tt // CHUNK
    t_idx = pl.program_id(1)

    @pl.when(t_idx == 0)
    def _():
        s_ref[...] = jnp.zeros(s_ref.shape, F32)
        u_ref[0:8, :] = jnp.zeros((8, CONV_WIDTH), F32)
        k_s[:, 0:PAD, :] = jnp.zeros((nc, PAD, KEY_W), F32)
        b_s[:, 0:PAD, :] = jnp.zeros((nc, PAD, KEY_W), F32)
        lgs3[:, 3 * CHUNK:4 * CHUNK, :] = jnp.zeros((nc, CHUNK, KEY_W), BF16)

    x = x_ref[0]
    shift = ada_ref[0, 3:4, :]
    scale = ada_ref[0, 4:5, :]
    gate = ada_ref[0, 5:6, :]
    h = _norm_mod(x, nw_ref[...], shift, scale).astype(BF16)

    def proj(lo, width):
        return lax.dot_general(h, win_ref[lo:lo + width, :].astype(BF16),
                               (((1,), (1,)), ((), ())), preferred_element_type=F32)

    tri_r = lax.broadcasted_iota(jnp.int32, (CHUNK, 4 * CHUNK), 0)
    tri_c = lax.broadcasted_iota(jnp.int32, (CHUNK, 4 * CHUNK), 1)
    lower4 = ((tri_c % CHUNK) <= tri_r).astype(BF16)
    hr = lax.broadcasted_iota(jnp.int32, (KEY_W, KEY_W), 0) // GLA_DK
    hc = lax.broadcasted_iota(jnp.int32, (KEY_W, KEY_W), 1) // GLA_DK
    head_bd = hr == hc
    ones_bd = head_bd.astype(BF16)
    col_blk = (lax.broadcasted_iota(jnp.int32, (1, KEY_W), 1) % GLA_DK) // SUB
    zero_v = jnp.zeros((CHUNK, GLA_DV), BF16)

    base = 3 * CONV_WIDTH
    q_s[...] = proj(base, KEY_W) * (GLA_DK ** -0.5)
    k_s[:, PAD:PAD + CHUNK, :] = proj(base + KEY_W, KEY_W).reshape(nc, CHUNK, KEY_W)
    v_s[...] = proj(base + 2 * KEY_W, VAL_W).astype(BF16)
    gk_low = proj(MIX_MAIN, GATE_RANK)
    z = _dot(gk_low.astype(BF16), wgk2_ref[...].astype(BF16)) + bgk_ref[...]
    log_sig = jnp.minimum(z, 0.0) - jnp.log1p(jnp.exp(-jnp.abs(z)))
    lg = log_sig * (LOG2E / GATE_NORMALIZER)
    lg_hi = lg.astype(BF16)
    rem = lg - lg_hi.astype(F32)
    lg_mid = rem.astype(BF16)
    lg_lo = (rem - lg_mid.astype(F32)).astype(BF16)
    lgs3[:, 0:CHUNK, :] = lg_hi.reshape(nc, CHUNK, KEY_W)
    lgs3[:, CHUNK:2 * CHUNK, :] = lg_mid.reshape(nc, CHUNK, KEY_W)
    lgs3[:, 2 * CHUNK:3 * CHUNK, :] = lg_lo.reshape(nc, CHUNK, KEY_W)
    for c in range(nc):
        b_s[c, PAD:PAD + CHUNK, :] = _dot(lower4, lgs3[c])

    cb = proj(0, CONV_WIDTH)
    u = proj(CONV_WIDTH, CONV_WIDTH) * proj(2 * CONV_WIDTH, CONV_WIDTH)
    u_ref[8:8 + tt, :] = u
    cw = convw_ref[...]
    conv = (u_ref[6:6 + tt, :] * cw[0:1, :] + u_ref[7:7 + tt, :] * cw[1:2, :]
            + u * cw[2:3, :])
    y_conv = cb * conv
    u_ref[0:8, :] = u_ref[tt:tt + 8, :]

    q = q_s[...]
    k = k_s[:, PAD:PAD + CHUNK, :].reshape(tt, KEY_W)
    b = b_s[:, PAD:PAD + CHUNK, :].reshape(tt, KEY_W)
    b3 = b.reshape(nc, CHUNK, KEY_W)
    q3 = q.reshape(nc, CHUNK, KEY_W)
    b_last = jnp.broadcast_to(b3[:, CHUNK - 1:CHUNK, :], (nc, CHUNK, KEY_W)).reshape(tt, KEY_W)
    b8 = b.reshape(tt // SUB, SUB, KEY_W)
    b_end = jnp.broadcast_to(b8[:, SUB - 1:SUB, :], (tt // SUB, SUB, KEY_W)).reshape(tt, KEY_W)
    qg = (q * jnp.exp2(b)).astype(BF16)
    kd = k * jnp.exp2(b_last - b)
    kk = k * jnp.exp2(b_end - b)

    s = None
    for d in range(SUB):
        if d == 0:
            e = q * k
        else:
            k_d = k_s[:, PAD - d:PAD - d + CHUNK, :].reshape(tt, KEY_W)
            b_d = b_s[:, PAD - d:PAD - d + CHUNK, :].reshape(tt, KEY_W)
            e = q * k_d * jnp.exp2(b - b_d)
        term = _dot(e.astype(BF16), ones_bd).reshape(nc, CHUNK, KEY_W) * band_ref[d]
        s = term if s is None else s + term
    sc3[...] = s

    g_out = proj(base + 2 * KEY_W + VAL_W, VAL_W)

    qj_all = []
    for j in range(NSUB - 1):
        r0 = SUB * (j + 1)
        qj_all.append((q3[:, r0:, :] * jnp.exp2(b3[:, r0:, :] - b3[:, r0 - 1:r0, :])).astype(BF16))
    for c in range(nc):
        kc = kk[c * CHUNK:(c + 1) * CHUNK, :]
        kbd = jnp.where(head_bd, jnp.concatenate([kc] * GLA_HEADS, axis=0), 0.0).astype(BF16)
        lhs = jnp.concatenate([qj[c] for qj in qj_all], axis=0)
        out = lax.dot_general(lhs, kbd, (((1,), (1,)), ((), ())), preferred_element_type=F32)
        off = 0
        for j in range(NSUB - 1):
            r0 = SUB * (j + 1)
            n = CHUNK - r0
            sc3[c, r0:, :] = jnp.where(col_blk == j, out[off:off + n, :], sc3[c, r0:, :])
            off += n

    states = [s_ref[hd] for hd in range(GLA_HEADS)]
    for c in range(nc):
        rows = slice(c * CHUNK, (c + 1) * CHUNK)
        bt = jnp.transpose(b[(c + 1) * CHUNK - 8:(c + 1) * CHUNK, :])
        decay_col = jnp.exp2(bt[:, 7:8])
        for hd in range(GLA_HEADS):
            sb_s[c, hd] = states[hd].astype(BF16)
        for p in range(2):
            lanes = slice(p * PAIR_W, (p + 1) * PAIR_W)
            kdt = jnp.transpose(kd[rows, lanes]).astype(BF16)
            ktv = _dot(kdt, v_s[rows, p * PAIR_V:(p + 1) * PAIR_V])
            dec = decay_col[lanes, :]
            h0, h1 = 2 * p, 2 * p + 1
            states[h0] = states[h0] * dec[0:GLA_DK, :] + ktv[0:GLA_DK, 0:GLA_DV]
            states[h1] = states[h1] * dec[GLA_DK:PAIR_W, :] + ktv[GLA_DK:PAIR_W, GLA_DV:PAIR_V]
    for hd in range(GLA_HEADS):
        s_ref[hd] = states[hd]

    for c in range(nc):
        rows = slice(c * CHUNK, (c + 1) * CHUNK)
        for p in range(2):
            lanes = slice(p * PAIR_W, (p + 1) * PAIR_W)
            h0, h1 = 2 * p, 2 * p + 1
            v0 = v_s[rows, h0 * GLA_DV:(h0 + 1) * GLA_DV]
            v1 = v_s[rows, h1 * GLA_DV:(h1 + 1) * GLA_DV]
            lhs = jnp.concatenate([sc3[c, :, lanes].astype(BF16), qg[rows, lanes]], axis=1)
            w = jnp.concatenate([
                jnp.concatenate([v0, zero_v], axis=1),
                jnp.concatenate([zero_v, v1], axis=1),
                jnp.concatenate([sb_s[c, h0], zero_v], axis=1),
                jnp.concatenate([zero_v, sb_s[c, h1]], axis=1)], axis=0)
            o_s[rows, p * PAIR_V:(p + 1) * PAIR_V] = _dot(lhs, w)

    o = o_s[...]
    gn = gn_ref[...]
    parts = []
    for hd in range(GLA_HEADS):
        oh = o[:, hd * GLA_DV:(hd + 1) * GLA_DV]
        ms = jnp.mean(oh * oh, axis=-1, keepdims=True)
        parts.append(oh * lax.rsqrt(ms + EPS) * gn)
    y_gla = jnp.concatenate(parts, axis=1) * (g_out * jax.nn.sigmoid(g_out))
    y = jnp.concatenate([y_conv, y_gla], axis=1).astype(BF16)
    o_ref[0] = x + gate * _dot(y, wout_ref[...].astype(BF16))


def _mix_call(x, ada3, norm_w, w_in_t, conv_w, w_gk2, b_gk, gla_norm, w_out, *, tt):
    bsz, t, d = x.shape
    nc = tt // CHUNK
    band = _band_masks()
    kern = functools.partial(_mix_kernel, tt=tt)
    const = lambda b, i: (0, 0)
    single = pl.Buffered(1)
    return pl.pallas_call(
        kern,
        grid=(bsz, t // tt),
        in_specs=[
            pl.BlockSpec((1, tt, d), lambda b, i: (b, i, 0)),
            pl.BlockSpec((1, N_ADA, d), lambda b, i: (b, 0, 0)),
            pl.BlockSpec((1, d), const),
            pl.BlockSpec(w_in_t.shape, const, pipeline_mode=single),
            pl.BlockSpec(conv_w.shape, const),
            pl.BlockSpec(w_gk2.shape, const),
            pl.BlockSpec(b_gk.shape, const),
            pl.BlockSpec(gla_norm.shape, const),
            pl.BlockSpec(w_out.shape, const, pipeline_mode=single),
            pl.BlockSpec(band.shape, lambda b, i: (0, 0, 0), pipeline_mode=single),
        ],
        out_specs=pl.BlockSpec((1, tt, d), lambda b, i: (b, i, 0)),
        out_shape=jax.ShapeDtypeStruct(x.shape, F32),
        scratch_shapes=[
            pltpu.VMEM((GLA_HEADS, GLA_DK, GLA_DV), F32),
            pltpu.VMEM((tt + 8, CONV_WIDTH), F32),
            pltpu.VMEM((tt, KEY_W), F32),
            pltpu.VMEM((nc, PAD + CHUNK, KEY_W), F32),
            pltpu.VMEM((nc, 4 * CHUNK, KEY_W), BF16),
            pltpu.VMEM((nc, PAD + CHUNK, KEY_W), F32),
            pltpu.VMEM((tt, VAL_W), BF16),
            pltpu.VMEM((tt, VAL_W), F32),
            pltpu.VMEM((nc, CHUNK, KEY_W), F32),
            pltpu.VMEM((nc, GLA_HEADS, GLA_DK, GLA_DV), BF16),
        ],
        compiler_params=pltpu.CompilerParams(
            dimension_semantics=("arbitrary", "arbitrary"), vmem_limit_bytes=VMEM_LIMIT),
        name="token_mixer",
    )(x, ada3, norm_w, w_in_t, conv_w, w_gk2, b_gk, gla_norm, w_out, band)


def kernel(x, c, w_ada, b_ada, norm_ffn1, w_ffn1_in, w_ffn1_out, norm_mix, w_mix_in, conv_w,
           w_gk2, b_gk, gla_norm, w_mix_out, norm_ffn2, w_ffn2_in, w_ffn2_out, norm_final):
    bsz, t, d = x.shape
    depth = w_ada.shape[0]
    c_pad = jnp.zeros((8, d), F32).at[:bsz].set(c)
    norm_f = norm_final.reshape(1, d)
    for l in range(depth):
        last = l == depth - 1
        ada = _ada_call(c_pad, w_ada[l], b_ada[l].reshape(1, -1))
        ada3 = ada[:bsz].reshape(bsz, N_ADA, d)
        x = _ffn_call(x, ada3, norm_ffn1[l].reshape(1, d), w_ffn1_in[l], w_ffn1_out[l],
                      norm_f, ada_base=0, final_norm=False, tm=TOKEN_TILE)
        x = _mix_call(x, ada3, norm_mix[l].reshape(1, d), jnp.transpose(w_mix_in[l]),
                      conv_w[l], w_gk2[l], b_gk[l].reshape(1, KEY_W),
                      gla_norm[l].reshape(1, GLA_DV), w_mix_out[l], tt=TOKEN_TILE)
        x = _ffn_call(x, ada3, norm_ffn2[l].reshape(1, d), w_ffn2_in[l], w_ffn2_out[l],
                      norm_f, ada_base=6, final_norm=last, tm=TOKEN_TILE)
    return x
```

```python
import functools
import math

import jax
import jax.numpy as jnp
from jax import lax
from jax.experimental import pallas as pl
from jax.experimental.pallas import tpu as pltpu

F32 = jnp.float32
BF16 = jnp.bfloat16

EPS = 1e-6
CONV_WIDTH = 512
GLA_HEADS = 4
GLA_DK = 64
GLA_DV = 128
KEY_W = GLA_HEADS * GLA_DK
VAL_W = GLA_HEADS * GLA_DV
GATE_RANK = 16
GATE_NORMALIZER = 16.0
CHUNK = 64
SUB = 8
NSUB = CHUNK // SUB
PAD = 8
N_ADA = 9
PAIR_W = 2 * GLA_DK
PAIR_V = 2 * GLA_DV
MIX_MAIN = 3 * CONV_WIDTH + 2 * KEY_W + 2 * VAL_W
VMEM_LIMIT = 56 * 1024 * 1024
LOG2E = math.log2(math.e)
FFN_CHUNK = 256
TOKEN_TILE = 512


def _dot(a, b):
    return jnp.dot(a, b, preferred_element_type=F32)


def _norm_mod(x, gain, shift, scale):
    ms = jnp.mean(x * x, axis=-1, keepdims=True)
    y = x * lax.rsqrt(ms + EPS) * gain
    return y * (1.0 + scale) + shift


def _ada_kernel(c_ref, w_ref, b_ref, o_ref):
    c = c_ref[...]
    ca = (c * jax.nn.sigmoid(c)).astype(BF16)
    o_ref[...] = _dot(ca, w_ref[...].astype(BF16)) + b_ref[...]


def _ada_call(c_pad, w_ada, b_ada):
    rows, d = c_pad.shape
    n = w_ada.shape[1]
    tn = 3 * d
    return pl.pallas_call(
        _ada_kernel,
        grid=(n // tn,),
        in_specs=[
            pl.BlockSpec((rows, d), lambda j: (0, 0)),
            pl.BlockSpec((d, tn), lambda j: (0, j)),
            pl.BlockSpec((1, tn), lambda j: (0, j)),
        ],
        out_specs=pl.BlockSpec((rows, tn), lambda j: (0, j)),
        out_shape=jax.ShapeDtypeStruct((rows, n), F32),
        compiler_params=pltpu.CompilerParams(
            dimension_semantics=("arbitrary",), vmem_limit_bytes=VMEM_LIMIT),
        name="ada_proj",
    )(c_pad, w_ada, b_ada)


def _ffn_kernel(x_ref, ada_ref, nw_ref, win_ref, wout_ref, nf_ref, o_ref, a_ref,
                *, ada_base, chunks, d_ff, final_norm):
    x = x_ref[0]
    shift = ada_ref[0, ada_base:ada_base + 1, :]
    scale = ada_ref[0, ada_base + 1:ada_base + 2, :]
    gate = ada_ref[0, ada_base + 2:ada_base + 3, :]
    h = _norm_mod(x, nw_ref[...], shift, scale).astype(BF16)
    off = 0
    for w in chunks:
        g = _dot(h, win_ref[:, off:off + w].astype(BF16))
        u = _dot(h, win_ref[:, d_ff + off:d_ff + off + w].astype(BF16))
        a_ref[:, off:off + w] = (g * jax.nn.sigmoid(g) * u).astype(BF16)
        off += w
    y = _dot(a_ref[...], wout_ref[...].astype(BF16))
    out = x + (0.5 * gate) * y
    if final_norm:
        ms = jnp.mean(out * out, axis=-1, keepdims=True)
        out = out * lax.rsqrt(ms + EPS) * nf_ref[...]
    o_ref[0] = out


def _ffn_call(x, ada3, norm_w, w_in, w_out, norm_f, *, ada_base, final_norm, tm):
    bsz, t, d = x.shape
    d_ff = w_out.shape[0]
    chunks = [FFN_CHUNK] * (d_ff // FFN_CHUNK)
    if d_ff % FFN_CHUNK:
        chunks.append(d_ff % FFN_CHUNK)
    kern = functools.partial(_ffn_kernel, ada_base=ada_base, chunks=tuple(chunks),
                             d_ff=d_ff, final_norm=final_norm)
    const = lambda b, i: (0, 0)
    return pl.pallas_call(
        kern,
        grid=(bsz, t // tm),
        in_specs=[
            pl.BlockSpec((1, tm, d), lambda b, i: (b, i, 0)),
            pl.BlockSpec((1, N_ADA, d), lambda b, i: (b, 0, 0)),
            pl.BlockSpec((1, d), const),
            pl.BlockSpec((d, 2 * d_ff), const, pipeline_mode=pl.Buffered(1)),
            pl.BlockSpec((d_ff, d), const, pipeline_mode=pl.Buffered(1)),
            pl.BlockSpec((1, d), const),
        ],
        out_specs=pl.BlockSpec((1, tm, d), lambda b, i: (b, i, 0)),
        out_shape=jax.ShapeDtypeStruct(x.shape, F32),
        scratch_shapes=[pltpu.VMEM((tm, d_ff), BF16)],
        compiler_params=pltpu.CompilerParams(
            dimension_semantics=("arbitrary", "arbitrary"), vmem_limit_bytes=VMEM_LIMIT),
        name="ffn_final" if final_norm else "ffn",
    )(x, ada3, norm_w, w_in, w_out, norm_f)


def _band_masks():
    i = jnp.arange(CHUNK)[None, :, None]
    j = (jnp.arange(KEY_W) % GLA_DK)[None, None, :]
    d = jnp.arange(SUB)[:, None, None]
    return ((j == i - d) & (i % SUB >= d)).astype(F32)


def _mix_kernel(x_ref, ada_ref, nw_ref, win_ref, convw_ref, wgk2_ref, bgk_ref,
                gn_ref, wout_ref, band_ref, nxt_in_ref, nxt_out_ref,
                o_ref, nxt_in_bf_ref, nxt_out_bf_ref,
                s_ref, u_ref, q_s, k_s, lgs3, b_s, v_s, o_s, sc3, sb_s, *, tt):
    nc = tt // CHUNK
    t_idx = pl.program_id(1)

    @pl.when(t_idx == 0)
    def _():
        s_ref[...] = jnp.zeros(s_ref.shape, F32)
        u_ref[0:8, :] = jnp.zeros((8, CONV_WIDTH), F32)
        k_s[:, 0:PAD, :] = jnp.zeros((nc, PAD, KEY_W), F32)
        b_s[:, 0:PAD, :] = jnp.zeros((nc, PAD, KEY_W), F32)
        lgs3[:, 3 * CHUNK:4 * CHUNK, :] = jnp.zeros((nc, CHUNK, KEY_W), BF16)

    x = x_ref[0]
    shift = ada_ref[0, 3:4, :]
    scale = ada_ref[0, 4:5, :]
    gate = ada_ref[0, 5:6, :]
    h = _norm_mod(x, nw_ref[...], shift, scale).astype(BF16)

    def proj(lo, width):
        return lax.dot_general(h, win_ref[lo:lo + width, :].astype(BF16),
                               (((1,), (1,)), ((), ())), preferred_element_type=F32)

    tri_r = lax.broadcasted_iota(jnp.int32, (CHUNK, 4 * CHUNK), 0)
    tri_c = lax.broadcasted_iota(jnp.int32, (CHUNK, 4 * CHUNK), 1)
    lower4 = ((tri_c % CHUNK) <= tri_r).astype(BF16)
    hr = lax.broadcasted_iota(jnp.int32, (KEY_W, KEY_W), 0) // GLA_DK
    hc = lax.broadcasted_iota(jnp.int32, (KEY_W, KEY_W), 1) // GLA_DK
    head_bd = hr == hc
    ones_bd = head_bd.astype(BF16)
    col_blk = (lax.broadcasted_iota(jnp.int32, (1, KEY_W), 1) % GLA_DK) // SUB
    zero_v = jnp.zeros((CHUNK, GLA_DV), BF16)

    base = 3 * CONV_WIDTH
    q_s[...] = proj(base, KEY_W) * (GLA_DK ** -0.5)
    k_s[:, PAD:PAD + CHUNK, :] = proj(base + KEY_W, KEY_W).reshape(nc, CHUNK, KEY_W)
    v_s[...] = proj(base + 2 * KEY_W, VAL_W).astype(BF16)
    gk_low = proj(MIX_MAIN, GATE_RANK)
    z = _dot(gk_low.astype(BF16), wgk2_ref[...].astype(BF16)) + bgk_ref[...]
    log_sig = jnp.minimum(z, 0.0) - jnp.log1p(jnp.exp(-jnp.abs(z)))
    lg = log_sig * (LOG2E / GATE_NORMALIZER)
    lg_hi = lg.astype(BF16)
    rem = lg - lg_hi.astype(F32)
    lg_mid = rem.astype(BF16)
    lg_lo = (rem - lg_mid.astype(F32)).astype(BF16)
    lgs3[:, 0:CHUNK, :] = lg_hi.reshape(nc, CHUNK, KEY_W)
    lgs3[:, CHUNK:2 * CHUNK, :] = lg_mid.reshape(nc, CHUNK, KEY_W)
    lgs3[:, 2 * CHUNK:3 * CHUNK, :] = lg_lo.reshape(nc, CHUNK, KEY_W)
    for c in range(nc):
        b_s[c, PAD:PAD + CHUNK, :] = _dot(lower4, lgs3[c])

    cb = proj(0, CONV_WIDTH)
    u = proj(CONV_WIDTH, CONV_WIDTH) * proj(2 * CONV_WIDTH, CONV_WIDTH)
    u_ref[8:8 + tt, :] = u
    cw = convw_ref[...]
    conv = (u_ref[6:6 + tt, :] * cw[0:1, :] + u_ref[7:7 + tt, :] * cw[1:2, :]
            + u * cw[2:3, :])
    y_conv = cb * conv
    u_ref[0:8, :] = u_ref[tt:tt + 8, :]

    q = q_s[...]
    k = k_s[:, PAD:PAD + CHUNK, :].reshape(tt, KEY_W)
    b = b_s[:, PAD:PAD + CHUNK, :].reshape(tt, KEY_W)
    b3 = b.reshape(nc, CHUNK, KEY_W)
    q3 = q.reshape(nc, CHUNK, KEY_W)
    b_last = jnp.broadcast_to(b3[:, CHUNK - 1:CHUNK, :], (nc, CHUNK, KEY_W)).reshape(tt, KEY_W)
    b8 = b.reshape(tt // SUB, SUB, KEY_W)
    b_end = jnp.broadcast_to(b8[:, SUB - 1:SUB, :], (tt // SUB, SUB, KEY_W)).reshape(tt, KEY_W)
    qg = (q * jnp.exp2(b)).astype(BF16)
    kd = k * jnp.exp2(b_last - b)
    kk = k * jnp.exp2(b_end - b)

    s = None
    for d in range(SUB):
        if d == 0:
            e = q * k
        else:
            k_d = k_s[:, PAD - d:PAD - d + CHUNK, :].reshape(tt, KEY_W)
            b_d = b_s[:, PAD - d:PAD - d + CHUNK, :].reshape(tt, KEY_W)
            e = q * k_d * jnp.exp2(b - b_d)
        term = _dot(e.astype(BF16), ones_bd).reshape(nc, CHUNK, KEY_W) * band_ref[d]
        s = term if s is None else s + term
    sc3[...] = s

    g_out = proj(base + 2 * KEY_W + VAL_W, VAL_W)

    qj_all = []
    for j in range(NSUB - 1):
        r0 = SUB * (j + 1)
        qj_all.append((q3[:, r0:, :] * jnp.exp2(b3[:, r0:, :] - b3[:, r0 - 1:r0, :])).astype(BF16))
    for c in range(nc):
        kc = kk[c * CHUNK:(c + 1) * CHUNK, :]
        kbd = jnp.where(head_bd, jnp.concatenate([kc] * GLA_HEADS, axis=0), 0.0).astype(BF16)
        lhs = jnp.concatenate([qj[c] for qj in qj_all], axis=0)
        out = lax.dot_general(lhs, kbd, (((1,), (1,)), ((), ())), preferred_element_type=F32)
        off = 0
        for j in range(NSUB - 1):
            r0 = SUB * (j + 1)
            n = CHUNK - r0
            sc3[c, r0:, :] = jnp.where(col_blk == j, out[off:off + n, :], sc3[c, r0:, :])
            off += n

    states = [s_ref[hd] for hd in range(GLA_HEADS)]
    for c in range(nc):
        rows = slice(c * CHUNK, (c + 1) * CHUNK)
        bt = jnp.transpose(b[(c + 1) * CHUNK - 8:(c + 1) * CHUNK, :])
        decay_col = jnp.exp2(bt[:, 7:8])
        for hd in range(GLA_HEADS):
            sb_s[c, hd] = states[hd].astype(BF16)
        for p in range(2):
            lanes = slice(p * PAIR_W, (p + 1) * PAIR_W)
            kdt = jnp.transpose(kd[rows, lanes]).astype(BF16)
            ktv = _dot(kdt, v_s[rows, p * PAIR_V:(p + 1) * PAIR_V])
            dec = decay_col[lanes, :]
            h0, h1 = 2 * p, 2 * p + 1
            states[h0] = states[h0] * dec[0:GLA_DK, :] + ktv[0:GLA_DK, 0:GLA_DV]
            states[h1] = states[h1] * dec[GLA_DK:PAIR_W, :] + ktv[GLA_DK:PAIR_W, GLA_DV:PAIR_V]
    for hd in range(GLA_HEADS):
        s_ref[hd] = states[hd]

    for c in range(nc):
        rows = slice(c * CHUNK, (c + 1) * CHUNK)
        for p in range(2):
            lanes = slice(p * PAIR_W, (p + 1) * PAIR_W)
            h0, h1 = 2 * p, 2 * p + 1
            v0 = v_s[rows, h0 * GLA_DV:(h0 + 1) * GLA_DV]
            v1 = v_s[rows, h1 * GLA_DV:(h1 + 1) * GLA_DV]
            lhs = jnp.concatenate([sc3[c, :, lanes].astype(BF16), qg[rows, lanes]], axis=1)
            w = jnp.concatenate([
                jnp.concatenate([v0, zero_v], axis=1),
                jnp.concatenate([zero_v, v1], axis=1),
                jnp.concatenate([sb_s[c, h0], zero_v], axis=1),
                jnp.concatenate([zero_v, sb_s[c, h1]], axis=1)], axis=0)
            o_s[rows, p * PAIR_V:(p + 1) * PAIR_V] = _dot(lhs, w)

    o = o_s[...]
    gn = gn_ref[...]
    parts = []
    for hd in range(GLA_HEADS):
        oh = o[:, hd * GLA_DV:(hd + 1) * GLA_DV]
        ms = jnp.mean(oh * oh, axis=-1, keepdims=True)
        parts.append(oh * lax.rsqrt(ms + EPS) * gn)
    y_gla = jnp.concatenate(parts, axis=1) * (g_out * jax.nn.sigmoid(g_out))
    y = jnp.concatenate([y_conv, y_gla], axis=1).astype(BF16)
    o_ref[0] = x + gate * _dot(y, wout_ref[...].astype(BF16))
    nxt_in_bf_ref[...] = nxt_in_ref[...].astype(BF16)
    nxt_out_bf_ref[...] = nxt_out_ref[...].astype(BF16)


def _side_block(rows, steps):
    k = 1
    while k <= steps:
        n_blocks = steps // k
        if steps % k == 0 and rows % n_blocks == 0 and (rows // n_blocks) % 16 == 0:
            return rows // n_blocks, k
        k *= 2
    raise ValueError(f"cannot stream {rows} rows over {steps} steps")


def _mix_call(x, ada3, norm_w, w_in_t, conv_w, w_gk2, b_gk, gla_norm, w_out, nxt_in, nxt_out,
              *, tt):
    bsz, t, d = x.shape
    nc = tt // CHUNK
    n_t = t // tt
    band = _band_masks()
    kern = functools.partial(_mix_kernel, tt=tt)
    const = lambda b, i: (0, 0)
    single = pl.Buffered(1)
    in_rows, in_k = _side_block(nxt_in.shape[0], bsz * n_t)
    out_rows, out_k = _side_block(nxt_out.shape[0], bsz * n_t)
    in_spec = pl.BlockSpec((in_rows, nxt_in.shape[1]), lambda b, i: ((b * n_t + i) // in_k, 0))
    out_spec = pl.BlockSpec((out_rows, nxt_out.shape[1]), lambda b, i: ((b * n_t + i) // out_k, 0))
    return pl.pallas_call(
        kern,
        grid=(bsz, n_t),
        in_specs=[
            pl.BlockSpec((1, tt, d), lambda b, i: (b, i, 0)),
            pl.BlockSpec((1, N_ADA, d), lambda b, i: (b, 0, 0)),
            pl.BlockSpec((1, d), const),
            pl.BlockSpec(w_in_t.shape, const, pipeline_mode=single),
            pl.BlockSpec(conv_w.shape, const),
            pl.BlockSpec(w_gk2.shape, const),
            pl.BlockSpec(b_gk.shape, const),
            pl.BlockSpec(gla_norm.shape, const),
            pl.BlockSpec(w_out.shape, const, pipeline_mode=single),
            pl.BlockSpec(band.shape, lambda b, i: (0, 0, 0), pipeline_mode=single),
            in_spec,
            out_spec,
        ],
        out_specs=[pl.BlockSpec((1, tt, d), lambda b, i: (b, i, 0)), in_spec, out_spec],
        out_shape=[jax.ShapeDtypeStruct(x.shape, F32),
                   jax.ShapeDtypeStruct(nxt_in.shape, BF16),
                   jax.ShapeDtypeStruct(nxt_out.shape, BF16)],
        scratch_shapes=[
            pltpu.VMEM((GLA_HEADS, GLA_DK, GLA_DV), F32),
            pltpu.VMEM((tt + 8, CONV_WIDTH), F32),
            pltpu.VMEM((tt, KEY_W), F32),
            pltpu.VMEM((nc, PAD + CHUNK, KEY_W), F32),
            pltpu.VMEM((nc, 4 * CHUNK, KEY_W), BF16),
            pltpu.VMEM((nc, PAD + CHUNK, KEY_W), F32),
            pltpu.VMEM((tt, VAL_W), BF16),
            pltpu.VMEM((tt, VAL_W), F32),
            pltpu.VMEM((nc, CHUNK, KEY_W), F32),
            pltpu.VMEM((nc, GLA_HEADS, GLA_DK, GLA_DV), BF16),
        ],
        compiler_params=pltpu.CompilerParams(
            dimension_semantics=("arbitrary", "arbitrary"), vmem_limit_bytes=VMEM_LIMIT),
        name="token_mixer",
    )(x, ada3, norm_w, w_in_t, conv_w, w_gk2, b_gk, gla_norm, w_out, band, nxt_in, nxt_out)


def kernel(x, c, w_ada, b_ada, norm_ffn1, w_ffn1_in, w_ffn1_out, norm_mix, w_mix_in, conv_w,
           w_gk2, b_gk, gla_norm, w_mix_out, norm_ffn2, w_ffn2_in, w_ffn2_out, norm_final):
    bsz, t, d = x.shape
    depth = w_ada.shape[0]
    c_pad = jnp.zeros((8, d), F32).at[:bsz].set(c)
    norm_f = norm_final.reshape(1, d)
    for l in range(depth):
        last = l == depth - 1
        ada = _ada_call(c_pad, w_ada[l], b_ada[l].reshape(1, -1))
        ada3 = ada[:bsz].reshape(bsz, N_ADA, d)
        x = _ffn_call(x, ada3, norm_ffn1[l].reshape(1, d), w_ffn1_in[l], w_ffn1_out[l],
                      norm_f, ada_base=0, final_norm=False, tm=TOKEN_TILE)
        x, w2_in, w2_out = _mix_call(
            x, ada3, norm_mix[l].reshape(1, d), jnp.transpose(w_mix_in[l]), conv_w[l], w_gk2[l],
            b_gk[l].reshape(1, KEY_W), gla_norm[l].reshape(1, GLA_DV), w_mix_out[l],
            w_ffn2_in[l], w_ffn2_out[l], tt=TOKEN_TILE)
        x = _ffn_call(x, ada3, norm_ffn2[l].reshape(1, d), w2_in, w2_out,
                      norm_f, ada_base=6, final_norm=last, tm=2 * TOKEN_TILE)
    return x
```

```python
import functools
import math

import jax
import jax.numpy as jnp
from jax import lax
from jax.experimental import pallas as pl
from jax.experimental.pallas import tpu as pltpu

F32 = jnp.float32
BF16 = jnp.bfloat16

EPS = 1e-6
CONV_WIDTH = 512
GLA_HEADS = 4
GLA_DK = 64
GLA_DV = 128
KEY_W = GLA_HEADS * GLA_DK
VAL_W = GLA_HEADS * GLA_DV
GATE_RANK = 16
GATE_NORMALIZER = 16.0
CHUNK = 64
SUB = 8
NSUB = CHUNK // SUB
PAD = 8
N_ADA = 9
PAIR_W = 2 * GLA_DK
PAIR_V = 2 * GLA_DV
MIX_MAIN = 3 * CONV_WIDTH + 2 * KEY_W + 2 * VAL_W
VMEM_LIMIT = 58 * 1024 * 1024
LOG2E = math.log2(math.e)
FFN_CHUNK = 256
TOKEN_TILE = 512


def _dot(a, b):
    return jnp.dot(a, b, preferred_element_type=F32)


def _norm_mod(x, gain, shift, scale):
    ms = jnp.mean(x * x, axis=-1, keepdims=True)
    y = x * lax.rsqrt(ms + EPS) * gain
    return y * (1.0 + scale) + shift


def _ada_kernel(c_ref, w_ref, b_ref, o_ref):
    c = c_ref[...]
    ca = (c * jax.nn.sigmoid(c)).astype(BF16)
    o_ref[...] = _dot(ca, w_ref[...].astype(BF16)) + b_ref[...]


def _ada_call(c_pad, w_ada, b_ada):
    rows, d = c_pad.shape
    n = w_ada.shape[1]
    tn = 3 * d
    return pl.pallas_call(
        _ada_kernel,
        grid=(n // tn,),
        in_specs=[
            pl.BlockSpec((rows, d), lambda j: (0, 0)),
            pl.BlockSpec((d, tn), lambda j: (0, j)),
            pl.BlockSpec((1, tn), lambda j: (0, j)),
        ],
        out_specs=pl.BlockSpec((rows, tn), lambda j: (0, j)),
        out_shape=jax.ShapeDtypeStruct((rows, n), F32),
        compiler_params=pltpu.CompilerParams(
            dimension_semantics=("arbitrary",), vmem_limit_bytes=VMEM_LIMIT),
        name="ada_proj",
    )(c_pad, w_ada, b_ada)


def _side_spec(shape, n_rows_grid, n_cols_grid):
    steps = n_rows_grid * n_cols_grid
    for axis, unit in ((0, 16), (1, 128)):
        k = 1
        while k < steps:
            n_blocks = steps // k
            size = shape[axis] // n_blocks
            if shape[axis] % n_blocks == 0 and size % unit == 0:
                block = (size, shape[1]) if axis == 0 else (shape[0], size)
                if axis == 0:
                    return pl.BlockSpec(block, lambda b, i, k=k: ((b * n_cols_grid + i) // k, 0))
                return pl.BlockSpec(block, lambda b, i, k=k: (0, (b * n_cols_grid + i) // k))
            k *= 2
    raise ValueError(f"cannot stream a {shape} matrix over {steps} grid steps")


def _cast_side_streams(src_refs, dst_refs):
    for src, dst in zip(src_refs, dst_refs):
        dst[...] = src[...].astype(BF16)


def _ffn_kernel(*refs, ada_base, chunks, d_ff, final_norm, n_side):
    x_ref, ada_ref, nw_ref, win_ref, wout_ref, nf_ref = refs[:6]
    o_ref = refs[6 + n_side]
    a_ref = refs[7 + 2 * n_side]
    _cast_side_streams(refs[6:6 + n_side], refs[7 + n_side:7 + 2 * n_side])
    x = x_ref[0]
    shift = ada_ref[0, ada_base:ada_base + 1, :]
    scale = ada_ref[0, ada_base + 1:ada_base + 2, :]
    gate = ada_ref[0, ada_base + 2:ada_base + 3, :]
    h = _norm_mod(x, nw_ref[...], shift, scale).astype(BF16)
    off = 0
    for w in chunks:
        g = _dot(h, win_ref[:, off:off + w].astype(BF16))
        u = _dot(h, win_ref[:, d_ff + off:d_ff + off + w].astype(BF16))
        a_ref[:, off:off + w] = (g * jax.nn.sigmoid(g) * u).astype(BF16)
        off += w
    y = _dot(a_ref[...], wout_ref[...].astype(BF16))
    out = x + (0.5 * gate) * y
    if final_norm:
        ms = jnp.mean(out * out, axis=-1, keepdims=True)
        out = out * lax.rsqrt(ms + EPS) * nf_ref[...]
    o_ref[0] = out


def _ffn_call(x, ada3, norm_w, w_in, w_out, norm_f, side=(), *, ada_base, final_norm, tm):
    bsz, t, d = x.shape
    d_ff = w_out.shape[0]
    n_t = t // tm
    chunks = [FFN_CHUNK] * (d_ff // FFN_CHUNK)
    if d_ff % FFN_CHUNK:
        chunks.append(d_ff % FFN_CHUNK)
    kern = functools.partial(_ffn_kernel, ada_base=ada_base, chunks=tuple(chunks),
                             d_ff=d_ff, final_norm=final_norm, n_side=len(side))
    const = lambda b, i: (0, 0)
    side_specs = [_side_spec(w.shape, bsz, n_t) for w in side]
    outs = pl.pallas_call(
        kern,
        grid=(bsz, n_t),
        in_specs=[
            pl.BlockSpec((1, tm, d), lambda b, i: (b, i, 0)),
            pl.BlockSpec((1, N_ADA, d), lambda b, i: (b, 0, 0)),
            pl.BlockSpec((1, d), const),
            pl.BlockSpec((d, 2 * d_ff), const, pipeline_mode=pl.Buffered(1)),
            pl.BlockSpec((d_ff, d), const, pipeline_mode=pl.Buffered(1)),
            pl.BlockSpec((1, d), const),
        ] + side_specs,
        out_specs=[pl.BlockSpec((1, tm, d), lambda b, i: (b, i, 0))] + side_specs,
        out_shape=[jax.ShapeDtypeStruct(x.shape, F32)]
        + [jax.ShapeDtypeStruct(w.shape, BF16) for w in side],
        scratch_shapes=[pltpu.VMEM((tm, d_ff), BF16)],
        compiler_params=pltpu.CompilerParams(
            dimension_semantics=("arbitrary", "arbitrary"), vmem_limit_bytes=VMEM_LIMIT),
        name="ffn_final" if final_norm else "ffn",
    )(x, ada3, norm_w, w_in, w_out, norm_f, *side)
    return outs[0], tuple(outs[1:])


def _band_masks():
    i = jnp.arange(CHUNK)[None, :, None]
    j = (jnp.arange(KEY_W) % GLA_DK)[None, None, :]
    d = jnp.arange(SUB)[:, None, None]
    return ((j == i - d) & (i % SUB >= d)).astype(F32)


def _mix_kernel(x_ref, ada_ref, nw_ref, win_ref, convw_ref, wgk2_ref, bgk_ref,
                gn_ref, wout_ref, band_ref, nxt_in_ref, nxt_out_ref,
                o_ref, nxt_in_bf_ref, nxt_out_bf_ref,
                s_ref, u_ref, q_s, k_s, lgs3, b_s, v_s, o_s, sc3, sb_s, *, tt):
    nc = tt // CHUNK
    t_idx = pl.program_id(1)

    @pl.when(t_idx == 0)
    def _():
        s_ref[...] = jnp.zeros(s_ref.shape, F32)
        u_ref[0:8, :] = jnp.zeros((8, CONV_WIDTH), F32)
        k_s[:, 0:PAD, :] = jnp.zeros((nc, PAD, KEY_W), F32)
        b_s[:, 0:PAD, :] = jnp.zeros((nc, PAD, KEY_W), F32)
        lgs3[:, 3 * CHUNK:4 * CHUNK, :] = jnp.zeros((nc, CHUNK, KEY_W), BF16)

    x = x_ref[0]
    shift = ada_ref[0, 3:4, :]
    scale = ada_ref[0, 4:5, :]
    gate = ada_ref[0, 5:6, :]
    h = _norm_mod(x, nw_ref[...], shift, scale).astype(BF16)

    def proj(lo, width):
        return lax.dot_general(h, win_ref[lo:lo + width, :].astype(BF16),
                               (((1,), (1,)), ((), ())), preferred_element_type=F32)

    tri_r = lax.broadcasted_iota(jnp.int32, (CHUNK, 4 * CHUNK), 0)
    tri_c = lax.broadcasted_iota(jnp.int32, (CHUNK, 4 * CHUNK), 1)
    lower4 = ((tri_c % CHUNK) <= tri_r).astype(BF16)
    hr = lax.broadcasted_iota(jnp.int32, (KEY_W, KEY_W), 0) // GLA_DK
    hc = lax.broadcasted_iota(jnp.int32, (KEY_W, KEY_W), 1) // GLA_DK
    head_bd = hr == hc
    ones_bd = head_bd.astype(BF16)
    col_blk = (lax.broadcasted_iota(jnp.int32, (1, KEY_W), 1) % GLA_DK) // SUB
    zero_v = jnp.zeros((CHUNK, GLA_DV), BF16)

    base = 3 * CONV_WIDTH
    q_s[...] = proj(base, KEY_W) * (GLA_DK ** -0.5)
    k_s[:, PAD:PAD + CHUNK, :] = proj(base + KEY_W, KEY_W).reshape(nc, CHUNK, KEY_W)
    v_s[...] = proj(base + 2 * KEY_W, VAL_W).astype(BF16)
    gk_low = proj(MIX_MAIN, GATE_RANK)
    z = _dot(gk_low.astype(BF16), wgk2_ref[...].astype(BF16)) + bgk_ref[...]
    log_sig = jnp.minimum(z, 0.0) - jnp.log1p(jnp.exp(-jnp.abs(z)))
    lg = log_sig * (LOG2E / GATE_NORMALIZER)
    lg_hi = lg.astype(BF16)
    rem = lg - lg_hi.astype(F32)
    lg_mid = rem.astype(BF16)
    lg_lo = (rem - lg_mid.astype(F32)).astype(BF16)
    lgs3[:, 0:CHUNK, :] = lg_hi.reshape(nc, CHUNK, KEY_W)
    lgs3[:, CHUNK:2 * CHUNK, :] = lg_mid.reshape(nc, CHUNK, KEY_W)
    lgs3[:, 2 * CHUNK:3 * CHUNK, :] = lg_lo.reshape(nc, CHUNK, KEY_W)
    for c in range(nc):
        b_s[c, PAD:PAD + CHUNK, :] = _dot(lower4, lgs3[c])

    cb = proj(0, CONV_WIDTH)
    u = proj(CONV_WIDTH, CONV_WIDTH) * proj(2 * CONV_WIDTH, CONV_WIDTH)
    u_ref[8:8 + tt, :] = u
    cw = convw_ref[...]
    conv = (u_ref[6:6 + tt, :] * cw[0:1, :] + u_ref[7:7 + tt, :] * cw[1:2, :]
            + u * cw[2:3, :])
    y_conv = cb * conv
    u_ref[0:8, :] = u_ref[tt:tt + 8, :]

    q = q_s[...]
    k = k_s[:, PAD:PAD + CHUNK, :].reshape(tt, KEY_W)
    b = b_s[:, PAD:PAD + CHUNK, :].reshape(tt, KEY_W)
    b3 = b.reshape(nc, CHUNK, KEY_W)
    q3 = q.reshape(nc, CHUNK, KEY_W)
    b_last = jnp.broadcast_to(b3[:, CHUNK - 1:CHUNK, :], (nc, CHUNK, KEY_W)).reshape(tt, KEY_W)
    b8 = b.reshape(tt // SUB, SUB, KEY_W)
    b_end = jnp.broadcast_to(b8[:, SUB - 1:SUB, :], (tt // SUB, SUB, KEY_W)).reshape(tt, KEY_W)
    qg = (q * jnp.exp2(b)).astype(BF16)
    kd = k * jnp.exp2(b_last - b)
    kk = k * jnp.exp2(b_end - b)

    s = None
    for d in range(SUB):
        if d == 0:
            e = q * k
        else:
            k_d = k_s[:, PAD - d:PAD - d + CHUNK, :].reshape(tt, KEY_W)
            b_d = b_s[:, PAD - d:PAD - d + CHUNK, :].reshape(tt, KEY_W)
            e = q * k_d * jnp.exp2(b - b_d)
        term = _dot(e.astype(BF16), ones_bd).reshape(nc, CHUNK, KEY_W) * band_ref[d]
        s = term if s is None else s + term
    sc3[...] = s

    g_out = proj(base + 2 * KEY_W + VAL_W, VAL_W)

    qj_all = []
    for j in range(NSUB - 1):
        r0 = SUB * (j + 1)
        qj_all.append((q3[:, r0:, :] * jnp.exp2(b3[:, r0:, :] - b3[:, r0 - 1:r0, :])).astype(BF16))
    for c in range(nc):
        kc = kk[c * CHUNK:(c + 1) * CHUNK, :]
        kbd = jnp.where(head_bd, jnp.concatenate([kc] * GLA_HEADS, axis=0), 0.0).astype(BF16)
        lhs = jnp.concatenate([qj[c] for qj in qj_all], axis=0)
        out = lax.dot_general(lhs, kbd, (((1,), (1,)), ((), ())), preferred_element_type=F32)
        off = 0
        for j in range(NSUB - 1):
            r0 = SUB * (j + 1)
            n = CHUNK - r0
            sc3[c, r0:, :] = jnp.where(col_blk == j, out[off:off + n, :], sc3[c, r0:, :])
            off += n

    states = [s_ref[hd] for hd in range(GLA_HEADS)]
    for c in range(nc):
        rows = slice(c * CHUNK, (c + 1) * CHUNK)
        bt = jnp.transpose(b[(c + 1) * CHUNK - 8:(c + 1) * CHUNK, :])
        decay_col = jnp.exp2(bt[:, 7:8])
        for hd in range(GLA_HEADS):
            sb_s[c, hd] = states[hd].astype(BF16)
        for p in range(2):
            lanes = slice(p * PAIR_W, (p + 1) * PAIR_W)
            kdt = jnp.transpose(kd[rows, lanes]).astype(BF16)
            ktv = _dot(kdt, v_s[rows, p * PAIR_V:(p + 1) * PAIR_V])
            dec = decay_col[lanes, :]
            h0, h1 = 2 * p, 2 * p + 1
            states[h0] = states[h0] * dec[0:GLA_DK, :] + ktv[0:GLA_DK, 0:GLA_DV]
            states[h1] = states[h1] * dec[GLA_DK:PAIR_W, :] + ktv[GLA_DK:PAIR_W, GLA_DV:PAIR_V]
    for hd in range(GLA_HEADS):
        s_ref[hd] = states[hd]

    for c in range(nc):
        rows = slice(c * CHUNK, (c + 1) * CHUNK)
        for p in range(2):
            lanes = slice(p * PAIR_W, (p + 1) * PAIR_W)
            h0, h1 = 2 * p, 2 * p + 1
            v0 = v_s[rows, h0 * GLA_DV:(h0 + 1) * GLA_DV]
            v1 = v_s[rows, h1 * GLA_DV:(h1 + 1) * GLA_DV]
            lhs = jnp.concatenate([sc3[c, :, lanes].astype(BF16), qg[rows, lanes]], axis=1)
            w = jnp.concatenate([
                jnp.concatenate([v0, zero_v], axis=1),
                jnp.concatenate([zero_v, v1], axis=1),
                jnp.concatenate([sb_s[c, h0], zero_v], axis=1),
                jnp.concatenate([zero_v, sb_s[c, h1]], axis=1)], axis=0)
            o_s[rows, p * PAIR_V:(p + 1) * PAIR_V] = _dot(lhs, w)

    o = o_s[...]
    gn = gn_ref[...]
    parts = []
    for hd in range(GLA_HEADS):
        oh = o[:, hd * GLA_DV:(hd + 1) * GLA_DV]
        ms = jnp.mean(oh * oh, axis=-1, keepdims=True)
        parts.append(oh * lax.rsqrt(ms + EPS) * gn)
    y_gla = jnp.concatenate(parts, axis=1) * (g_out * jax.nn.sigmoid(g_out))
    y = jnp.concatenate([y_conv, y_gla], axis=1).astype(BF16)
    o_ref[0] = x + gate * _dot(y, wout_ref[...].astype(BF16))
    _cast_side_streams((nxt_in_ref, nxt_out_ref), (nxt_in_bf_ref, nxt_out_bf_ref))


def _mix_call(x, ada3, norm_w, w_in_t, conv_w, w_gk2, b_gk, gla_norm, w_out, nxt_in, nxt_out,
              *, tt):
    bsz, t, d = x.shape
    nc = tt // CHUNK
    n_t = t // tt
    band = _band_masks()
    kern = functools.partial(_mix_kernel, tt=tt)
    const = lambda b, i: (0, 0)
    single = pl.Buffered(1)
    in_spec = _side_spec(nxt_in.shape, bsz, n_t)
    out_spec = _side_spec(nxt_out.shape, bsz, n_t)
    return pl.pallas_call(
        kern,
        grid=(bsz, n_t),
        in_specs=[
            pl.BlockSpec((1, tt, d), lambda b, i: (b, i, 0)),
            pl.BlockSpec((1, N_ADA, d), lambda b, i: (b, 0, 0)),
            pl.BlockSpec((1, d), const),
            pl.BlockSpec(w_in_t.shape, const, pipeline_mode=single),
            pl.BlockSpec(conv_w.shape, const),
            pl.BlockSpec(w_gk2.shape, const),
            pl.BlockSpec(b_gk.shape, const),
            pl.BlockSpec(gla_norm.shape, const),
            pl.BlockSpec(w_out.shape, const, pipeline_mode=single),
            pl.BlockSpec(band.shape, lambda b, i: (0, 0, 0), pipeline_mode=single),
            in_spec,
            out_spec,
        ],
        out_specs=[pl.BlockSpec((1, tt, d), lambda b, i: (b, i, 0)), in_spec, out_spec],
        out_shape=[jax.ShapeDtypeStruct(x.shape, F32),
                   jax.ShapeDtypeStruct(nxt_in.shape, BF16),
                   jax.ShapeDtypeStruct(nxt_out.shape, BF16)],
        scratch_shapes=[
            pltpu.VMEM((GLA_HEADS, GLA_DK, GLA_DV), F32),
            pltpu.VMEM((tt + 8, CONV_WIDTH), F32),
            pltpu.VMEM((tt, KEY_W), F32),
            pltpu.VMEM((nc, PAD + CHUNK, KEY_W), F32),
            pltpu.VMEM((nc, 4 * CHUNK, KEY_W), BF16),
            pltpu.VMEM((nc, PAD + CHUNK, KEY_W), F32),
            pltpu.VMEM((tt, VAL_W), BF16),
            pltpu.VMEM((tt, VAL_W), F32),
            pltpu.VMEM((nc, CHUNK, KEY_W), F32),
            pltpu.VMEM((nc, GLA_HEADS, GLA_DK, GLA_DV), BF16),
        ],
        compiler_params=pltpu.CompilerParams(
            dimension_semantics=("arbitrary", "arbitrary"), vmem_limit_bytes=VMEM_LIMIT),
        name="token_mixer",
    )(x, ada3, norm_w, w_in_t, conv_w, w_gk2, b_gk, gla_norm, w_out, band, nxt_in, nxt_out)


def kernel(x, c, w_ada, b_ada, norm_ffn1, w_ffn1_in, w_ffn1_out, norm_mix, w_mix_in, conv_w,
           w_gk2, b_gk, gla_norm, w_mix_out, norm_ffn2, w_ffn2_in, w_ffn2_out, norm_final):
    bsz, t, d = x.shape
    depth = w_ada.shape[0]
    c_pad = jnp.zeros((8, d), F32).at[:bsz].set(c)
    norm_f = norm_final.reshape(1, d)
    for l in range(depth):
        last = l == depth - 1
        ada = _ada_call(c_pad, w_ada[l], b_ada[l].reshape(1, -1))
        ada3 = ada[:bsz].reshape(bsz, N_ADA, d)
        x, (wm_in_t, wm_out) = _ffn_call(
            x, ada3, norm_ffn1[l].reshape(1, d), w_ffn1_in[l], w_ffn1_out[l], norm_f,
            side=(jnp.transpose(w_mix_in[l]), w_mix_out[l]),
            ada_base=0, final_norm=False, tm=TOKEN_TILE)
        x, w2_in, w2_out = _mix_call(
            x, ada3, norm_mix[l].reshape(1, d), wm_in_t, conv_w[l], w_gk2[l],
            b_gk[l].reshape(1, KEY_W), gla_norm[l].reshape(1, GLA_DV), wm_out,
            w_ffn2_in[l], w_ffn2_out[l], tt=2 * TOKEN_TILE)
        x, _ = _ffn_call(x, ada3, norm_ffn2[l].reshape(1, d), w2_in, w2_out, norm_f,
                         ada_base=6, final_norm=last, tm=2 * TOKEN_TILE)
    return x
```

```python
import functools
import math

import jax
import jax.numpy as jnp
from jax import lax
from jax.experimental import pallas as pl
from jax.experimental.pallas import tpu as pltpu

F32 = jnp.float32
BF16 = jnp.bfloat16

EPS = 1e-6
CONV_WIDTH = 512
GLA_HEADS = 4
GLA_DK = 64
GLA_DV = 128
KEY_W = GLA_HEADS * GLA_DK
VAL_W = GLA_HEADS * GLA_DV
GATE_RANK = 16
GATE_NORMALIZER = 16.0
CHUNK = 64
SUB = 8
NSUB = CHUNK // SUB
PAD = 8
N_ADA = 9
PAIR_W = 2 * GLA_DK
PAIR_V = 2 * GLA_DV
MIX_MAIN = 3 * CONV_WIDTH + 2 * KEY_W + 2 * VAL_W
VMEM_LIMIT = 58 * 1024 * 1024
LOG2E = math.log2(math.e)
FFN_CHUNK = 256
ADA_K_BLOCK = 128
TOKEN_TILE = 512


def _dot(a, b):
    return jnp.dot(a, b, preferred_element_type=F32)


def _norm_mod(x, gain, shift, scale):
    ms = jnp.mean(x * x, axis=-1, keepdims=True)
    y = x * lax.rsqrt(ms + EPS) * gain
    return y * (1.0 + scale) + shift


def _ada_kernel(c_ref, w_ref, b_ref, o_ref):
    @pl.when(pl.program_id(0) == 0)
    def _():
        o_ref[...] = jnp.broadcast_to(b_ref[...], o_ref.shape)

    c = c_ref[...]
    ca = (c * jax.nn.sigmoid(c)).astype(BF16)
    o_ref[...] += _dot(ca, w_ref[...].astype(BF16))


def _ada_call(c_pad, w_ada, b_ada):
    rows, d = c_pad.shape
    n = w_ada.shape[1]
    tk = ADA_K_BLOCK
    return pl.pallas_call(
        _ada_kernel,
        grid=(d // tk,),
        in_specs=[
            pl.BlockSpec((rows, tk), lambda k: (0, k)),
            pl.BlockSpec((tk, n), lambda k: (k, 0)),
            pl.BlockSpec((1, n), lambda k: (0, 0)),
        ],
        out_specs=pl.BlockSpec((rows, n), lambda k: (0, 0)),
        out_shape=jax.ShapeDtypeStruct((rows, n), F32),
        compiler_params=pltpu.CompilerParams(
            dimension_semantics=("arbitrary",), vmem_limit_bytes=VMEM_LIMIT),
        name="ada_proj",
    )(c_pad, w_ada, b_ada)


def _side_spec(shape, n_rows_grid, n_cols_grid):
    steps = n_rows_grid * n_cols_grid
    for axis, unit in ((0, 16), (1, 128)):
        k = 1
        while k < steps:
            n_blocks = steps // k
            size = shape[axis] // n_blocks
            if shape[axis] % n_blocks == 0 and size % unit == 0:
                block = (size, shape[1]) if axis == 0 else (shape[0], size)
                if axis == 0:
                    return pl.BlockSpec(block, lambda b, i, k=k: ((b * n_cols_grid + i) // k, 0))
                return pl.BlockSpec(block, lambda b, i, k=k: (0, (b * n_cols_grid + i) // k))
            k *= 2
    raise ValueError(f"cannot stream a {shape} matrix over {steps} grid steps")


def _cast_side_streams(src_refs, dst_refs):
    for src, dst in zip(src_refs, dst_refs):
        dst[...] = src[...].astype(BF16)


def _ffn_kernel(*refs, ada_base, chunks, d_ff, final_norm, n_side):
    x_ref, ada_ref, nw_ref, win_ref, wout_ref, nf_ref = refs[:6]
    o_ref = refs[6 + n_side]
    a_ref = refs[7 + 2 * n_side]
    _cast_side_streams(refs[6:6 + n_side], refs[7 + n_side:7 + 2 * n_side])
    x = x_ref[0]
    shift = ada_ref[0, ada_base:ada_base + 1, :]
    scale = ada_ref[0, ada_base + 1:ada_base + 2, :]
    gate = ada_ref[0, ada_base + 2:ada_base + 3, :]
    h = _norm_mod(x, nw_ref[...], shift, scale).astype(BF16)
    off = 0
    for w in chunks:
        g = _dot(h, win_ref[:, off:off + w].astype(BF16))
        u = _dot(h, win_ref[:, d_ff + off:d_ff + off + w].astype(BF16))
        a_ref[:, off:off + w] = (g * jax.nn.sigmoid(g) * u).astype(BF16)
        off += w
    y = _dot(a_ref[...], wout_ref[...].astype(BF16))
    out = x + (0.5 * gate) * y
    if final_norm:
        ms = jnp.mean(out * out, axis=-1, keepdims=True)
        out = out * lax.rsqrt(ms + EPS) * nf_ref[...]
    o_ref[0] = out


def _ffn_call(x, ada3, norm_w, w_in, w_out, norm_f, side=(), *, ada_base, final_norm, tm):
    bsz, t, d = x.shape
    d_ff = w_out.shape[0]
    n_t = t // tm
    chunks = [FFN_CHUNK] * (d_ff // FFN_CHUNK)
    if d_ff % FFN_CHUNK:
        chunks.append(d_ff % FFN_CHUNK)
    kern = functools.partial(_ffn_kernel, ada_base=ada_base, chunks=tuple(chunks),
                             d_ff=d_ff, final_norm=final_norm, n_side=len(side))
    const = lambda b, i: (0, 0)
    side_specs = [_side_spec(w.shape, bsz, n_t) for w in side]
    outs = pl.pallas_call(
        kern,
        grid=(bsz, n_t),
        in_specs=[
            pl.BlockSpec((1, tm, d), lambda b, i: (b, i, 0)),
            pl.BlockSpec((1, N_ADA, d), lambda b, i: (b, 0, 0)),
            pl.BlockSpec((1, d), const),
            pl.BlockSpec((d, 2 * d_ff), const, pipeline_mode=pl.Buffered(1)),
            pl.BlockSpec((d_ff, d), const, pipeline_mode=pl.Buffered(1)),
            pl.BlockSpec((1, d), const),
        ] + side_specs,
        out_specs=[pl.BlockSpec((1, tm, d), lambda b, i: (b, i, 0))] + side_specs,
        out_shape=[jax.ShapeDtypeStruct(x.shape, F32)]
        + [jax.ShapeDtypeStruct(w.shape, BF16) for w in side],
        scratch_shapes=[pltpu.VMEM((tm, d_ff), BF16)],
        compiler_params=pltpu.CompilerParams(
            dimension_semantics=("arbitrary", "arbitrary"), vmem_limit_bytes=VMEM_LIMIT),
        name="ffn_final" if final_norm else "ffn",
    )(x, ada3, norm_w, w_in, w_out, norm_f, *side)
    return outs[0], tuple(outs[1:])


def _band_masks():
    i = jnp.arange(CHUNK)[None, :, None]
    j = (jnp.arange(KEY_W) % GLA_DK)[None, None, :]
    d = jnp.arange(SUB)[:, None, None]
    return ((j == i - d) & (i % SUB >= d)).astype(F32)


def _mix_kernel(x_ref, ada_ref, nw_ref, win_ref, convw_ref, wgk2_ref, bgk_ref,
                gn_ref, wout_ref, band_ref, nxt_in_ref, nxt_out_ref,
                o_ref, nxt_in_bf_ref, nxt_out_bf_ref,
                s_ref, u_ref, q_s, k_s, lgs3, b_s, v_s, o_s, sc3, sb_s, *, tt):
    nc = tt // CHUNK
    t_idx = pl.program_id(1)

    @pl.when(t_idx == 0)
    def _():
        s_ref[...] = jnp.zeros(s_ref.shape, F32)
        u_ref[0:8, :] = jnp.zeros((8, CONV_WIDTH), F32)
        k_s[:, 0:PAD, :] = jnp.zeros((nc, PAD, KEY_W), F32)
        b_s[:, 0:PAD, :] = jnp.zeros((nc, PAD, KEY_W), F32)
        lgs3[:, 3 * CHUNK:4 * CHUNK, :] = jnp.zeros((nc, CHUNK, KEY_W), BF16)

    x = x_ref[0]
    shift = ada_ref[0, 3:4, :]
    scale = ada_ref[0, 4:5, :]
    gate = ada_ref[0, 5:6, :]
    h = _norm_mod(x, nw_ref[...], shift, scale).astype(BF16)

    def proj(lo, width):
        return lax.dot_general(h, win_ref[lo:lo + width, :].astype(BF16),
                               (((1,), (1,)), ((), ())), preferred_element_type=F32)

    tri_r = lax.broadcasted_iota(jnp.int32, (CHUNK, 4 * CHUNK), 0)
    tri_c = lax.broadcasted_iota(jnp.int32, (CHUNK, 4 * CHUNK), 1)
    lower4 = ((tri_c % CHUNK) <= tri_r).astype(BF16)
    hr = lax.broadcasted_iota(jnp.int32, (KEY_W, KEY_W), 0) // GLA_DK
    hc = lax.broadcasted_iota(jnp.int32, (KEY_W, KEY_W), 1) // GLA_DK
    head_bd = hr == hc
    ones_bd = head_bd.astype(BF16)
    col_blk = (lax.broadcasted_iota(jnp.int32, (1, KEY_W), 1) % GLA_DK) // SUB
    zero_v = jnp.zeros((CHUNK, GLA_DV), BF16)

    base = 3 * CONV_WIDTH
    q_s[...] = proj(base, KEY_W) * (GLA_DK ** -0.5)
    k_s[:, PAD:PAD + CHUNK, :] = proj(base + KEY_W, KEY_W).reshape(nc, CHUNK, KEY_W)
    v_s[...] = proj(base + 2 * KEY_W, VAL_W).astype(BF16)
    gk_t = lax.dot_general(win_ref[MIX_MAIN:MIX_MAIN + GATE_RANK, :].astype(BF16), h,
                           (((1,), (1,)), ((), ())), preferred_element_type=F32)
    z = lax.dot_general(gk_t.astype(BF16), wgk2_ref[...].astype(BF16),
                        (((0,), (0,)), ((), ())), preferred_element_type=F32) + bgk_ref[...]
    log_sig = jnp.minimum(z, 0.0) - jnp.log1p(jnp.exp(-jnp.abs(z)))
    lg = log_sig * (LOG2E / GATE_NORMALIZER)
    lg_hi = lg.astype(BF16)
    rem = lg - lg_hi.astype(F32)
    lg_mid = rem.astype(BF16)
    lg_lo = (rem - lg_mid.astype(F32)).astype(BF16)
    lgs3[:, 0:CHUNK, :] = lg_hi.reshape(nc, CHUNK, KEY_W)
    lgs3[:, CHUNK:2 * CHUNK, :] = lg_mid.reshape(nc, CHUNK, KEY_W)
    lgs3[:, 2 * CHUNK:3 * CHUNK, :] = lg_lo.reshape(nc, CHUNK, KEY_W)
    for c in range(nc):
        b_s[c, PAD:PAD + CHUNK, :] = _dot(lower4, lgs3[c])

    cb = proj(0, CONV_WIDTH)
    u = proj(CONV_WIDTH, CONV_WIDTH) * proj(2 * CONV_WIDTH, CONV_WIDTH)
    u_ref[8:8 + tt, :] = u
    cw = convw_ref[...]
    conv = (u_ref[6:6 + tt, :] * cw[0:1, :] + u_ref[7:7 + tt, :] * cw[1:2, :]
            + u * cw[2:3, :])
    y_conv = cb * conv
    u_ref[0:8, :] = u_ref[tt:tt + 8, :]

    q = q_s[...]
    k = k_s[:, PAD:PAD + CHUNK, :].reshape(tt, KEY_W)
    b = b_s[:, PAD:PAD + CHUNK, :].reshape(tt, KEY_W)
    b3 = b.reshape(nc, CHUNK, KEY_W)
    q3 = q.reshape(nc, CHUNK, KEY_W)
    b_last = jnp.broadcast_to(b3[:, CHUNK - 1:CHUNK, :], (nc, CHUNK, KEY_W)).reshape(tt, KEY_W)
    b8 = b.reshape(tt // SUB, SUB, KEY_W)
    b_end = jnp.broadcast_to(b8[:, SUB - 1:SUB, :], (tt // SUB, SUB, KEY_W)).reshape(tt, KEY_W)
    qg = (q * jnp.exp2(b)).astype(BF16)
    kd = k * jnp.exp2(b_last - b)
    kk = k * jnp.exp2(b_end - b)

    s = None
    for d in range(SUB):
        if d == 0:
            e = q * k
        else:
            k_d = k_s[:, PAD - d:PAD - d + CHUNK, :].reshape(tt, KEY_W)
            b_d = b_s[:, PAD - d:PAD - d + CHUNK, :].reshape(tt, KEY_W)
            e = q * k_d * jnp.exp2(b - b_d)
        term = _dot(e.astype(BF16), ones_bd).reshape(nc, CHUNK, KEY_W) * band_ref[d]
        s = term if s is None else s + term
    sc3[...] = s

    g_out = proj(base + 2 * KEY_W + VAL_W, VAL_W)

    qj_all = []
    for j in range(NSUB - 1):
        r0 = SUB * (j + 1)
        qj_all.append((q3[:, r0:, :] * jnp.exp2(b3[:, r0:, :] - b3[:, r0 - 1:r0, :])).astype(BF16))
    for c in range(nc):
        kc = kk[c * CHUNK:(c + 1) * CHUNK, :]
        kbd = jnp.where(head_bd, jnp.concatenate([kc] * GLA_HEADS, axis=0), 0.0).astype(BF16)
        lhs = jnp.concatenate([qj[c] for qj in qj_all], axis=0)
        out = lax.dot_general(lhs, kbd, (((1,), (1,)), ((), ())), preferred_element_type=F32)
        off = 0
        for j in range(NSUB - 1):
            r0 = SUB * (j + 1)
            n = CHUNK - r0
            sc3[c, r0:, :] = jnp.where(col_blk == j, out[off:off + n, :], sc3[c, r0:, :])
            off += n

    states = [s_ref[hd] for hd in range(GLA_HEADS)]
    for c in range(nc):
        rows = slice(c * CHUNK, (c + 1) * CHUNK)
        bt = jnp.transpose(b[(c + 1) * CHUNK - 8:(c + 1) * CHUNK, :])
        decay_col = jnp.exp2(bt[:, 7:8])
        for hd in range(GLA_HEADS):
            sb_s[c, hd] = states[hd].astype(BF16)
        for p in range(2):
            lanes = slice(p * PAIR_W, (p + 1) * PAIR_W)
            kdt = jnp.transpose(kd[rows, lanes]).astype(BF16)
            ktv = _dot(kdt, v_s[rows, p * PAIR_V:(p + 1) * PAIR_V])
            dec = decay_col[lanes, :]
            h0, h1 = 2 * p, 2 * p + 1
            states[h0] = states[h0] * dec[0:GLA_DK, :] + ktv[0:GLA_DK, 0:GLA_DV]
            states[h1] = states[h1] * dec[GLA_DK:PAIR_W, :] + ktv[GLA_DK:PAIR_W, GLA_DV:PAIR_V]
    for hd in range(GLA_HEADS):
        s_ref[hd] = states[hd]

    for c in range(nc):
        rows = slice(c * CHUNK, (c + 1) * CHUNK)
        for p in range(2):
            lanes = slice(p * PAIR_W, (p + 1) * PAIR_W)
            h0, h1 = 2 * p, 2 * p + 1
            v0 = v_s[rows, h0 * GLA_DV:(h0 + 1) * GLA_DV]
            v1 = v_s[rows, h1 * GLA_DV:(h1 + 1) * GLA_DV]
            lhs = jnp.concatenate([sc3[c, :, lanes].astype(BF16), qg[rows, lanes]], axis=1)
            w = jnp.concatenate([
                jnp.concatenate([v0, zero_v], axis=1),
                jnp.concatenate([zero_v, v1], axis=1),
                jnp.concatenate([sb_s[c, h0], zero_v], axis=1),
                jnp.concatenate([zero_v, sb_s[c, h1]], axis=1)], axis=0)
            o_s[rows, p * PAIR_V:(p + 1) * PAIR_V] = _dot(lhs, w)

    o = o_s[...]
    gn = gn_ref[...]
    parts = []
    for hd in range(GLA_HEADS):
        oh = o[:, hd * GLA_DV:(hd + 1) * GLA_DV]
        ms = jnp.mean(oh * oh, axis=-1, keepdims=True)
        parts.append(oh * lax.rsqrt(ms + EPS) * gn)
    y_gla = jnp.concatenate(parts, axis=1) * (g_out * jax.nn.sigmoid(g_out))
    y = jnp.concatenate([y_conv, y_gla], axis=1).astype(BF16)
    o_ref[0] = x + gate * _dot(y, wout_ref[...].astype(BF16))
    _cast_side_streams((nxt_in_ref, nxt_out_ref), (nxt_in_bf_ref, nxt_out_bf_ref))


def _mix_call(x, ada3, norm_w, w_in_t, conv_w, w_gk2, b_gk, gla_norm, w_out, nxt_in, nxt_out,
              *, tt):
    bsz, t, d = x.shape
    nc = tt // CHUNK
    n_t = t // tt
    band = _band_masks()
    kern = functools.partial(_mix_kernel, tt=tt)
    const = lambda b, i: (0, 0)
    single = pl.Buffered(1)
    in_spec = _side_spec(nxt_in.shape, bsz, n_t)
    out_spec = _side_spec(nxt_out.shape, bsz, n_t)
    return pl.pallas_call(
        kern,
        grid=(bsz, n_t),
        in_specs=[
            pl.BlockSpec((1, tt, d), lambda b, i: (b, i, 0)),
            pl.BlockSpec((1, N_ADA, d), lambda b, i: (b, 0, 0)),
            pl.BlockSpec((1, d), const),
            pl.BlockSpec(w_in_t.shape, const, pipeline_mode=single),
            pl.BlockSpec(conv_w.shape, const),
            pl.BlockSpec(w_gk2.shape, const),
            pl.BlockSpec(b_gk.shape, const),
            pl.BlockSpec(gla_norm.shape, const),
            pl.BlockSpec(w_out.shape, const, pipeline_mode=single),
            pl.BlockSpec(band.shape, lambda b, i: (0, 0, 0), pipeline_mode=single),
            in_spec,
            out_spec,
        ],
        out_specs=[pl.BlockSpec((1, tt, d), lambda b, i: (b, i, 0)), in_spec, out_spec],
        out_shape=[jax.ShapeDtypeStruct(x.shape, F32),
                   jax.ShapeDtypeStruct(nxt_in.shape, BF16),
                   jax.ShapeDtypeStruct(nxt_out.shape, BF16)],
        scratch_shapes=[
            pltpu.VMEM((GLA_HEADS, GLA_DK, GLA_DV), F32),
            pltpu.VMEM((tt + 8, CONV_WIDTH), F32),
            pltpu.VMEM((tt, KEY_W), F32),
            pltpu.VMEM((nc, PAD + CHUNK, KEY_W), F32),
            pltpu.VMEM((nc, 4 * CHUNK, KEY_W), BF16),
            pltpu.VMEM((nc, PAD + CHUNK, KEY_W), F32),
            pltpu.VMEM((tt, VAL_W), BF16),
            pltpu.VMEM((tt, VAL_W), F32),
            pltpu.VMEM((nc, CHUNK, KEY_W), F32),
            pltpu.VMEM((nc, GLA_HEADS, GLA_DK, GLA_DV), BF16),
        ],
        compiler_params=pltpu.CompilerParams(
            dimension_semantics=("arbitrary", "arbitrary"), vmem_limit_bytes=VMEM_LIMIT),
        name="token_mixer",
    )(x, ada3, norm_w, w_in_t, conv_w, w_gk2, b_gk, gla_norm, w_out, band, nxt_in, nxt_out)


def kernel(x, c, w_ada, b_ada, norm_ffn1, w_ffn1_in, w_ffn1_out, norm_mix, w_mix_in, conv_w,
           w_gk2, b_gk, gla_norm, w_mix_out, norm_ffn2, w_ffn2_in, w_ffn2_out, norm_final):
    bsz, t, d = x.shape
    depth = w_ada.shape[0]
    c_pad = jnp.zeros((8, d), F32).at[:bsz].set(c)
    norm_f = norm_final.reshape(1, d)
    for l in range(depth):
        last = l == depth - 1
        ada = _ada_call(c_pad, w_ada[l], b_ada[l].reshape(1, -1))
        ada3 = ada[:bsz].reshape(bsz, N_ADA, d)
        x, (wm_in_t, wm_out) = _ffn_call(
            x, ada3, norm_ffn1[l].reshape(1, d), w_ffn1_in[l], w_ffn1_out[l], norm_f,
            side=(jnp.transpose(w_mix_in[l]), w_mix_out[l]),
            ada_base=0, final_norm=False, tm=TOKEN_TILE)
        x, w2_in, w2_out = _mix_call(
            x, ada3, norm_mix[l].reshape(1, d), wm_in_t, conv_w[l], w_gk2[l],
            b_gk[l].reshape(1, KEY_W), gla_norm[l].reshape(1, GLA_DV), wm_out,
            w_ffn2_in[l], w_ffn2_out[l], tt=2 * TOKEN_TILE)
        x, _ = _ffn_call(x, ada3, norm_ffn2[l].reshape(1, d), w2_in, w2_out, norm_f,
                         ada_base=6, final_norm=last, tm=2 * TOKEN_TILE)
    return x
```

```python
import functools
import math

import jax
import jax.numpy as jnp
from jax import lax
from jax.experimental import pallas as pl
from jax.experimental.pallas import tpu as pltpu

F32 = jnp.float32
BF16 = jnp.bfloat16

EPS = 1e-6
CONV_WIDTH = 512
GLA_HEADS = 4
GLA_DK = 64
GLA_DV = 128
KEY_W = GLA_HEADS * GLA_DK
VAL_W = GLA_HEADS * GLA_DV
GATE_RANK = 16
GATE_NORMALIZER = 16.0
CHUNK = 64
SUB = 8
NSUB = CHUNK // SUB
PAD = 8
N_ADA = 9
PAIR_W = 2 * GLA_DK
PAIR_V = 2 * GLA_DV
MIX_MAIN = 3 * CONV_WIDTH + 2 * KEY_W + 2 * VAL_W
VMEM_LIMIT = 58 * 1024 * 1024
LOG2E = math.log2(math.e)
FFN_CHUNK = 256
ADA_K_BLOCK = 128
TOKEN_TILE = 1024


def _dot(a, b):
    return jnp.dot(a, b, preferred_element_type=F32)


def _norm_mod(x, gain, shift, scale):
    ms = jnp.mean(x * x, axis=-1, keepdims=True)
    y = x * lax.rsqrt(ms + EPS) * gain
    return y * (1.0 + scale) + shift


def _side_spec(shape, n_steps, step_of):
    for axis, unit in ((0, 16), (1, 128)):
        k = 1
        while k < n_steps:
            n_blocks = n_steps // k
            size = shape[axis] // n_blocks
            if shape[axis] % n_blocks == 0 and size % unit == 0:
                block = (size, shape[1]) if axis == 0 else (shape[0], size)
                if axis == 0:
                    return pl.BlockSpec(block, lambda *g, k=k: (step_of(*g) // k, 0))
                return pl.BlockSpec(block, lambda *g, k=k: (0, step_of(*g) // k))
            k *= 2
    raise ValueError(f"cannot stream a {shape} matrix over {n_steps} grid steps")


def _cast_side_streams(src_refs, dst_refs):
    for src, dst in zip(src_refs, dst_refs):
        dst[...] = src[...].astype(BF16)


def _ada_kernel(*refs, n_side):
    c_ref, w_ref, b_ref = refs[:3]
    o_ref = refs[3 + n_side]
    _cast_side_streams(refs[3:3 + n_side], refs[4 + n_side:4 + 2 * n_side])

    @pl.when(pl.program_id(0) == 0)
    def _():
        o_ref[...] = jnp.broadcast_to(b_ref[...], o_ref.shape)

    c = c_ref[...]
    ca = (c * jax.nn.sigmoid(c)).astype(BF16)
    o_ref[...] += _dot(ca, w_ref[...].astype(BF16))


def _ada_call(c_pad, w_ada, b_ada, side=()):
    rows, d = c_pad.shape
    n = w_ada.shape[1]
    tk = ADA_K_BLOCK
    side_specs = [_side_spec(w.shape, d // tk, lambda k: k) for w in side]
    outs = pl.pallas_call(
        functools.partial(_ada_kernel, n_side=len(side)),
        grid=(d // tk,),
        in_specs=[
            pl.BlockSpec((rows, tk), lambda k: (0, k)),
            pl.BlockSpec((tk, n), lambda k: (k, 0)),
            pl.BlockSpec((1, n), lambda k: (0, 0)),
        ] + side_specs,
        out_specs=[pl.BlockSpec((rows, n), lambda k: (0, 0))] + side_specs,
        out_shape=[jax.ShapeDtypeStruct((rows, n), F32)]
        + [jax.ShapeDtypeStruct(w.shape, BF16) for w in side],
        compiler_params=pltpu.CompilerParams(
            dimension_semantics=("arbitrary",), vmem_limit_bytes=VMEM_LIMIT),
        name="ada_proj",
    )(c_pad, w_ada, b_ada, *side)
    return outs[0], tuple(outs[1:])


def _ffn_kernel(*refs, ada_base, chunks, d_ff, final_norm, n_side):
    x_ref, ada_ref, nw_ref, win_ref, wout_ref, nf_ref = refs[:6]
    o_ref = refs[6 + n_side]
    a_ref = refs[7 + 2 * n_side]
    _cast_side_streams(refs[6:6 + n_side], refs[7 + n_side:7 + 2 * n_side])
    x = x_ref[0]
    shift = ada_ref[0, ada_base:ada_base + 1, :]
    scale = ada_ref[0, ada_base + 1:ada_base + 2, :]
    gate = ada_ref[0, ada_base + 2:ada_base + 3, :]
    h = _norm_mod(x, nw_ref[...], shift, scale).astype(BF16)
    off = 0
    for w in chunks:
        g = _dot(h, win_ref[:, off:off + w].astype(BF16))
        u = _dot(h, win_ref[:, d_ff + off:d_ff + off + w].astype(BF16))
        a_ref[:, off:off + w] = (g * jax.nn.sigmoid(g) * u).astype(BF16)
        off += w
    y = _dot(a_ref[...], wout_ref[...].astype(BF16))
    out = x + (0.5 * gate) * y
    if final_norm:
        ms = jnp.mean(out * out, axis=-1, keepdims=True)
        out = out * lax.rsqrt(ms + EPS) * nf_ref[...]
    o_ref[0] = out


def _ffn_call(x, ada3, norm_w, w_in, w_out, norm_f, side=(), *, ada_base, final_norm, tm):
    bsz, t, d = x.shape
    d_ff = w_out.shape[0]
    n_t = t // tm
    chunks = [FFN_CHUNK] * (d_ff // FFN_CHUNK)
    if d_ff % FFN_CHUNK:
        chunks.append(d_ff % FFN_CHUNK)
    kern = functools.partial(_ffn_kernel, ada_base=ada_base, chunks=tuple(chunks),
                             d_ff=d_ff, final_norm=final_norm, n_side=len(side))
    const = lambda b, i: (0, 0)
    side_specs = [_side_spec(w.shape, bsz * n_t, lambda b, i: b * n_t + i) for w in side]
    outs = pl.pallas_call(
        kern,
        grid=(bsz, n_t),
        in_specs=[
            pl.BlockSpec((1, tm, d), lambda b, i: (b, i, 0)),
            pl.BlockSpec((1, N_ADA, d), lambda b, i: (b, 0, 0)),
            pl.BlockSpec((1, d), const),
            pl.BlockSpec((d, 2 * d_ff), const, pipeline_mode=pl.Buffered(1)),
            pl.BlockSpec((d_ff, d), const, pipeline_mode=pl.Buffered(1)),
            pl.BlockSpec((1, d), const),
        ] + side_specs,
        out_specs=[pl.BlockSpec((1, tm, d), lambda b, i: (b, i, 0))] + side_specs,
        out_shape=[jax.ShapeDtypeStruct(x.shape, F32)]
        + [jax.ShapeDtypeStruct(w.shape, BF16) for w in side],
        scratch_shapes=[pltpu.VMEM((tm, d_ff), BF16)],
        compiler_params=pltpu.CompilerParams(
            dimension_semantics=("arbitrary", "arbitrary"), vmem_limit_bytes=VMEM_LIMIT),
        name="ffn_final" if final_norm else "ffn",
    )(x, ada3, norm_w, w_in, w_out, norm_f, *side)
    return outs[0], tuple(outs[1:])


def _band_masks():
    i = jnp.arange(CHUNK)[None, :, None]
    j = (jnp.arange(KEY_W) % GLA_DK)[None, None, :]
    d = jnp.arange(SUB)[:, None, None]
    return ((j == i - d) & (i % SUB >= d)).astype(F32)


def _mix_kernel(x_ref, ada_ref, nw_ref, win_ref, convw_ref, wgk2_ref, bgk_ref,
                gn_ref, wout_ref, band_ref, nxt_in_ref, nxt_out_ref,
                o_ref, nxt_in_bf_ref, nxt_out_bf_ref,
                s_ref, u_ref, q_s, k_s, lgs3, b_s, v_s, o_s, sc3, sb_s, *, tt):
    nc = tt // CHUNK
    t_idx = pl.program_id(1)

    @pl.when(t_idx == 0)
    def _():
        s_ref[...] = jnp.zeros(s_ref.shape, F32)
        u_ref[0:8, :] = jnp.zeros((8, CONV_WIDTH), F32)
        k_s[:, 0:PAD, :] = jnp.zeros((nc, PAD, KEY_W), F32)
        b_s[:, 0:PAD, :] = jnp.zeros((nc, PAD, KEY_W), F32)
        lgs3[:, 3 * CHUNK:4 * CHUNK, :] = jnp.zeros((nc, CHUNK, KEY_W), BF16)

    x = x_ref[0]
    shift = ada_ref[0, 3:4, :]
    scale = ada_ref[0, 4:5, :]
    gate = ada_ref[0, 5:6, :]
    h = _norm_mod(x, nw_ref[...], shift, scale).astype(BF16)

    def proj(lo, width):
        return lax.dot_general(h, win_ref[lo:lo + width, :].astype(BF16),
                               (((1,), (1,)), ((), ())), preferred_element_type=F32)

    tri_r = lax.broadcasted_iota(jnp.int32, (CHUNK, 4 * CHUNK), 0)
    tri_c = lax.broadcasted_iota(jnp.int32, (CHUNK, 4 * CHUNK), 1)
    lower4 = ((tri_c % CHUNK) <= tri_r).astype(BF16)
    hr = lax.broadcasted_iota(jnp.int32, (KEY_W, KEY_W), 0) // GLA_DK
    hc = lax.broadcasted_iota(jnp.int32, (KEY_W, KEY_W), 1) // GLA_DK
    head_bd = hr == hc
    ones_bd = head_bd.astype(BF16)
    col_blk = (lax.broadcasted_iota(jnp.int32, (1, KEY_W), 1) % GLA_DK) // SUB
    zero_v = jnp.zeros((CHUNK, GLA_DV), BF16)

    base = 3 * CONV_WIDTH
    q_s[...] = proj(base, KEY_W) * (GLA_DK ** -0.5)
    k_s[:, PAD:PAD + CHUNK, :] = proj(base + KEY_W, KEY_W).reshape(nc, CHUNK, KEY_W)
    v_s[...] = proj(base + 2 * KEY_W, VAL_W).astype(BF16)
    gk_t = lax.dot_general(win_ref[MIX_MAIN:MIX_MAIN + GATE_RANK, :].astype(BF16), h,
                           (((1,), (1,)), ((), ())), preferred_element_type=F32)
    z = lax.dot_general(gk_t.astype(BF16), wgk2_ref[...].astype(BF16),
                        (((0,), (0,)), ((), ())), preferred_element_type=F32) + bgk_ref[...]
    log_sig = jnp.minimum(z, 0.0) - jnp.log1p(jnp.exp(-jnp.abs(z)))
    lg = log_sig * (LOG2E / GATE_NORMALIZER)
    lg_hi = lg.astype(BF16)
    rem = lg - lg_hi.astype(F32)
    lg_mid = rem.astype(BF16)
    lg_lo = (rem - lg_mid.astype(F32)).astype(BF16)
    lgs3[:, 0:CHUNK, :] = lg_hi.reshape(nc, CHUNK, KEY_W)
    lgs3[:, CHUNK:2 * CHUNK, :] = lg_mid.reshape(nc, CHUNK, KEY_W)
    lgs3[:, 2 * CHUNK:3 * CHUNK, :] = lg_lo.reshape(nc, CHUNK, KEY_W)
    for c in range(nc):
        b_s[c, PAD:PAD + CHUNK, :] = _dot(lower4, lgs3[c])

    cb = proj(0, CONV_WIDTH)
    u = proj(CONV_WIDTH, CONV_WIDTH) * proj(2 * CONV_WIDTH, CONV_WIDTH)
    u_ref[8:8 + tt, :] = u
    cw = convw_ref[...]
    conv = (u_ref[6:6 + tt, :] * cw[0:1, :] + u_ref[7:7 + tt, :] * cw[1:2, :]
            + u * cw[2:3, :])
    y_conv = cb * conv
    u_ref[0:8, :] = u_ref[tt:tt + 8, :]

    q = q_s[...]
    k = k_s[:, PAD:PAD + CHUNK, :].reshape(tt, KEY_W)
    b = b_s[:, PAD:PAD + CHUNK, :].reshape(tt, KEY_W)
    b3 = b.reshape(nc, CHUNK, KEY_W)
    q3 = q.reshape(nc, CHUNK, KEY_W)
    b_last = jnp.broadcast_to(b3[:, CHUNK - 1:CHUNK, :], (nc, CHUNK, KEY_W)).reshape(tt, KEY_W)
    b8 = b.reshape(tt // SUB, SUB, KEY_W)
    b_end = jnp.broadcast_to(b8[:, SUB - 1:SUB, :], (tt // SUB, SUB, KEY_W)).reshape(tt, KEY_W)
    qg = (q * jnp.exp2(b)).astype(BF16)
    kd = k * jnp.exp2(b_last - b)
    kk = k * jnp.exp2(b_end - b)

    s = None
    for d in range(SUB):
        if d == 0:
            e = q * k
        else:
            k_d = k_s[:, PAD - d:PAD - d + CHUNK, :].reshape(tt, KEY_W)
            b_d = b_s[:, PAD - d:PAD - d + CHUNK, :].reshape(tt, KEY_W)
            e = q * k_d * jnp.exp2(b - b_d)
        term = _dot(e.astype(BF16), ones_bd).reshape(nc, CHUNK, KEY_W) * band_ref[d]
        s = term if s is None else s + term
    sc3[...] = s

    g_out = proj(base + 2 * KEY_W + VAL_W, VAL_W)

    qj_all = []
    for j in range(NSUB - 1):
        r0 = SUB * (j + 1)
        qj_all.append((q3[:, r0:, :] * jnp.exp2(b3[:, r0:, :] - b3[:, r0 - 1:r0, :])).astype(BF16))
    for c in range(nc):
        kc = kk[c * CHUNK:(c + 1) * CHUNK, :]
        kbd = jnp.where(head_bd, jnp.concatenate([kc] * GLA_HEADS, axis=0), 0.0).astype(BF16)
        lhs = jnp.concatenate([qj[c] for qj in qj_all], axis=0)
        out = lax.dot_general(lhs, kbd, (((1,), (1,)), ((), ())), preferred_element_type=F32)
        off = 0
        for j in range(NSUB - 1):
            r0 = SUB * (j + 1)
            n = CHUNK - r0
            sc3[c, r0:, :] = jnp.where(col_blk == j, out[off:off + n, :], sc3[c, r0:, :])
            off += n

    states = [s_ref[hd] for hd in range(GLA_HEADS)]
    for c in range(nc):
        rows = slice(c * CHUNK, (c + 1) * CHUNK)
        bt = jnp.transpose(b[(c + 1) * CHUNK - 8:(c + 1) * CHUNK, :])
        decay_col = jnp.exp2(bt[:, 7:8])
        for hd in range(GLA_HEADS):
            sb_s[c, hd] = states[hd].astype(BF16)
        for p in range(2):
            lanes = slice(p * PAIR_W, (p + 1) * PAIR_W)
            kdt = jnp.transpose(kd[rows, lanes]).astype(BF16)
            ktv = _dot(kdt, v_s[rows, p * PAIR_V:(p + 1) * PAIR_V])
            dec = decay_col[lanes, :]
            h0, h1 = 2 * p, 2 * p + 1
            states[h0] = states[h0] * dec[0:GLA_DK, :] + ktv[0:GLA_DK, 0:GLA_DV]
            states[h1] = states[h1] * dec[GLA_DK:PAIR_W, :] + ktv[GLA_DK:PAIR_W, GLA_DV:PAIR_V]
    for hd in range(GLA_HEADS):
        s_ref[hd] = states[hd]

    for c in range(nc):
        rows = slice(c * CHUNK, (c + 1) * CHUNK)
        for p in range(2):
            lanes = slice(p * PAIR_W, (p + 1) * PAIR_W)
            h0, h1 = 2 * p, 2 * p + 1
            v0 = v_s[rows, h0 * GLA_DV:(h0 + 1) * GLA_DV]
            v1 = v_s[rows, h1 * GLA_DV:(h1 + 1) * GLA_DV]
            lhs = jnp.concatenate([sc3[c, :, lanes].astype(BF16), qg[rows, lanes]], axis=1)
            w = jnp.concatenate([
                jnp.concatenate([v0, zero_v], axis=1),
                jnp.concatenate([zero_v, v1], axis=1),
                jnp.concatenate([sb_s[c, h0], zero_v], axis=1),
                jnp.concatenate([zero_v, sb_s[c, h1]], axis=1)], axis=0)
            o_s[rows, p * PAIR_V:(p + 1) * PAIR_V] = _dot(lhs, w)

    o = o_s[...]
    gn = gn_ref[...]
    parts = []
    for hd in range(GLA_HEADS):
        oh = o[:, hd * GLA_DV:(hd + 1) * GLA_DV]
        ms = jnp.mean(oh * oh, axis=-1, keepdims=True)
        parts.append(oh * lax.rsqrt(ms + EPS) * gn)
    y_gla = jnp.concatenate(parts, axis=1) * (g_out * jax.nn.sigmoid(g_out))
    y = jnp.concatenate([y_conv, y_gla], axis=1).astype(BF16)
    o_ref[0] = x + gate * _dot(y, wout_ref[...].astype(BF16))
    _cast_side_streams((nxt_in_ref, nxt_out_ref), (nxt_in_bf_ref, nxt_out_bf_ref))


def _mix_call(x, ada3, norm_w, w_in_t, conv_w, w_gk2, b_gk, gla_norm, w_out, nxt_in, nxt_out,
              *, tt):
    bsz, t, d = x.shape
    nc = tt // CHUNK
    n_t = t // tt
    band = _band_masks()
    kern = functools.partial(_mix_kernel, tt=tt)
    const = lambda b, i: (0, 0)
    single = pl.Buffered(1)
    in_spec = _side_spec(nxt_in.shape, bsz * n_t, lambda b, i: b * n_t + i)
    out_spec = _side_spec(nxt_out.shape, bsz * n_t, lambda b, i: b * n_t + i)
    return pl.pallas_call(
        kern,
        grid=(bsz, n_t),
        in_specs=[
            pl.BlockSpec((1, tt, d), lambda b, i: (b, i, 0)),
            pl.BlockSpec((1, N_ADA, d), lambda b, i: (b, 0, 0)),
            pl.BlockSpec((1, d), const),
            pl.BlockSpec(w_in_t.shape, const, pipeline_mode=single),
            pl.BlockSpec(conv_w.shape, const),
            pl.BlockSpec(w_gk2.shape, const),
            pl.BlockSpec(b_gk.shape, const),
            pl.BlockSpec(gla_norm.shape, const),
            pl.BlockSpec(w_out.shape, const, pipeline_mode=single),
            pl.BlockSpec(band.shape, lambda b, i: (0, 0, 0), pipeline_mode=single),
            in_spec,
            out_spec,
        ],
        out_specs=[pl.BlockSpec((1, tt, d), lambda b, i: (b, i, 0)), in_spec, out_spec],
        out_shape=[jax.ShapeDtypeStruct(x.shape, F32),
                   jax.ShapeDtypeStruct(nxt_in.shape, BF16),
                   jax.ShapeDtypeStruct(nxt_out.shape, BF16)],
        scratch_shapes=[
            pltpu.VMEM((GLA_HEADS, GLA_DK, GLA_DV), F32),
            pltpu.VMEM((tt + 8, CONV_WIDTH), F32),
            pltpu.VMEM((tt, KEY_W), F32),
            pltpu.VMEM((nc, PAD + CHUNK, KEY_W), F32),
            pltpu.VMEM((nc, 4 * CHUNK, KEY_W), BF16),
            pltpu.VMEM((nc, PAD + CHUNK, KEY_W), F32),
            pltpu.VMEM((tt, VAL_W), BF16),
            pltpu.VMEM((tt, VAL_W), F32),
            pltpu.VMEM((nc, CHUNK, KEY_W), F32),
            pltpu.VMEM((nc, GLA_HEADS, GLA_DK, GLA_DV), BF16),
        ],
        compiler_params=pltpu.CompilerParams(
            dimension_semantics=("arbitrary", "arbitrary"), vmem_limit_bytes=VMEM_LIMIT),
        name="token_mixer",
    )(x, ada3, norm_w, w_in_t, conv_w, w_gk2, b_gk, gla_norm, w_out, band, nxt_in, nxt_out)


def kernel(x, c, w_ada, b_ada, norm_ffn1, w_ffn1_in, w_ffn1_out, norm_mix, w_mix_in, conv_w,
           w_gk2, b_gk, gla_norm, w_mix_out, norm_ffn2, w_ffn2_in, w_ffn2_out, norm_final):
    bsz, t, d = x.shape
    depth = w_ada.shape[0]
    c_pad = jnp.zeros((8, d), F32).at[:bsz].set(c)
    norm_f = norm_final.reshape(1, d)
    for l in range(depth):
        last = l == depth - 1
        ada, (wm_in_t,) = _ada_call(
            c_pad, w_ada[l], b_ada[l].reshape(1, -1),
            side=(jnp.transpose(w_mix_in[l]),))
        ada3 = ada[:bsz].reshape(bsz, N_ADA, d)
        x, (wm_out,) = _ffn_call(
            x, ada3, norm_ffn1[l].reshape(1, d), w_ffn1_in[l], w_ffn1_out[l], norm_f,
            side=(w_mix_out[l],), ada_base=0, final_norm=False, tm=TOKEN_TILE)
        x, w2_in, w2_out = _mix_call(
            x, ada3, norm_mix[l].reshape(1, d), wm_in_t, conv_w[l], w_gk2[l],
            b_gk[l].reshape(1, KEY_W), gla_norm[l].reshape(1, GLA_DV), wm_out,
            w_ffn2_in[l], w_ffn2_out[l], tt=TOKEN_TILE)
        x, _ = _ffn_call(x, ada3, norm_ffn2[l].reshape(1, d), w2_in, w2_out, norm_f,
                         ada_base=6, final_norm=last, tm=TOKEN_TILE)
    return x
```

```python
import functools
import math

import jax
import jax.numpy as jnp
from jax import lax
from jax.experimental import pallas as pl
from jax.experimental.pallas import tpu as pltpu

F32 = jnp.float32
BF16 = jnp.bfloat16

EPS = 1e-6
CONV_WIDTH = 512
GLA_HEADS = 4
GLA_DK = 64
GLA_DV = 128
KEY_W = GLA_HEADS * GLA_DK
VAL_W = GLA_HEADS * GLA_DV
GATE_RANK = 16
GATE_NORMALIZER = 16.0
CHUNK = 64
SUB = 8
NSUB = CHUNK // SUB
PAD = 8
N_ADA = 9
PAIR_W = 2 * GLA_DK
PAIR_V = 2 * GLA_DV
MIX_MAIN = 3 * CONV_WIDTH + 2 * KEY_W + 2 * VAL_W
VMEM_LIMIT = 58 * 1024 * 1024
LOG2E = math.log2(math.e)
FFN_CHUNK = 256
ADA_K_BLOCK = 128
TOKEN_TILE = 512


def _dot(a, b):
    return jnp.dot(a, b, preferred_element_type=F32)


def _norm_mod(x, gain, shift, scale):
    ms = jnp.mean(x * x, axis=-1, keepdims=True)
    y = x * lax.rsqrt(ms + EPS) * gain
    return y * (1.0 + scale) + shift


def _ada_kernel(c_ref, w_ref, b_ref, o_ref):
    @pl.when(pl.program_id(0) == 0)
    def _():
        o_ref[...] = jnp.broadcast_to(b_ref[...], o_ref.shape)

    c = c_ref[...]
    ca = (c * jax.nn.sigmoid(c)).astype(BF16)
    o_ref[...] += _dot(ca, w_ref[...].astype(BF16))


def _ada_call(c, w_ada, b_ada):
    rows, d = c.shape
    n = w_ada.shape[1]
    tk = ADA_K_BLOCK
    return pl.pallas_call(
        _ada_kernel,
        grid=(d // tk,),
        in_specs=[
            pl.BlockSpec((rows, tk), lambda k: (0, k)),
            pl.BlockSpec((tk, n), lambda k: (k, 0)),
            pl.BlockSpec((1, n), lambda k: (0, 0)),
        ],
        out_specs=pl.BlockSpec((rows, n), lambda k: (0, 0)),
        out_shape=jax.ShapeDtypeStruct((rows, n), F32),
        compiler_params=pltpu.CompilerParams(
            dimension_semantics=("arbitrary",), vmem_limit_bytes=VMEM_LIMIT),
        name="ada_proj",
    )(c, w_ada, b_ada)


def _side_spec(shape, n_rows_grid, n_cols_grid):
    steps = n_rows_grid * n_cols_grid
    for axis, unit in ((0, 16), (1, 128)):
        k = 1
        while k < steps:
            n_blocks = steps // k
            size = shape[axis] // n_blocks
            if shape[axis] % n_blocks == 0 and size % unit == 0:
                block = (size, shape[1]) if axis == 0 else (shape[0], size)
                if axis == 0:
                    return pl.BlockSpec(block, lambda b, i, k=k: ((b * n_cols_grid + i) // k, 0))
                return pl.BlockSpec(block, lambda b, i, k=k: (0, (b * n_cols_grid + i) // k))
            k *= 2
    raise ValueError(f"cannot stream a {shape} matrix over {steps} grid steps")


def _cast_side_streams(src_refs, dst_refs):
    for src, dst in zip(src_refs, dst_refs):
        dst[...] = src[...].astype(BF16)


def _ffn_kernel(*refs, ada_base, chunks, d_ff, final_norm, n_side):
    x_ref, ada_ref, nw_ref, win_ref, wout_ref, nf_ref = refs[:6]
    o_ref = refs[6 + n_side]
    a_ref = refs[7 + 2 * n_side]
    _cast_side_streams(refs[6:6 + n_side], refs[7 + n_side:7 + 2 * n_side])
    x = x_ref[0]
    shift = ada_ref[0, ada_base:ada_base + 1, :]
    scale = ada_ref[0, ada_base + 1:ada_base + 2, :]
    gate = ada_ref[0, ada_base + 2:ada_base + 3, :]
    h = _norm_mod(x, nw_ref[...], shift, scale).astype(BF16)
    off = 0
    for w in chunks:
        g = _dot(h, win_ref[:, off:off + w].astype(BF16))
        u = _dot(h, win_ref[:, d_ff + off:d_ff + off + w].astype(BF16))
        a_ref[:, off:off + w] = (g * jax.nn.sigmoid(g) * u).astype(BF16)
        off += w
    y = _dot(a_ref[...], wout_ref[...].astype(BF16))
    out = x + (0.5 * gate) * y
    if final_norm:
        ms = jnp.mean(out * out, axis=-1, keepdims=True)
        out = out * lax.rsqrt(ms + EPS) * nf_ref[...]
    o_ref[0] = out


def _ffn_call(x, ada3, norm_w, w_in, w_out, norm_f, side=(), *, ada_base, final_norm, tm):
    bsz, t, d = x.shape
    d_ff = w_out.shape[0]
    n_t = t // tm
    chunks = [FFN_CHUNK] * (d_ff // FFN_CHUNK)
    if d_ff % FFN_CHUNK:
        chunks.append(d_ff % FFN_CHUNK)
    kern = functools.partial(_ffn_kernel, ada_base=ada_base, chunks=tuple(chunks),
                             d_ff=d_ff, final_norm=final_norm, n_side=len(side))
    const = lambda b, i: (0, 0)
    side_specs = [_side_spec(w.shape, bsz, n_t) for w in side]
    outs = pl.pallas_call(
        kern,
        grid=(bsz, n_t),
        in_specs=[
            pl.BlockSpec((1, tm, d), lambda b, i: (b, i, 0)),
            pl.BlockSpec((1, N_ADA, d), lambda b, i: (b, 0, 0)),
            pl.BlockSpec((1, d), const),
            pl.BlockSpec((d, 2 * d_ff), const, pipeline_mode=pl.Buffered(1)),
            pl.BlockSpec((d_ff, d), const, pipeline_mode=pl.Buffered(1)),
            pl.BlockSpec((1, d), const),
        ] + side_specs,
        out_specs=[pl.BlockSpec((1, tm, d), lambda b, i: (b, i, 0))] + side_specs,
        out_shape=[jax.ShapeDtypeStruct(x.shape, F32)]
        + [jax.ShapeDtypeStruct(w.shape, BF16) for w in side],
        scratch_shapes=[pltpu.VMEM((tm, d_ff), BF16)],
        compiler_params=pltpu.CompilerParams(
            dimension_semantics=("arbitrary", "arbitrary"), vmem_limit_bytes=VMEM_LIMIT),
        name="ffn_final" if final_norm else "ffn",
    )(x, ada3, norm_w, w_in, w_out, norm_f, *side)
    return outs[0], tuple(outs[1:])


def _band_masks():
    i = jnp.arange(CHUNK)[None, :, None]
    j = (jnp.arange(KEY_W) % GLA_DK)[None, None, :]
    d = jnp.arange(SUB)[:, None, None]
    return ((j == i - d) & (i % SUB >= d)).astype(F32)


def _mix_kernel(x_ref, ada_ref, nw_ref, win_ref, convw_ref, wgk2_ref, bgk_ref,
                gn_ref, wout_ref, band_ref, nxt_in_ref, nxt_out_ref,
                o_ref, nxt_in_bf_ref, nxt_out_bf_ref,
                s_ref, u_ref, q_s, k_s, lgs3, b_s, v_s, o_s, sc3, sb_s, *, tt):
    nc = tt // CHUNK
    t_idx = pl.program_id(1)

    @pl.when(t_idx == 0)
    def _():
        s_ref[...] = jnp.zeros(s_ref.shape, F32)
        u_ref[0:8, :] = jnp.zeros((8, CONV_WIDTH), F32)
        k_s[:, 0:PAD, :] = jnp.zeros((nc, PAD, KEY_W), F32)
        b_s[:, 0:PAD, :] = jnp.zeros((nc, PAD, KEY_W), F32)
        lgs3[:, 3 * CHUNK:4 * CHUNK, :] = jnp.zeros((nc, CHUNK, KEY_W), BF16)

    x = x_ref[0]
    shift = ada_ref[0, 3:4, :]
    scale = ada_ref[0, 4:5, :]
    gate = ada_ref[0, 5:6, :]
    h = _norm_mod(x, nw_ref[...], shift, scale).astype(BF16)

    def proj(lo, width):
        return lax.dot_general(h, win_ref[lo:lo + width, :].astype(BF16),
                               (((1,), (1,)), ((), ())), preferred_element_type=F32)

    tri_r = lax.broadcasted_iota(jnp.int32, (CHUNK, 4 * CHUNK), 0)
    tri_c = lax.broadcasted_iota(jnp.int32, (CHUNK, 4 * CHUNK), 1)
    lower4 = ((tri_c % CHUNK) <= tri_r).astype(BF16)
    hr = lax.broadcasted_iota(jnp.int32, (KEY_W, KEY_W), 0) // GLA_DK
    hc = lax.broadcasted_iota(jnp.int32, (KEY_W, KEY_W), 1) // GLA_DK
    head_bd = hr == hc
    ones_bd = head_bd.astype(BF16)
    col_blk = (lax.broadcasted_iota(jnp.int32, (1, KEY_W), 1) % GLA_DK) // SUB
    zero_v = jnp.zeros((CHUNK, GLA_DV), BF16)

    base = 3 * CONV_WIDTH
    q_s[...] = proj(base, KEY_W) * (GLA_DK ** -0.5)
    k_s[:, PAD:PAD + CHUNK, :] = proj(base + KEY_W, KEY_W).reshape(nc, CHUNK, KEY_W)
    v_s[...] = proj(base + 2 * KEY_W, VAL_W).astype(BF16)
    gk_t = lax.dot_general(win_ref[MIX_MAIN:MIX_MAIN + GATE_RANK, :].astype(BF16), h,
                           (((1,), (1,)), ((), ())), preferred_element_type=F32)
    z = lax.dot_general(gk_t.astype(BF16), wgk2_ref[...].astype(BF16),
                        (((0,), (0,)), ((), ())), preferred_element_type=F32) + bgk_ref[...]
    log_sig = jnp.minimum(z, 0.0) - jnp.log1p(jnp.exp(-jnp.abs(z)))
    lg = log_sig * (LOG2E / GATE_NORMALIZER)
    lg_hi = lg.astype(BF16)
    rem = lg - lg_hi.astype(F32)
    lg_mid = rem.astype(BF16)
    lg_lo = (rem - lg_mid.astype(F32)).astype(BF16)
    lgs3[:, 0:CHUNK, :] = lg_hi.reshape(nc, CHUNK, KEY_W)
    lgs3[:, CHUNK:2 * CHUNK, :] = lg_mid.reshape(nc, CHUNK, KEY_W)
    lgs3[:, 2 * CHUNK:3 * CHUNK, :] = lg_lo.reshape(nc, CHUNK, KEY_W)
    for c in range(nc):
        b_s[c, PAD:PAD + CHUNK, :] = _dot(lower4, lgs3[c])

    cb = proj(0, CONV_WIDTH)
    u = proj(CONV_WIDTH, CONV_WIDTH) * proj(2 * CONV_WIDTH, CONV_WIDTH)
    u_ref[8:8 + tt, :] = u
    cw = convw_ref[...]
    conv = (u_ref[6:6 + tt, :] * cw[0:1, :] + u_ref[7:7 + tt, :] * cw[1:2, :]
            + u * cw[2:3, :])
    y_conv = cb * conv
    u_ref[0:8, :] = u_ref[tt:tt + 8, :]

    q = q_s[...]
    k = k_s[:, PAD:PAD + CHUNK, :].reshape(tt, KEY_W)
    b = b_s[:, PAD:PAD + CHUNK, :].reshape(tt, KEY_W)
    b3 = b.reshape(nc, CHUNK, KEY_W)
    q3 = q.reshape(nc, CHUNK, KEY_W)
    b_last = jnp.broadcast_to(b3[:, CHUNK - 1:CHUNK, :], (nc, CHUNK, KEY_W)).reshape(tt, KEY_W)
    b8 = b.reshape(tt // SUB, SUB, KEY_W)
    b_end = jnp.broadcast_to(b8[:, SUB - 1:SUB, :], (tt // SUB, SUB, KEY_W)).reshape(tt, KEY_W)
    qg = (q * jnp.exp2(b)).astype(BF16)
    kd = k * jnp.exp2(b_last - b)
    kk = k * jnp.exp2(b_end - b)

    s = None
    for d in range(SUB):
        if d == 0:
            e = q * k
        else:
            k_d = k_s[:, PAD - d:PAD - d + CHUNK, :].reshape(tt, KEY_W)
            b_d = b_s[:, PAD - d:PAD - d + CHUNK, :].reshape(tt, KEY_W)
            e = q * k_d * jnp.exp2(b - b_d)
        term = _dot(e.astype(BF16), ones_bd).reshape(nc, CHUNK, KEY_W) * band_ref[d]
        s = term if s is None else s + term
    sc3[...] = s

    g_out = proj(base + 2 * KEY_W + VAL_W, VAL_W)

    qj_all = []
    for j in range(NSUB - 1):
        r0 = SUB * (j + 1)
        qj_all.append((q3[:, r0:, :] * jnp.exp2(b3[:, r0:, :] - b3[:, r0 - 1:r0, :])).astype(BF16))
    for c in range(nc):
        kc = kk[c * CHUNK:(c + 1) * CHUNK, :]
        kbd = jnp.where(head_bd, jnp.concatenate([kc] * GLA_HEADS, axis=0), 0.0).astype(BF16)
        lhs = jnp.concatenate([qj[c] for qj in qj_all], axis=0)
        out = lax.dot_general(lhs, kbd, (((1,), (1,)), ((), ())), preferred_element_type=F32)
        off = 0
        for j in range(NSUB - 1):
            r0 = SUB * (j + 1)
            n = CHUNK - r0
            sc3[c, r0:, :] = jnp.where(col_blk == j, out[off:off + n, :], sc3[c, r0:, :])
            off += n

    states = [s_ref[hd] for hd in range(GLA_HEADS)]
    for c in range(nc):
        rows = slice(c * CHUNK, (c + 1) * CHUNK)
        bt = jnp.transpose(b[(c + 1) * CHUNK - 8:(c + 1) * CHUNK, :])
        decay_col = jnp.exp2(bt[:, 7:8])
        for hd in range(GLA_HEADS):
            sb_s[c, hd] = states[hd].astype(BF16)
        for p in range(2):
            lanes = slice(p * PAIR_W, (p + 1) * PAIR_W)
            kdt = jnp.transpose(kd[rows, lanes]).astype(BF16)
            ktv = _dot(kdt, v_s[rows, p * PAIR_V:(p + 1) * PAIR_V])
            dec = decay_col[lanes, :]
            h0, h1 = 2 * p, 2 * p + 1
            states[h0] = states[h0] * dec[0:GLA_DK, :] + ktv[0:GLA_DK, 0:GLA_DV]
            states[h1] = states[h1] * dec[GLA_DK:PAIR_W, :] + ktv[GLA_DK:PAIR_W, GLA_DV:PAIR_V]
    for hd in range(GLA_HEADS):
        s_ref[hd] = states[hd]

    for c in range(nc):
        rows = slice(c * CHUNK, (c + 1) * CHUNK)
        for p in range(2):
            lanes = slice(p * PAIR_W, (p + 1) * PAIR_W)
            h0, h1 = 2 * p, 2 * p + 1
            v0 = v_s[rows, h0 * GLA_DV:(h0 + 1) * GLA_DV]
            v1 = v_s[rows, h1 * GLA_DV:(h1 + 1) * GLA_DV]
            lhs = jnp.concatenate([sc3[c, :, lanes].astype(BF16), qg[rows, lanes]], axis=1)
            w = jnp.concatenate([
                jnp.concatenate([v0, zero_v], axis=1),
                jnp.concatenate([zero_v, v1], axis=1),
                jnp.concatenate([sb_s[c, h0], zero_v], axis=1),
                jnp.concatenate([zero_v, sb_s[c, h1]], axis=1)], axis=0)
            o_s[rows, p * PAIR_V:(p + 1) * PAIR_V] = _dot(lhs, w)

    o = o_s[...]
    gn = gn_ref[...]
    parts = []
    for hd in range(GLA_HEADS):
        oh = o[:, hd * GLA_DV:(hd + 1) * GLA_DV]
        ms = jnp.mean(oh * oh, axis=-1, keepdims=True)
        parts.append(oh * lax.rsqrt(ms + EPS) * gn)
    y_gla = jnp.concatenate(parts, axis=1) * (g_out * jax.nn.sigmoid(g_out))
    y = jnp.concatenate([y_conv, y_gla], axis=1).astype(BF16)
    o_ref[0] = x + gate * _dot(y, wout_ref[...].astype(BF16))
    _cast_side_streams((nxt_in_ref, nxt_out_ref), (nxt_in_bf_ref, nxt_out_bf_ref))


def _mix_call(x, ada3, norm_w, w_in_t, conv_w, w_gk2, b_gk, gla_norm, w_out, nxt_in, nxt_out,
              *, tt):
    bsz, t, d = x.shape
    nc = tt // CHUNK
    n_t = t // tt
    band = _band_masks()
    kern = functools.partial(_mix_kernel, tt=tt)
    const = lambda b, i: (0, 0)
    single = pl.Buffered(1)
    in_spec = _side_spec(nxt_in.shape, bsz, n_t)
    out_spec = _side_spec(nxt_out.shape, bsz, n_t)
    return pl.pallas_call(
        kern,
        grid=(bsz, n_t),
        in_specs=[
            pl.BlockSpec((1, tt, d), lambda b, i: (b, i, 0)),
            pl.BlockSpec((1, N_ADA, d), lambda b, i: (b, 0, 0)),
            pl.BlockSpec((1, d), const),
            pl.BlockSpec(w_in_t.shape, const, pipeline_mode=single),
            pl.BlockSpec(conv_w.shape, const),
            pl.BlockSpec(w_gk2.shape, const),
            pl.BlockSpec(b_gk.shape, const),
            pl.BlockSpec(gla_norm.shape, const),
            pl.BlockSpec(w_out.shape, const, pipeline_mode=single),
            pl.BlockSpec(band.shape, lambda b, i: (0, 0, 0), pipeline_mode=single),
            in_spec,
            out_spec,
        ],
        out_specs=[pl.BlockSpec((1, tt, d), lambda b, i: (b, i, 0)), in_spec, out_spec],
        out_shape=[jax.ShapeDtypeStruct(x.shape, F32),
                   jax.ShapeDtypeStruct(nxt_in.shape, BF16),
                   jax.ShapeDtypeStruct(nxt_out.shape, BF16)],
        scratch_shapes=[
            pltpu.VMEM((GLA_HEADS, GLA_DK, GLA_DV), F32),
            pltpu.VMEM((tt + 8, CONV_WIDTH), F32),
            pltpu.VMEM((tt, KEY_W), F32),
            pltpu.VMEM((nc, PAD + CHUNK, KEY_W), F32),
            pltpu.VMEM((nc, 4 * CHUNK, KEY_W), BF16),
            pltpu.VMEM((nc, PAD + CHUNK, KEY_W), F32),
            pltpu.VMEM((tt, VAL_W), BF16),
            pltpu.VMEM((tt, VAL_W), F32),
            pltpu.VMEM((nc, CHUNK, KEY_W), F32),
            pltpu.VMEM((nc, GLA_HEADS, GLA_DK, GLA_DV), BF16),
        ],
        compiler_params=pltpu.CompilerParams(
            dimension_semantics=("arbitrary", "arbitrary"), vmem_limit_bytes=VMEM_LIMIT),
        name="token_mixer",
    )(x, ada3, norm_w, w_in_t, conv_w, w_gk2, b_gk, gla_norm, w_out, band, nxt_in, nxt_out)


def kernel(x, c, w_ada, b_ada, norm_ffn1, w_ffn1_in, w_ffn1_out, norm_mix, w_mix_in, conv_w,
           w_gk2, b_gk, gla_norm, w_mix_out, norm_ffn2, w_ffn2_in, w_ffn2_out, norm_final):
    bsz, t, d = x.shape
    depth = w_ada.shape[0]
    norm_f = norm_final.reshape(1, d)
    for l in range(depth):
        last = l == depth - 1
        ada3 = _ada_call(c, w_ada[l], b_ada[l].reshape(1, -1)).reshape(bsz, N_ADA, d)
        x, (wm_in_t, wm_out) = _ffn_call(
            x, ada3, norm_ffn1[l].reshape(1, d), w_ffn1_in[l], w_ffn1_out[l], norm_f,
            side=(jnp.transpose(w_mix_in[l]), w_mix_out[l]),
            ada_base=0, final_norm=False, tm=TOKEN_TILE)
        x, w2_in, w2_out = _mix_call(
            x, ada3, norm_mix[l].reshape(1, d), wm_in_t, conv_w[l], w_gk2[l],
            b_gk[l].reshape(1, KEY_W), gla_norm[l].reshape(1, GLA_DV), wm_out,
            w_ffn2_in[l], w_ffn2_out[l], tt=2 * TOKEN_TILE)
        x, _ = _ffn_call(x, ada3, norm_ffn2[l].reshape(1, d), w2_in, w2_out, norm_f,
                         ada_base=6, final_norm=last, tm=2 * TOKEN_TILE)
    return x
```

```python
import functools
import math

import jax
import jax.numpy as jnp
from jax import lax
from jax.experimental import pallas as pl
from jax.experimental.pallas import tpu as pltpu

F32 = jnp.float32
BF16 = jnp.bfloat16

EPS = 1e-6
CONV_WIDTH = 512
GLA_HEADS = 4
GLA_DK = 64
GLA_DV = 128
KEY_W = GLA_HEADS * GLA_DK
VAL_W = GLA_HEADS * GLA_DV
GATE_RANK = 16
GATE_NORMALIZER = 16.0
CHUNK = 64
SUB = 8
NSUB = CHUNK // SUB
PAD = 8
N_ADA = 9
PAIR_W = 2 * GLA_DK
PAIR_V = 2 * GLA_DV
MIX_MAIN = 3 * CONV_WIDTH + 2 * KEY_W + 2 * VAL_W
VMEM_LIMIT = 58 * 1024 * 1024
LOG2E = math.log2(math.e)
FFN_CHUNK = 256
ADA_K_BLOCK = 128
TOKEN_TILE = 512


def _dot(a, b):
    return jnp.dot(a, b, preferred_element_type=F32)


def _norm_mod(x, gain, shift, scale):
    ms = jnp.mean(x * x, axis=-1, keepdims=True)
    y = x * lax.rsqrt(ms + EPS) * gain
    return y * (1.0 + scale) + shift


def _ada_kernel(c_ref, w_ref, b_ref, o_ref):
    @pl.when(pl.program_id(0) == 0)
    def _():
        o_ref[...] = jnp.broadcast_to(b_ref[...], o_ref.shape)

    c = c_ref[...]
    ca = (c * jax.nn.sigmoid(c)).astype(BF16)
    o_ref[...] += _dot(ca, w_ref[...].astype(BF16))


def _ada_call(c, w_ada, b_ada):
    rows, d = c.shape
    n = w_ada.shape[1]
    tk = ADA_K_BLOCK
    return pl.pallas_call(
        _ada_kernel,
        grid=(d // tk,),
        in_specs=[
            pl.BlockSpec((rows, tk), lambda k: (0, k)),
            pl.BlockSpec((tk, n), lambda k: (k, 0)),
            pl.BlockSpec((1, n), lambda k: (0, 0)),
        ],
        out_specs=pl.BlockSpec((rows, n), lambda k: (0, 0)),
        out_shape=jax.ShapeDtypeStruct((rows, n), F32),
        compiler_params=pltpu.CompilerParams(
            dimension_semantics=("arbitrary",), vmem_limit_bytes=VMEM_LIMIT),
        name="ada_proj",
    )(c, w_ada, b_ada)


def _side_spec(shape, n_rows_grid, n_cols_grid):
    steps = n_rows_grid * n_cols_grid
    for axis, unit in ((0, 16), (1, 128)):
        k = 1
        while k < steps:
            n_blocks = steps // k
            size = shape[axis] // n_blocks
            if shape[axis] % n_blocks == 0 and size % unit == 0:
                block = (size, shape[1]) if axis == 0 else (shape[0], size)
                if axis == 0:
                    return pl.BlockSpec(block, lambda b, i, k=k: ((b * n_cols_grid + i) // k, 0))
                return pl.BlockSpec(block, lambda b, i, k=k: (0, (b * n_cols_grid + i) // k))
            k *= 2
    raise ValueError(f"cannot stream a {shape} matrix over {steps} grid steps")


def _cast_side_streams(src_refs, dst_refs):
    for src, dst in zip(src_refs, dst_refs):
        dst[...] = src[...].astype(BF16)


def _ffn_tile(x_ref, ada_ref, side_in, o_ref, side_out, nw_ref, win_ref, wout_ref, nf_ref, a_ref,
              *, ada_base, chunks, d_ff, final_norm):
    _cast_side_streams(side_in, side_out)
    x = x_ref[0]
    shift = ada_ref[0, ada_base:ada_base + 1, :]
    scale = ada_ref[0, ada_base + 1:ada_base + 2, :]
    gate = ada_ref[0, ada_base + 2:ada_base + 3, :]
    h = _norm_mod(x, nw_ref[...], shift, scale).astype(BF16)
    off = 0
    for w in chunks:
        g = _dot(h, win_ref[:, off:off + w].astype(BF16))
        u = _dot(h, win_ref[:, d_ff + off:d_ff + off + w].astype(BF16))
        a_ref[:, off:off + w] = (g * jax.nn.sigmoid(g) * u).astype(BF16)
        off += w
    y = _dot(a_ref[...], wout_ref[...].astype(BF16))
    out = x + (0.5 * gate) * y
    if final_norm:
        ms = jnp.mean(out * out, axis=-1, keepdims=True)
        out = out * lax.rsqrt(ms + EPS) * nf_ref[...]
    o_ref[0] = out


def _ffn_call(x, ada3, norm_w, w_in, w_out, norm_f, side=(), *, ada_base, final_norm, tm):
    bsz, t, d = x.shape
    d_ff = w_out.shape[0]
    n_t = t // tm
    n_side = len(side)
    chunks = [FFN_CHUNK] * (d_ff // FFN_CHUNK)
    if d_ff % FFN_CHUNK:
        chunks.append(d_ff % FFN_CHUNK)
    side_specs = [_side_spec(w.shape, bsz, n_t) for w in side]
    tile_spec = pl.BlockSpec((1, tm, d), lambda b, i: (b, i, 0))
    ada_spec = pl.BlockSpec((1, N_ADA, d), lambda b, i: (b, 0, 0))

    def outer(*refs):
        x_hbm, ada_hbm = refs[:2]
        side_hbm = refs[2:2 + n_side]
        nw_ref, win_ref, wout_ref, nf_ref = refs[2 + n_side:6 + n_side]
        o_hbm = refs[6 + n_side]
        side_out_hbm = refs[7 + n_side:7 + 2 * n_side]
        a_ref = refs[7 + 2 * n_side]

        def tile(*trefs):
            x_ref, ada_ref = trefs[:2]
            s_in = trefs[2:2 + n_side]
            o_ref = trefs[2 + n_side]
            s_out = trefs[3 + n_side:3 + 2 * n_side]
            _ffn_tile(x_ref, ada_ref, s_in, o_ref, s_out, nw_ref, win_ref, wout_ref, nf_ref, a_ref,
                      ada_base=ada_base, chunks=tuple(chunks), d_ff=d_ff, final_norm=final_norm)

        pltpu.emit_pipeline(
            tile, grid=(bsz, n_t),
            in_specs=[tile_spec, ada_spec] + side_specs,
            out_specs=[tile_spec] + side_specs,
        )(x_hbm, ada_hbm, *side_hbm, o_hbm, *side_out_hbm)

    hbm = pl.BlockSpec(memory_space=pl.ANY)
    vmem = pl.BlockSpec(memory_space=pltpu.VMEM)
    outs = pl.pallas_call(
        outer,
        in_specs=[hbm, hbm] + [hbm] * n_side + [vmem, vmem, vmem, vmem],
        out_specs=[hbm] + [hbm] * n_side,
        out_shape=[jax.ShapeDtypeStruct(x.shape, F32)]
        + [jax.ShapeDtypeStruct(w.shape, BF16) for w in side],
        scratch_shapes=[pltpu.VMEM((tm, d_ff), BF16)],
        compiler_params=pltpu.CompilerParams(vmem_limit_bytes=VMEM_LIMIT),
        name="ffn_final" if final_norm else "ffn",
    )(x, ada3, *side, norm_w, w_in, w_out, norm_f)
    return outs[0], tuple(outs[1:])


def _band_masks():
    i = jnp.arange(CHUNK)[None, :, None]
    j = (jnp.arange(KEY_W) % GLA_DK)[None, None, :]
    d = jnp.arange(SUB)[:, None, None]
    return ((j == i - d) & (i % SUB >= d)).astype(F32)


def _mix_kernel(x_ref, ada_ref, nw_ref, win_ref, convw_ref, wgk2_ref, bgk_ref,
                gn_ref, wout_ref, band_ref, nxt_in_ref, nxt_out_ref,
                o_ref, nxt_in_bf_ref, nxt_out_bf_ref,
                s_ref, u_ref, q_s, k_s, lgs3, b_s, v_s, o_s, sc3, sb_s, *, tt):
    nc = tt // CHUNK
    t_idx = pl.program_id(1)

    @pl.when(t_idx == 0)
    def _():
        s_ref[...] = jnp.zeros(s_ref.shape, F32)
        u_ref[0:8, :] = jnp.zeros((8, CONV_WIDTH), F32)
        k_s[:, 0:PAD, :] = jnp.zeros((nc, PAD, KEY_W), F32)
        b_s[:, 0:PAD, :] = jnp.zeros((nc, PAD, KEY_W), F32)
        lgs3[:, 3 * CHUNK:4 * CHUNK, :] = jnp.zeros((nc, CHUNK, KEY_W), BF16)

    x = x_ref[0]
    shift = ada_ref[0, 3:4, :]
    scale = ada_ref[0, 4:5, :]
    gate = ada_ref[0, 5:6, :]
    h = _norm_mod(x, nw_ref[...], shift, scale).astype(BF16)

    def proj(lo, width):
        return lax.dot_general(h, win_ref[lo:lo + width, :].astype(BF16),
                               (((1,), (1,)), ((), ())), preferred_element_type=F32)

    tri_r = lax.broadcasted_iota(jnp.int32, (CHUNK, 4 * CHUNK), 0)
    tri_c = lax.broadcasted_iota(jnp.int32, (CHUNK, 4 * CHUNK), 1)
    lower4 = ((tri_c % CHUNK) <= tri_r).astype(BF16)
    hr = lax.broadcasted_iota(jnp.int32, (KEY_W, KEY_W), 0) // GLA_DK
    hc = lax.broadcasted_iota(jnp.int32, (KEY_W, KEY_W), 1) // GLA_DK
    head_bd = hr == hc
    ones_bd = head_bd.astype(BF16)
    col_blk = (lax.broadcasted_iota(jnp.int32, (1, KEY_W), 1) % GLA_DK) // SUB
    zero_v = jnp.zeros((CHUNK, GLA_DV), BF16)

    base = 3 * CONV_WIDTH
    q_s[...] = proj(base, KEY_W) * (GLA_DK ** -0.5)
    k_s[:, PAD:PAD + CHUNK, :] = proj(base + KEY_W, KEY_W).reshape(nc, CHUNK, KEY_W)
    v_s[...] = proj(base + 2 * KEY_W, VAL_W).astype(BF16)
    gk_t = lax.dot_general(win_ref[MIX_MAIN:MIX_MAIN + GATE_RANK, :].astype(BF16), h,
                           (((1,), (1,)), ((), ())), preferred_element_type=F32)
    z = lax.dot_general(gk_t.astype(BF16), wgk2_ref[...].astype(BF16),
                        (((0,), (0,)), ((), ())), preferred_element_type=F32) + bgk_ref[...]
    log_sig = jnp.minimum(z, 0.0) - jnp.log1p(jnp.exp(-jnp.abs(z)))
    lg = log_sig * (LOG2E / GATE_NORMALIZER)
    lg_hi = lg.astype(BF16)
    rem = lg - lg_hi.astype(F32)
    lg_mid = rem.astype(BF16)
    lg_lo = (rem - lg_mid.astype(F32)).astype(BF16)
    lgs3[:, 0:CHUNK, :] = lg_hi.reshape(nc, CHUNK, KEY_W)
    lgs3[:, CHUNK:2 * CHUNK, :] = lg_mid.reshape(nc, CHUNK, KEY_W)
    lgs3[:, 2 * CHUNK:3 * CHUNK, :] = lg_lo.reshape(nc, CHUNK, KEY_W)
    for c in range(nc):
        b_s[c, PAD:PAD + CHUNK, :] = _dot(lower4, lgs3[c])

    cb = proj(0, CONV_WIDTH)
    u = proj(CONV_WIDTH, CONV_WIDTH) * proj(2 * CONV_WIDTH, CONV_WIDTH)
    u_ref[8:8 + tt, :] = u
    cw = convw_ref[...]
    conv = (u_ref[6:6 + tt, :] * cw[0:1, :] + u_ref[7:7 + tt, :] * cw[1:2, :]
            + u * cw[2:3, :])
    y_conv = cb * conv
    u_ref[0:8, :] = u_ref[tt:tt + 8, :]

    q = q_s[...]
    k = k_s[:, PAD:PAD + CHUNK, :].reshape(tt, KEY_W)
    b = b_s[:, PAD:PAD + CHUNK, :].reshape(tt, KEY_W)
    b3 = b.reshape(nc, CHUNK, KEY_W)
    q3 = q.reshape(nc, CHUNK, KEY_W)
    b_last = jnp.broadcast_to(b3[:, CHUNK - 1:CHUNK, :], (nc, CHUNK, KEY_W)).reshape(tt, KEY_W)
    b8 = b.reshape(tt // SUB, SUB, KEY_W)
    b_end = jnp.broadcast_to(b8[:, SUB - 1:SUB, :], (tt // SUB, SUB, KEY_W)).reshape(tt, KEY_W)
    qg = (q * jnp.exp2(b)).astype(BF16)
    kd = k * jnp.exp2(b_last - b)
    kk = k * jnp.exp2(b_end - b)

    s = None
    for d in range(SUB):
        if d == 0:
            e = q * k
        else:
            k_d = k_s[:, PAD - d:PAD - d + CHUNK, :].reshape(tt, KEY_W)
            b_d = b_s[:, PAD - d:PAD - d + CHUNK, :].reshape(tt, KEY_W)
            e = q * k_d * jnp.exp2(b - b_d)
        term = _dot(e.astype(BF16), ones_bd).reshape(nc, CHUNK, KEY_W) * band_ref[d]
        s = term if s is None else s + term
    sc3[...] = s

    g_out = proj(base + 2 * KEY_W + VAL_W, VAL_W)

    qj_all = []
    for j in range(NSUB - 1):
        r0 = SUB * (j + 1)
        qj_all.append((q3[:, r0:, :] * jnp.exp2(b3[:, r0:, :] - b3[:, r0 - 1:r0, :])).astype(BF16))
    for c in range(nc):
        kc = kk[c * CHUNK:(c + 1) * CHUNK, :]
        kbd = jnp.where(head_bd, jnp.concatenate([kc] * GLA_HEADS, axis=0), 0.0).astype(BF16)
        lhs = jnp.concatenate([qj[c] for qj in qj_all], axis=0)
        out = lax.dot_general(lhs, kbd, (((1,), (1,)), ((), ())), preferred_element_type=F32)
        off = 0
        for j in range(NSUB - 1):
            r0 = SUB * (j + 1)
            n = CHUNK - r0
            sc3[c, r0:, :] = jnp.where(col_blk == j, out[off:off + n, :], sc3[c, r0:, :])
            off += n

    states = [s_ref[hd] for hd in range(GLA_HEADS)]
    for c in range(nc):
        rows = slice(c * CHUNK, (c + 1) * CHUNK)
        bt = jnp.transpose(b[(c + 1) * CHUNK - 8:(c + 1) * CHUNK, :])
        decay_col = jnp.exp2(bt[:, 7:8])
        for hd in range(GLA_HEADS):
            sb_s[c, hd] = states[hd].astype(BF16)
        for p in range(2):
            lanes = slice(p * PAIR_W, (p + 1) * PAIR_W)
            kdt = jnp.transpose(kd[rows, lanes]).astype(BF16)
            ktv = _dot(kdt, v_s[rows, p * PAIR_V:(p + 1) * PAIR_V])
            dec = decay_col[lanes, :]
            h0, h1 = 2 * p, 2 * p + 1
            states[h0] = states[h0] * dec[0:GLA_DK, :] + ktv[0:GLA_DK, 0:GLA_DV]
            states[h1] = states[h1] * dec[GLA_DK:PAIR_W, :] + ktv[GLA_DK:PAIR_W, GLA_DV:PAIR_V]
    for hd in range(GLA_HEADS):
        s_ref[hd] = states[hd]

    for c in range(nc):
        rows = slice(c * CHUNK, (c + 1) * CHUNK)
        for p in range(2):
            lanes = slice(p * PAIR_W, (p + 1) * PAIR_W)
            h0, h1 = 2 * p, 2 * p + 1
            v0 = v_s[rows, h0 * GLA_DV:(h0 + 1) * GLA_DV]
            v1 = v_s[rows, h1 * GLA_DV:(h1 + 1) * GLA_DV]
            lhs = jnp.concatenate([sc3[c, :, lanes].astype(BF16), qg[rows, lanes]], axis=1)
            w = jnp.concatenate([
                jnp.concatenate([v0, zero_v], axis=1),
                jnp.concatenate([zero_v, v1], axis=1),
                jnp.concatenate([sb_s[c, h0], zero_v], axis=1),
                jnp.concatenate([zero_v, sb_s[c, h1]], axis=1)], axis=0)
            o_s[rows, p * PAIR_V:(p + 1) * PAIR_V] = _dot(lhs, w)

    o = o_s[...]
    gn = gn_ref[...]
    parts = []
    for hd in range(GLA_HEADS):
        oh = o[:, hd * GLA_DV:(hd + 1) * GLA_DV]
        ms = jnp.mean(oh * oh, axis=-1, keepdims=True)
        parts.append(oh * lax.rsqrt(ms + EPS) * gn)
    y_gla = jnp.concatenate(parts, axis=1) * (g_out * jax.nn.sigmoid(g_out))
    y = jnp.concatenate([y_conv, y_gla], axis=1).astype(BF16)
    o_ref[0] = x + gate * _dot(y, wout_ref[...].astype(BF16))
    _cast_side_streams((nxt_in_ref, nxt_out_ref), (nxt_in_bf_ref, nxt_out_bf_ref))


def _mix_call(x, ada3, norm_w, w_in_t, conv_w, w_gk2, b_gk, gla_norm, w_out, nxt_in, nxt_out,
              *, tt):
    bsz, t, d = x.shape
    nc = tt // CHUNK
    n_t = t // tt
    band = _band_masks()
    kern = functools.partial(_mix_kernel, tt=tt)
    const = lambda b, i: (0, 0)
    single = pl.Buffered(1)
    in_spec = _side_spec(nxt_in.shape, bsz, n_t)
    out_spec = _side_spec(nxt_out.shape, bsz, n_t)
    return pl.pallas_call(
        kern,
        grid=(bsz, n_t),
        in_specs=[
            pl.BlockSpec((1, tt, d), lambda b, i: (b, i, 0)),
            pl.BlockSpec((1, N_ADA, d), lambda b, i: (b, 0, 0)),
            pl.BlockSpec((1, d), const),
            pl.BlockSpec(w_in_t.shape, const, pipeline_mode=single),
            pl.BlockSpec(conv_w.shape, const),
            pl.BlockSpec(w_gk2.shape, const),
            pl.BlockSpec(b_gk.shape, const),
            pl.BlockSpec(gla_norm.shape, const),
            pl.BlockSpec(w_out.shape, const, pipeline_mode=single),
            pl.BlockSpec(band.shape, lambda b, i: (0, 0, 0), pipeline_mode=single),
            in_spec,
            out_spec,
        ],
        out_specs=[pl.BlockSpec((1, tt, d), lambda b, i: (b, i, 0)), in_spec, out_spec],
        out_shape=[jax.ShapeDtypeStruct(x.shape, F32),
                   jax.ShapeDtypeStruct(nxt_in.shape, BF16),
                   jax.ShapeDtypeStruct(nxt_out.shape, BF16)],
        scratch_shapes=[
            pltpu.VMEM((GLA_HEADS, GLA_DK, GLA_DV), F32),
            pltpu.VMEM((tt + 8, CONV_WIDTH), F32),
            pltpu.VMEM((tt, KEY_W), F32),
            pltpu.VMEM((nc, PAD + CHUNK, KEY_W), F32),
            pltpu.VMEM((nc, 4 * CHUNK, KEY_W), BF16),
            pltpu.VMEM((nc, PAD + CHUNK, KEY_W), F32),
            pltpu.VMEM((tt, VAL_W), BF16),
            pltpu.VMEM((tt, VAL_W), F32),
            pltpu.VMEM((nc, CHUNK, KEY_W), F32),
            pltpu.VMEM((nc, GLA_HEADS, GLA_DK, GLA_DV), BF16),
        ],
        compiler_params=pltpu.CompilerParams(
            dimension_semantics=("arbitrary", "arbitrary"), vmem_limit_bytes=VMEM_LIMIT),
        name="token_mixer",
    )(x, ada3, norm_w, w_in_t, conv_w, w_gk2, b_gk, gla_norm, w_out, band, nxt_in, nxt_out)


def kernel(x, c, w_ada, b_ada, norm_ffn1, w_ffn1_in, w_ffn1_out, norm_mix, w_mix_in, conv_w,
           w_gk2, b_gk, gla_norm, w_mix_out, norm_ffn2, w_ffn2_in, w_ffn2_out, norm_final):
    bsz, t, d = x.shape
    depth = w_ada.shape[0]
    norm_f = norm_final.reshape(1, d)
    for l in range(depth):
        last = l == depth - 1
        ada3 = _ada_call(c, w_ada[l], b_ada[l].reshape(1, -1)).reshape(bsz, N_ADA, d)
        x, (wm_in_t, wm_out) = _ffn_call(
            x, ada3, norm_ffn1[l].reshape(1, d), w_ffn1_in[l], w_ffn1_out[l], norm_f,
            side=(jnp.transpose(w_mix_in[l]), w_mix_out[l]),
            ada_base=0, final_norm=False, tm=TOKEN_TILE)
        x, w2_in, w2_out = _mix_call(
            x, ada3, norm_mix[l].reshape(1, d), wm_in_t, conv_w[l], w_gk2[l],
            b_gk[l].reshape(1, KEY_W), gla_norm[l].reshape(1, GLA_DV), wm_out,
            w_ffn2_in[l], w_ffn2_out[l], tt=2 * TOKEN_TILE)
        x, _ = _ffn_call(x, ada3, norm_ffn2[l].reshape(1, d), w2_in, w2_out, norm_f,
                         ada_base=6, final_norm=last, tm=2 * TOKEN_TILE)
    return x
```

```python
import functools
import math

import jax
import jax.numpy as jnp
from jax import lax
from jax.experimental import pallas as pl
from jax.experimental.pallas import tpu as pltpu

F32 = jnp.float32
BF16 = jnp.bfloat16

EPS = 1e-6
CONV_WIDTH = 512
GLA_HEADS = 4
GLA_DK = 64
GLA_DV = 128
KEY_W = GLA_HEADS * GLA_DK
VAL_W = GLA_HEADS * GLA_DV
GATE_RANK = 16
GATE_NORMALIZER = 16.0
CHUNK = 64
SUB = 8
NSUB = CHUNK // SUB
PAD = 8
N_ADA = 9
PAIR_W = 2 * GLA_DK
PAIR_V = 2 * GLA_DV
MIX_MAIN = 3 * CONV_WIDTH + 2 * KEY_W + 2 * VAL_W
VMEM_LIMIT = 58 * 1024 * 1024
LOG2E = math.log2(math.e)
FFN_CHUNK = 256
ADA_K_BLOCK = 128
TOKEN_TILE = 512


def _dot(a, b):
    return jnp.dot(a, b, preferred_element_type=F32)


def _norm_mod(x, gain, shift, scale):
    ms = jnp.mean(x * x, axis=-1, keepdims=True)
    y = x * lax.rsqrt(ms + EPS) * gain
    return y * (1.0 + scale) + shift


def _ada_kernel(c_ref, w_ref, b_ref, o_ref):
    @pl.when(pl.program_id(0) == 0)
    def _():
        o_ref[...] = jnp.broadcast_to(b_ref[...], o_ref.shape)

    c = c_ref[...]
    ca = (c * jax.nn.sigmoid(c)).astype(BF16)
    o_ref[...] += _dot(ca, w_ref[...].astype(BF16))


def _ada_call(c, w_ada, b_ada):
    rows, d = c.shape
    n = w_ada.shape[1]
    tk = ADA_K_BLOCK
    return pl.pallas_call(
        _ada_kernel,
        grid=(d // tk,),
        in_specs=[
            pl.BlockSpec((rows, tk), lambda k: (0, k)),
            pl.BlockSpec((tk, n), lambda k: (k, 0)),
            pl.BlockSpec((1, n), lambda k: (0, 0)),
        ],
        out_specs=pl.BlockSpec((rows, n), lambda k: (0, 0)),
        out_shape=jax.ShapeDtypeStruct((rows, n), F32),
        compiler_params=pltpu.CompilerParams(
            dimension_semantics=("arbitrary",), vmem_limit_bytes=VMEM_LIMIT),
        name="ada_proj",
    )(c, w_ada, b_ada)


def _side_spec(shape, n_rows_grid, n_cols_grid):
    steps = n_rows_grid * n_cols_grid
    for axis, unit in ((0, 16), (1, 128)):
        k = 1
        while k < steps:
            n_blocks = steps // k
            size = shape[axis] // n_blocks
            if shape[axis] % n_blocks == 0 and size % unit == 0:
                block = (size, shape[1]) if axis == 0 else (shape[0], size)
                if axis == 0:
                    return pl.BlockSpec(block, lambda b, i, k=k: ((b * n_cols_grid + i) // k, 0))
                return pl.BlockSpec(block, lambda b, i, k=k: (0, (b * n_cols_grid + i) // k))
            k *= 2
    raise ValueError(f"cannot stream a {shape} matrix over {steps} grid steps")


def _cast_side_streams(src_refs, dst_refs):
    for src, dst in zip(src_refs, dst_refs):
        dst[...] = src[...].astype(BF16)


def _ffn_tile(x_ref, ada_ref, side_in, o_ref, side_out, nw_ref, win_ref, wout_ref, nf_ref, a_ref,
              *, ada_base, chunks, d_ff, final_norm):
    _cast_side_streams(side_in, side_out)
    x = x_ref[0]
    shift = ada_ref[0, ada_base:ada_base + 1, :]
    scale = ada_ref[0, ada_base + 1:ada_base + 2, :]
    gate = ada_ref[0, ada_base + 2:ada_base + 3, :]
    h = _norm_mod(x, nw_ref[...], shift, scale).astype(BF16)
    off = 0
    for w in chunks:
        g = _dot(h, win_ref[:, off:off + w].astype(BF16))
        u = _dot(h, win_ref[:, d_ff + off:d_ff + off + w].astype(BF16))
        a_ref[:, off:off + w] = (g * jax.nn.sigmoid(g) * u).astype(BF16)
        off += w
    y = _dot(a_ref[...], wout_ref[...].astype(BF16))
    out = x + (0.5 * gate) * y
    if final_norm:
        ms = jnp.mean(out * out, axis=-1, keepdims=True)
        out = out * lax.rsqrt(ms + EPS) * nf_ref[...]
    o_ref[0] = out


def _ffn_call(x, ada3, norm_w, w_in, w_out, norm_f, side=(), *, ada_base, final_norm, tm):
    bsz, t, d = x.shape
    d_ff = w_out.shape[0]
    n_t = t // tm
    n_side = len(side)
    chunks = [FFN_CHUNK] * (d_ff // FFN_CHUNK)
    if d_ff % FFN_CHUNK:
        chunks.append(d_ff % FFN_CHUNK)
    side_specs = [_side_spec(w.shape, bsz, n_t) for w in side]
    tile_spec = pl.BlockSpec((1, tm, d), lambda b, i: (b, i, 0))
    ada_spec = pl.BlockSpec((1, N_ADA, d), lambda b, i: (b, 0, 0))

    def outer(*refs):
        x_hbm, ada_hbm = refs[:2]
        side_hbm = refs[2:2 + n_side]
        nw_ref, win_ref, wout_ref, nf_ref = refs[2 + n_side:6 + n_side]
        o_hbm = refs[6 + n_side]
        side_out_hbm = refs[7 + n_side:7 + 2 * n_side]
        a_ref = refs[7 + 2 * n_side]

        def tile(*trefs):
            x_ref, ada_ref = trefs[:2]
            s_in = trefs[2:2 + n_side]
            o_ref = trefs[2 + n_side]
            s_out = trefs[3 + n_side:3 + 2 * n_side]
            _ffn_tile(x_ref, ada_ref, s_in, o_ref, s_out, nw_ref, win_ref, wout_ref, nf_ref, a_ref,
                      ada_base=ada_base, chunks=tuple(chunks), d_ff=d_ff, final_norm=final_norm)

        pltpu.emit_pipeline(
            tile, grid=(bsz, n_t),
            in_specs=[tile_spec, ada_spec] + side_specs,
            out_specs=[tile_spec] + side_specs,
        )(x_hbm, ada_hbm, *side_hbm, o_hbm, *side_out_hbm)

    hbm = pl.BlockSpec(memory_space=pl.ANY)
    vmem = pl.BlockSpec(memory_space=pltpu.VMEM)
    outs = pl.pallas_call(
        outer,
        in_specs=[hbm, hbm] + [hbm] * n_side + [vmem, vmem, vmem, vmem],
        out_specs=[hbm] + [hbm] * n_side,
        out_shape=[jax.ShapeDtypeStruct(x.shape, F32)]
        + [jax.ShapeDtypeStruct(w.shape, BF16) for w in side],
        scratch_shapes=[pltpu.VMEM((tm, d_ff), BF16)],
        compiler_params=pltpu.CompilerParams(vmem_limit_bytes=VMEM_LIMIT),
        name="ffn_final" if final_norm else "ffn",
    )(x, ada3, *side, norm_w, w_in, w_out, norm_f)
    return outs[0], tuple(outs[1:])


def _band_masks():
    i = jnp.arange(CHUNK)[None, :, None]
    j = (jnp.arange(KEY_W) % GLA_DK)[None, None, :]
    d = jnp.arange(SUB)[:, None, None]
    return ((j == i - d) & (i % SUB >= d)).astype(F32)


def _mix_kernel(x_ref, ada_ref, nw_ref, win_ref, convw_ref, wgk2_ref, bgk_ref,
                gn_ref, wout_ref, band_ref, nxt_in_ref, nxt_out_ref,
                o_ref, nxt_in_bf_ref, nxt_out_bf_ref,
                s_ref, u_ref, q_s, k_s, lgs3, b_s, v_s, o_s, sc3, sb_s, *, tt):
    nc = tt // CHUNK
    t_idx = pl.program_id(1)

    @pl.when(t_idx == 0)
    def _():
        s_ref[...] = jnp.zeros(s_ref.shape, F32)
        u_ref[0:8, :] = jnp.zeros((8, CONV_WIDTH), F32)
        k_s[:, 0:PAD, :] = jnp.zeros((nc, PAD, KEY_W), F32)
        b_s[:, 0:PAD, :] = jnp.zeros((nc, PAD, KEY_W), F32)
        lgs3[:, 3 * CHUNK:4 * CHUNK, :] = jnp.zeros((nc, CHUNK, KEY_W), BF16)

    x = x_ref[0]
    shift = ada_ref[0, 3:4, :]
    scale = ada_ref[0, 4:5, :]
    gate = ada_ref[0, 5:6, :]
    h = _norm_mod(x, nw_ref[...], shift, scale).astype(BF16)

    def proj(lo, width):
        return lax.dot_general(h, win_ref[lo:lo + width, :].astype(BF16),
                               (((1,), (1,)), ((), ())), preferred_element_type=F32)

    tri_r = lax.broadcasted_iota(jnp.int32, (CHUNK, 4 * CHUNK), 0)
    tri_c = lax.broadcasted_iota(jnp.int32, (CHUNK, 4 * CHUNK), 1)
    lower4 = ((tri_c % CHUNK) <= tri_r).astype(BF16)
    hr = lax.broadcasted_iota(jnp.int32, (KEY_W, KEY_W), 0) // GLA_DK
    hc = lax.broadcasted_iota(jnp.int32, (KEY_W, KEY_W), 1) // GLA_DK
    head_bd = hr == hc
    ones_bd = head_bd.astype(BF16)
    col_blk = (lax.broadcasted_iota(jnp.int32, (1, KEY_W), 1) % GLA_DK) // SUB
    zero_v = jnp.zeros((CHUNK, GLA_DV), BF16)

    base = 3 * CONV_WIDTH
    q_s[...] = proj(base, KEY_W) * (GLA_DK ** -0.5)
    k_s[:, PAD:PAD + CHUNK, :] = proj(base + KEY_W, KEY_W).reshape(nc, CHUNK, KEY_W)
    v_s[...] = proj(base + 2 * KEY_W, VAL_W).astype(BF16)
    gk_t = lax.dot_general(win_ref[MIX_MAIN:MIX_MAIN + GATE_RANK, :].astype(BF16), h,
                           (((1,), (1,)), ((), ())), preferred_element_type=F32)
    z = lax.dot_general(gk_t.astype(BF16), wgk2_ref[...].astype(BF16),
                        (((0,), (0,)), ((), ())), preferred_element_type=F32) + bgk_ref[...]
    log_sig = jnp.minimum(z, 0.0) - jnp.log1p(jnp.exp(-jnp.abs(z)))
    lg = log_sig * (LOG2E / GATE_NORMALIZER)
    lg_hi = lg.astype(BF16)
    rem = lg - lg_hi.astype(F32)
    lg_mid = rem.astype(BF16)
    lg_lo = (rem - lg_mid.astype(F32)).astype(BF16)
    lgs3[:, 0:CHUNK, :] = lg_hi.reshape(nc, CHUNK, KEY_W)
    lgs3[:, CHUNK:2 * CHUNK, :] = lg_mid.reshape(nc, CHUNK, KEY_W)
    lgs3[:, 2 * CHUNK:3 * CHUNK, :] = lg_lo.reshape(nc, CHUNK, KEY_W)
    for c in range(nc):
        b_s[c, PAD:PAD + CHUNK, :] = _dot(lower4, lgs3[c])

    cb = proj(0, CONV_WIDTH)
    u = proj(CONV_WIDTH, CONV_WIDTH) * proj(2 * CONV_WIDTH, CONV_WIDTH)
    u_ref[8:8 + tt, :] = u
    cw = convw_ref[...]
    conv = (u_ref[6:6 + tt, :] * cw[0:1, :] + u_ref[7:7 + tt, :] * cw[1:2, :]
            + u * cw[2:3, :])
    y_conv = cb * conv
    u_ref[0:8, :] = u_ref[tt:tt + 8, :]

    q = q_s[...]
    k = k_s[:, PAD:PAD + CHUNK, :].reshape(tt, KEY_W)
    b = b_s[:, PAD:PAD + CHUNK, :].reshape(tt, KEY_W)
    b3 = b.reshape(nc, CHUNK, KEY_W)
    q3 = q.reshape(nc, CHUNK, KEY_W)
    b_last = jnp.broadcast_to(b3[:, CHUNK - 1:CHUNK, :], (nc, CHUNK, KEY_W)).reshape(tt, KEY_W)
    b8 = b.reshape(tt // SUB, SUB, KEY_W)
    b_end = jnp.broadcast_to(b8[:, SUB - 1:SUB, :], (tt // SUB, SUB, KEY_W)).reshape(tt, KEY_W)
    qg = (q * jnp.exp2(b)).astype(BF16)
    kd = k * jnp.exp2(b_last - b)
    kk = k * jnp.exp2(b_end - b)

    s = None
    for d in range(SUB):
        if d == 0:
            e = q * k
        else:
            k_d = k_s[:, PAD - d:PAD - d + CHUNK, :].reshape(tt, KEY_W)
            b_d = b_s[:, PAD - d:PAD - d + CHUNK, :].reshape(tt, KEY_W)
            e = q * k_d * jnp.exp2(b - b_d)
        term = _dot(e.astype(BF16), ones_bd).reshape(nc, CHUNK, KEY_W) * band_ref[d]
        s = term if s is None else s + term
    sc3[...] = s

    g_out = proj(base + 2 * KEY_W + VAL_W, VAL_W)

    qj_all = []
    for j in range(NSUB - 1):
        r0 = SUB * (j + 1)
        qj_all.append((q3[:, r0:, :] * jnp.exp2(b3[:, r0:, :] - b3[:, r0 - 1:r0, :])).astype(BF16))
    for c in range(nc):
        kc = kk[c * CHUNK:(c + 1) * CHUNK, :]
        kbd = jnp.where(head_bd, jnp.concatenate([kc] * GLA_HEADS, axis=0), 0.0).astype(BF16)
        lhs = jnp.concatenate([qj[c] for qj in qj_all], axis=0)
        out = lax.dot_general(lhs, kbd, (((1,), (1,)), ((), ())), preferred_element_type=F32)
        off = 0
        for j in range(NSUB - 1):
            r0 = SUB * (j + 1)
            n = CHUNK - r0
            sc3[c, r0:, :] = jnp.where(col_blk == j, out[off:off + n, :], sc3[c, r0:, :])
            off += n

    states = [s_ref[hd] for hd in range(GLA_HEADS)]
    for c in range(nc):
        rows = slice(c * CHUNK, (c + 1) * CHUNK)
        bt = jnp.transpose(b[(c + 1) * CHUNK - 8:(c + 1) * CHUNK, :])
        decay_col = jnp.exp2(bt[:, 7:8])
        for hd in range(GLA_HEADS):
            sb_s[c, hd] = states[hd].astype(BF16)
        for p in range(2):
            lanes = slice(p * PAIR_W, (p + 1) * PAIR_W)
            kdt = jnp.transpose(kd[rows, lanes]).astype(BF16)
            ktv = _dot(kdt, v_s[rows, p * PAIR_V:(p + 1) * PAIR_V])
            dec = decay_col[lanes, :]
            h0, h1 = 2 * p, 2 * p + 1
            states[h0] = states[h0] * dec[0:GLA_DK, :] + ktv[0:GLA_DK, 0:GLA_DV]
            states[h1] = states[h1] * dec[GLA_DK:PAIR_W, :] + ktv[GLA_DK:PAIR_W, GLA_DV:PAIR_V]
    for hd in range(GLA_HEADS):
        s_ref[hd] = states[hd]

    for c in range(nc):
        rows = slice(c * CHUNK, (c + 1) * CHUNK)
        for p in range(2):
            lanes = slice(p * PAIR_W, (p + 1) * PAIR_W)
            h0, h1 = 2 * p, 2 * p + 1
            v0 = v_s[rows, h0 * GLA_DV:(h0 + 1) * GLA_DV]
            v1 = v_s[rows, h1 * GLA_DV:(h1 + 1) * GLA_DV]
            lhs = jnp.concatenate([sc3[c, :, lanes].astype(BF16), qg[rows, lanes]], axis=1)
            w = jnp.concatenate([
                jnp.concatenate([v0, zero_v], axis=1),
                jnp.concatenate([zero_v, v1], axis=1),
                jnp.concatenate([sb_s[c, h0], zero_v], axis=1),
                jnp.concatenate([zero_v, sb_s[c, h1]], axis=1)], axis=0)
            o_s[rows, p * PAIR_V:(p + 1) * PAIR_V] = _dot(lhs, w)

    o = o_s[...]
    gn = gn_ref[...]
    parts = []
    for hd in range(GLA_HEADS):
        oh = o[:, hd * GLA_DV:(hd + 1) * GLA_DV]
        ms = jnp.mean(oh * oh, axis=-1, keepdims=True)
        parts.append(oh * lax.rsqrt(ms + EPS) * gn)
    y_gla = jnp.concatenate(parts, axis=1) * (g_out * jax.nn.sigmoid(g_out))
    y = jnp.concatenate([y_conv, y_gla], axis=1).astype(BF16)
    o_ref[0] = x + gate * _dot(y, wout_ref[...].astype(BF16))
    _cast_side_streams((nxt_in_ref, nxt_out_ref), (nxt_in_bf_ref, nxt_out_bf_ref))


def _mix_call(x, ada3, norm_w, w_in_t, conv_w, w_gk2, b_gk, gla_norm, w_out, nxt_in, nxt_out,
              *, tt):
    bsz, t, d = x.shape
    nc = tt // CHUNK
    n_t = t // tt
    band = _band_masks()
    in_spec = _side_spec(nxt_in.shape, bsz, n_t)
    out_spec = _side_spec(nxt_out.shape, bsz, n_t)
    tile_spec = pl.BlockSpec((1, tt, d), lambda b, i: (b, i, 0))
    ada_spec = pl.BlockSpec((1, N_ADA, d), lambda b, i: (b, 0, 0))
    n_const = 8

    def outer(*refs):
        x_hbm, ada_hbm, nin_hbm, nout_hbm = refs[:4]
        consts = refs[4:4 + n_const]
        o_hbm, nin_bf_hbm, nout_bf_hbm = refs[4 + n_const:7 + n_const]
        scratch = refs[7 + n_const:]

        def tile(x_ref, ada_ref, nin_ref, nout_ref, o_ref, nin_bf_ref, nout_bf_ref):
            _mix_kernel(x_ref, ada_ref, *consts, nin_ref, nout_ref, o_ref, nin_bf_ref, nout_bf_ref,
                        *scratch, tt=tt)

        pltpu.emit_pipeline(
            tile, grid=(bsz, n_t),
            in_specs=[tile_spec, ada_spec, in_spec, out_spec],
            out_specs=[tile_spec, in_spec, out_spec],
        )(x_hbm, ada_hbm, nin_hbm, nout_hbm, o_hbm, nin_bf_hbm, nout_bf_hbm)

    hbm = pl.BlockSpec(memory_space=pl.ANY)
    vmem = pl.BlockSpec(memory_space=pltpu.VMEM)
    return pl.pallas_call(
        outer,
        in_specs=[hbm] * 4 + [vmem] * n_const,
        out_specs=[hbm] * 3,
        out_shape=[jax.ShapeDtypeStruct(x.shape, F32),
                   jax.ShapeDtypeStruct(nxt_in.shape, BF16),
                   jax.ShapeDtypeStruct(nxt_out.shape, BF16)],
        scratch_shapes=[
            pltpu.VMEM((GLA_HEADS, GLA_DK, GLA_DV), F32),
            pltpu.VMEM((tt + 8, CONV_WIDTH), F32),
            pltpu.VMEM((tt, KEY_W), F32),
            pltpu.VMEM((nc, PAD + CHUNK, KEY_W), F32),
            pltpu.VMEM((nc, 4 * CHUNK, KEY_W), BF16),
            pltpu.VMEM((nc, PAD + CHUNK, KEY_W), F32),
            pltpu.VMEM((tt, VAL_W), BF16),
            pltpu.VMEM((tt, VAL_W), F32),
            pltpu.VMEM((nc, CHUNK, KEY_W), F32),
            pltpu.VMEM((nc, GLA_HEADS, GLA_DK, GLA_DV), BF16),
        ],
        compiler_params=pltpu.CompilerParams(vmem_limit_bytes=VMEM_LIMIT),
        name="token_mixer",
    )(x, ada3, nxt_in, nxt_out, norm_w, w_in_t, conv_w, w_gk2, b_gk, gla_norm, w_out, band)


def kernel(x, c, w_ada, b_ada, norm_ffn1, w_ffn1_in, w_ffn1_out, norm_mix, w_mix_in, conv_w,
           w_gk2, b_gk, gla_norm, w_mix_out, norm_ffn2, w_ffn2_in, w_ffn2_out, norm_final):
    bsz, t, d = x.shape
    depth = w_ada.shape[0]
    norm_f = norm_final.reshape(1, d)
    for l in range(depth):
        last = l == depth - 1
        ada3 = _ada_call(c, w_ada[l], b_ada[l].reshape(1, -1)).reshape(bsz, N_ADA, d)
        x, (wm_in_t, wm_out) = _ffn_call(
            x, ada3, norm_ffn1[l].reshape(1, d), w_ffn1_in[l], w_ffn1_out[l], norm_f,
            side=(jnp.transpose(w_mix_in[l]), w_mix_out[l]),
            ada_base=0, final_norm=False, tm=TOKEN_TILE)
        x, w2_in, w2_out = _mix_call(
            x, ada3, norm_mix[l].reshape(1, d), wm_in_t, conv_w[l], w_gk2[l],
            b_gk[l].reshape(1, KEY_W), gla_norm[l].reshape(1, GLA_DV), wm_out,
            w_ffn2_in[l], w_ffn2_out[l], tt=2 * TOKEN_TILE)
        x, _ = _ffn_call(x, ada3, norm_ffn2[l].reshape(1, d), w2_in, w2_out, norm_f,
                         ada_base=6, final_norm=last, tm=2 * TOKEN_TILE)
    return x
```

```python
import functools
import math

import jax
import jax.numpy as jnp
import numpy as np
from jax import lax
from jax.experimental import pallas as pl
from jax.experimental.pallas import tpu as pltpu

F32 = jnp.float32
BF16 = jnp.bfloat16

EPS = 1e-6
CONV_WIDTH = 512
GLA_HEADS = 4
GLA_DK = 64
GLA_DV = 128
KEY_W = GLA_HEADS * GLA_DK
VAL_W = GLA_HEADS * GLA_DV
GATE_RANK = 16
GATE_NORMALIZER = 16.0
CHUNK = 64
SUB = 8
NSUB = CHUNK // SUB
PAD = 8
N_ADA = 9
PAIR_W = 2 * GLA_DK
PAIR_V = 2 * GLA_DV
MIX_MAIN = 3 * CONV_WIDTH + 2 * KEY_W + 2 * VAL_W
VMEM_LIMIT = 58 * 1024 * 1024
LOG2E = math.log2(math.e)
FFN_CHUNK = 256
ADA_K_BLOCK = 128
TOKEN_TILE = 512


def _dot(a, b):
    return jnp.dot(a, b, preferred_element_type=F32)


def _norm_mod(x, gain, shift, scale):
    ms = jnp.mean(x * x, axis=-1, keepdims=True)
    y = x * lax.rsqrt(ms + EPS) * gain
    return y * (1.0 + scale) + shift


def _ada_kernel(c_ref, w_ref, b_ref, o_ref):
    @pl.when(pl.program_id(0) == 0)
    def _():
        o_ref[...] = jnp.broadcast_to(b_ref[...], o_ref.shape)

    c = c_ref[...]
    ca = (c * jax.nn.sigmoid(c)).astype(BF16)
    o_ref[...] += _dot(ca, w_ref[...].astype(BF16))


def _ada_call(c, w_ada, b_ada):
    rows, d = c.shape
    n = w_ada.shape[1]
    tk = ADA_K_BLOCK
    return pl.pallas_call(
        _ada_kernel,
        grid=(d // tk,),
        in_specs=[
            pl.BlockSpec((rows, tk), lambda k: (0, k)),
            pl.BlockSpec((tk, n), lambda k: (k, 0)),
            pl.BlockSpec((1, n), lambda k: (0, 0)),
        ],
        out_specs=pl.BlockSpec((rows, n), lambda k: (0, 0)),
        out_shape=jax.ShapeDtypeStruct((rows, n), F32),
        compiler_params=pltpu.CompilerParams(
            dimension_semantics=("arbitrary",), vmem_limit_bytes=VMEM_LIMIT),
        name="ada_proj",
    )(c, w_ada, b_ada)


def _side_spec(shape, n_rows_grid, n_cols_grid):
    steps = n_rows_grid * n_cols_grid
    for axis, unit in ((0, 16), (1, 128)):
        k = 1
        while k < steps:
            n_blocks = steps // k
            size = shape[axis] // n_blocks
            if shape[axis] % n_blocks == 0 and size % unit == 0:
                block = (size, shape[1]) if axis == 0 else (shape[0], size)
                if axis == 0:
                    return pl.BlockSpec(block, lambda b, i, k=k: ((b * n_cols_grid + i) // k, 0))
                return pl.BlockSpec(block, lambda b, i, k=k: (0, (b * n_cols_grid + i) // k))
            k *= 2
    raise ValueError(f"cannot stream a {shape} matrix over {steps} grid steps")


def _cast_side_streams(src_refs, dst_refs):
    for src, dst in zip(src_refs, dst_refs):
        dst[...] = src[...].astype(BF16)


def _ffn_tile(x_ref, ada_ref, side_in, o_ref, side_out, nw_ref, win_ref, wout_ref, nf_ref, a_ref,
              *, ada_base, chunks, d_ff, final_norm):
    _cast_side_streams(side_in, side_out)
    x = x_ref[0]
    shift = ada_ref[0, ada_base:ada_base + 1, :]
    scale = ada_ref[0, ada_base + 1:ada_base + 2, :]
    gate = ada_ref[0, ada_base + 2:ada_base + 3, :]
    h = _norm_mod(x, nw_ref[...], shift, scale).astype(BF16)
    off = 0
    for w in chunks:
        g = _dot(h, win_ref[:, off:off + w].astype(BF16))
        u = _dot(h, win_ref[:, d_ff + off:d_ff + off + w].astype(BF16))
        a_ref[:, off:off + w] = (g * jax.nn.sigmoid(g) * u).astype(BF16)
        off += w
    n_rows = a_ref.shape[0]
    n_halves = 2 if final_norm else 1
    for r in range(n_halves):
        rs = slice(r * n_rows // n_halves, (r + 1) * n_rows // n_halves)
        y = _dot(a_ref[rs, :], wout_ref[...].astype(BF16))
        out = x[rs] + (0.5 * gate) * y
        if final_norm:
            ms = jnp.mean(out * out, axis=-1, keepdims=True)
            out = out * lax.rsqrt(ms + EPS) * nf_ref[...]
        o_ref[0, rs, :] = out


def _ffn_call(x, ada3, norm_w, w_in, w_out, norm_f, side=(), *, ada_base, final_norm, tm):
    bsz, t, d = x.shape
    d_ff = w_out.shape[0]
    n_t = t // tm
    n_side = len(side)
    chunks = [FFN_CHUNK] * (d_ff // FFN_CHUNK)
    if d_ff % FFN_CHUNK:
        chunks.append(d_ff % FFN_CHUNK)
    side_specs = [_side_spec(w.shape, bsz, n_t) for w in side]
    tile_spec = pl.BlockSpec((1, tm, d), lambda b, i: (b, i, 0))
    ada_spec = pl.BlockSpec((1, N_ADA, d), lambda b, i: (b, 0, 0))

    def outer(*refs):
        x_hbm, ada_hbm = refs[:2]
        side_hbm = refs[2:2 + n_side]
        nw_ref, win_ref, wout_ref, nf_ref = refs[2 + n_side:6 + n_side]
        o_hbm = refs[6 + n_side]
        side_out_hbm = refs[7 + n_side:7 + 2 * n_side]
        a_ref = refs[7 + 2 * n_side]

        def tile(*trefs):
            x_ref, ada_ref = trefs[:2]
            s_in = trefs[2:2 + n_side]
            o_ref = trefs[2 + n_side]
            s_out = trefs[3 + n_side:3 + 2 * n_side]
            _ffn_tile(x_ref, ada_ref, s_in, o_ref, s_out, nw_ref, win_ref, wout_ref, nf_ref, a_ref,
                      ada_base=ada_base, chunks=tuple(chunks), d_ff=d_ff, final_norm=final_norm)

        pltpu.emit_pipeline(
            tile, grid=(bsz, n_t),
            in_specs=[tile_spec, ada_spec] + side_specs,
            out_specs=[tile_spec] + side_specs,
        )(x_hbm, ada_hbm, *side_hbm, o_hbm, *side_out_hbm)

    hbm = pl.BlockSpec(memory_space=pl.ANY)
    vmem = pl.BlockSpec(memory_space=pltpu.VMEM)
    outs = pl.pallas_call(
        outer,
        in_specs=[hbm, hbm] + [hbm] * n_side + [vmem, vmem, vmem, vmem],
        out_specs=[hbm] + [hbm] * n_side,
        out_shape=[jax.ShapeDtypeStruct(x.shape, F32)]
        + [jax.ShapeDtypeStruct(w.shape, BF16) for w in side],
        scratch_shapes=[pltpu.VMEM((tm, d_ff), BF16)],
        compiler_params=pltpu.CompilerParams(vmem_limit_bytes=VMEM_LIMIT),
        name="ffn_final" if final_norm else "ffn",
    )(x, ada3, *side, norm_w, w_in, w_out, norm_f)
    return outs[0], tuple(outs[1:])


def _band_masks():
    i = np.arange(CHUNK)[None, :, None]
    j = (np.arange(KEY_W) % GLA_DK)[None, None, :]
    d = np.arange(SUB)[:, None, None]
    return jnp.asarray(((j == i - d) & (i % SUB >= d)).astype(np.float32))


def _mix_kernel(x_ref, ada_ref, nw_ref, win_ref, convw_ref, wgk2_ref, bgk_ref,
                gn_ref, wout_ref, band_ref, nxt_in_ref, nxt_out_ref,
                o_ref, nxt_in_bf_ref, nxt_out_bf_ref,
                s_ref, u_ref, q_s, k_s, lgs3, b_s, v_s, o_s, sc3, sb_s, *, tt):
    nc = tt // CHUNK
    t_idx = pl.program_id(1)

    @pl.when(t_idx == 0)
    def _():
        s_ref[...] = jnp.zeros(s_ref.shape, F32)
        u_ref[0:8, :] = jnp.zeros((8, CONV_WIDTH), F32)
        k_s[:, 0:PAD, :] = jnp.zeros((nc, PAD, KEY_W), F32)
        b_s[:, 0:PAD, :] = jnp.zeros((nc, PAD, KEY_W), F32)
        lgs3[:, 3 * CHUNK:4 * CHUNK, :] = jnp.zeros((nc, CHUNK, KEY_W), BF16)

    x = x_ref[0]
    shift = ada_ref[0, 3:4, :]
    scale = ada_ref[0, 4:5, :]
    gate = ada_ref[0, 5:6, :]
    h = _norm_mod(x, nw_ref[...], shift, scale).astype(BF16)

    def proj(lo, width):
        return lax.dot_general(h, win_ref[lo:lo + width, :].astype(BF16),
                               (((1,), (1,)), ((), ())), preferred_element_type=F32)

    tri_r = lax.broadcasted_iota(jnp.int32, (CHUNK, 4 * CHUNK), 0)
    tri_c = lax.broadcasted_iota(jnp.int32, (CHUNK, 4 * CHUNK), 1)
    lower4 = ((tri_c % CHUNK) <= tri_r).astype(BF16)
    hr = lax.broadcasted_iota(jnp.int32, (KEY_W, KEY_W), 0) // GLA_DK
    hc = lax.broadcasted_iota(jnp.int32, (KEY_W, KEY_W), 1) // GLA_DK
    head_bd = hr == hc
    ones_bd = head_bd.astype(BF16)
    col_blk = (lax.broadcasted_iota(jnp.int32, (1, KEY_W), 1) % GLA_DK) // SUB
    zero_v = jnp.zeros((CHUNK, GLA_DV), BF16)

    base = 3 * CONV_WIDTH
    q_s[...] = proj(base, KEY_W) * (GLA_DK ** -0.5)
    k_s[:, PAD:PAD + CHUNK, :] = proj(base + KEY_W, KEY_W).reshape(nc, CHUNK, KEY_W)
    v_s[...] = proj(base + 2 * KEY_W, VAL_W).astype(BF16)
    gk_t = lax.dot_general(win_ref[MIX_MAIN:MIX_MAIN + GATE_RANK, :].astype(BF16), h,
                           (((1,), (1,)), ((), ())), preferred_element_type=F32)
    z = lax.dot_general(gk_t.astype(BF16), wgk2_ref[...].astype(BF16),
                        (((0,), (0,)), ((), ())), preferred_element_type=F32) + bgk_ref[...]
    log_sig = jnp.minimum(z, 0.0) - jnp.log1p(jnp.exp(-jnp.abs(z)))
    lg = log_sig * (LOG2E / GATE_NORMALIZER)
    lg_hi = lg.astype(BF16)
    rem = lg - lg_hi.astype(F32)
    lg_mid = rem.astype(BF16)
    lg_lo = (rem - lg_mid.astype(F32)).astype(BF16)
    lgs3[:, 0:CHUNK, :] = lg_hi.reshape(nc, CHUNK, KEY_W)
    lgs3[:, CHUNK:2 * CHUNK, :] = lg_mid.reshape(nc, CHUNK, KEY_W)
    lgs3[:, 2 * CHUNK:3 * CHUNK, :] = lg_lo.reshape(nc, CHUNK, KEY_W)
    for c in range(nc):
        b_s[c, PAD:PAD + CHUNK, :] = _dot(lower4, lgs3[c])

    cb = proj(0, CONV_WIDTH)
    u = proj(CONV_WIDTH, CONV_WIDTH) * proj(2 * CONV_WIDTH, CONV_WIDTH)
    u_ref[8:8 + tt, :] = u
    cw = convw_ref[...]
    conv = (u_ref[6:6 + tt, :] * cw[0:1, :] + u_ref[7:7 + tt, :] * cw[1:2, :]
            + u * cw[2:3, :])
    y_conv = cb * conv
    u_ref[0:8, :] = u_ref[tt:tt + 8, :]

    q = q_s[...]
    k = k_s[:, PAD:PAD + CHUNK, :].reshape(tt, KEY_W)
    b = b_s[:, PAD:PAD + CHUNK, :].reshape(tt, KEY_W)
    b3 = b.reshape(nc, CHUNK, KEY_W)
    q3 = q.reshape(nc, CHUNK, KEY_W)
    b_last = jnp.broadcast_to(b3[:, CHUNK - 1:CHUNK, :], (nc, CHUNK, KEY_W)).reshape(tt, KEY_W)
    b8 = b.reshape(tt // SUB, SUB, KEY_W)
    b_end = jnp.broadcast_to(b8[:, SUB - 1:SUB, :], (tt // SUB, SUB, KEY_W)).reshape(tt, KEY_W)
    qg = (q * jnp.exp2(b)).astype(BF16)
    kd = k * jnp.exp2(b_last - b)
    kk = k * jnp.exp2(b_end - b)

    s = None
    for d in range(SUB):
        if d == 0:
            e = q * k
        else:
            k_d = k_s[:, PAD - d:PAD - d + CHUNK, :].reshape(tt, KEY_W)
            b_d = b_s[:, PAD - d:PAD - d + CHUNK, :].reshape(tt, KEY_W)
            e = q * k_d * jnp.exp2(b - b_d)
        term = _dot(e.astype(BF16), ones_bd).reshape(nc, CHUNK, KEY_W) * band_ref[d]
        s = term if s is None else s + term
    sc3[...] = s

    g_out = proj(base + 2 * KEY_W + VAL_W, VAL_W)

    qj_all = []
    for j in range(NSUB - 1):
        r0 = SUB * (j + 1)
        qj_all.append((q3[:, r0:, :] * jnp.exp2(b3[:, r0:, :] - b3[:, r0 - 1:r0, :])).astype(BF16))
    for c in range(nc):
        kc = kk[c * CHUNK:(c + 1) * CHUNK, :]
        kbd = jnp.where(head_bd, jnp.concatenate([kc] * GLA_HEADS, axis=0), 0.0).astype(BF16)
        lhs = jnp.concatenate([qj[c] for qj in qj_all], axis=0)
        out = lax.dot_general(lhs, kbd, (((1,), (1,)), ((), ())), preferred_element_type=F32)
        off = 0
        for j in range(NSUB - 1):
            r0 = SUB * (j + 1)
            n = CHUNK - r0
            sc3[c, r0:, :] = jnp.where(col_blk == j, out[off:off + n, :], sc3[c, r0:, :])
            off += n

    states = [s_ref[hd] for hd in range(GLA_HEADS)]
    for c in range(nc):
        rows = slice(c * CHUNK, (c + 1) * CHUNK)
        bt = jnp.transpose(b[(c + 1) * CHUNK - 8:(c + 1) * CHUNK, :])
        decay_col = jnp.exp2(bt[:, 7:8])
        for hd in range(GLA_HEADS):
            sb_s[c, hd] = states[hd].astype(BF16)
        for p in range(2):
            lanes = slice(p * PAIR_W, (p + 1) * PAIR_W)
            kdt = jnp.transpose(kd[rows, lanes]).astype(BF16)
            ktv = _dot(kdt, v_s[rows, p * PAIR_V:(p + 1) * PAIR_V])
            dec = decay_col[lanes, :]
            h0, h1 = 2 * p, 2 * p + 1
            states[h0] = states[h0] * dec[0:GLA_DK, :] + ktv[0:GLA_DK, 0:GLA_DV]
            states[h1] = states[h1] * dec[GLA_DK:PAIR_W, :] + ktv[GLA_DK:PAIR_W, GLA_DV:PAIR_V]
    for hd in range(GLA_HEADS):
        s_ref[hd] = states[hd]

    for c in range(nc):
        rows = slice(c * CHUNK, (c + 1) * CHUNK)
        for p in range(2):
            lanes = slice(p * PAIR_W, (p + 1) * PAIR_W)
            h0, h1 = 2 * p, 2 * p + 1
            v0 = v_s[rows, h0 * GLA_DV:(h0 + 1) * GLA_DV]
            v1 = v_s[rows, h1 * GLA_DV:(h1 + 1) * GLA_DV]
            lhs = jnp.concatenate([sc3[c, :, lanes].astype(BF16), qg[rows, lanes]], axis=1)
            w = jnp.concatenate([
                jnp.concatenate([v0, zero_v], axis=1),
                jnp.concatenate([zero_v, v1], axis=1),
                jnp.concatenate([sb_s[c, h0], zero_v], axis=1),
                jnp.concatenate([zero_v, sb_s[c, h1]], axis=1)], axis=0)
            o_s[rows, p * PAIR_V:(p + 1) * PAIR_V] = _dot(lhs, w)

    o = o_s[...]
    gn = gn_ref[...]
    parts = []
    for hd in range(GLA_HEADS):
        oh = o[:, hd * GLA_DV:(hd + 1) * GLA_DV]
        ms = jnp.mean(oh * oh, axis=-1, keepdims=True)
        parts.append(oh * lax.rsqrt(ms + EPS) * gn)
    y_gla = jnp.concatenate(parts, axis=1) * (g_out * jax.nn.sigmoid(g_out))
    y = jnp.concatenate([y_conv, y_gla], axis=1).astype(BF16)
    o_ref[0] = x + gate * _dot(y, wout_ref[...].astype(BF16))
    _cast_side_streams((nxt_in_ref, nxt_out_ref), (nxt_in_bf_ref, nxt_out_bf_ref))


def _mix_call(x, ada3, norm_w, w_in_t, conv_w, w_gk2, b_gk, gla_norm, w_out, nxt_in, nxt_out,
              *, tt):
    bsz, t, d = x.shape
    nc = tt // CHUNK
    n_t = t // tt
    band = _band_masks()
    in_spec = _side_spec(nxt_in.shape, bsz, n_t)
    out_spec = _side_spec(nxt_out.shape, bsz, n_t)
    tile_spec = pl.BlockSpec((1, tt, d), lambda b, i: (b, i, 0))
    ada_spec = pl.BlockSpec((1, N_ADA, d), lambda b, i: (b, 0, 0))
    n_const = 8

    def outer(*refs):
        x_hbm, ada_hbm, nin_hbm, nout_hbm = refs[:4]
        consts = refs[4:4 + n_const]
        o_hbm, nin_bf_hbm, nout_bf_hbm = refs[4 + n_const:7 + n_const]
        scratch = refs[7 + n_const:]

        def tile(x_ref, ada_ref, nin_ref, nout_ref, o_ref, nin_bf_ref, nout_bf_ref):
            _mix_kernel(x_ref, ada_ref, *consts, nin_ref, nout_ref, o_ref, nin_bf_ref, nout_bf_ref,
                        *scratch, tt=tt)

        pltpu.emit_pipeline(
            tile, grid=(bsz, n_t),
            in_specs=[tile_spec, ada_spec, in_spec, out_spec],
            out_specs=[tile_spec, in_spec, out_spec],
        )(x_hbm, ada_hbm, nin_hbm, nout_hbm, o_hbm, nin_bf_hbm, nout_bf_hbm)

    hbm = pl.BlockSpec(memory_space=pl.ANY)
    vmem = pl.BlockSpec(memory_space=pltpu.VMEM)
    return pl.pallas_call(
        outer,
        in_specs=[hbm] * 4 + [vmem] * n_const,
        out_specs=[hbm] * 3,
        out_shape=[jax.ShapeDtypeStruct(x.shape, F32),
                   jax.ShapeDtypeStruct(nxt_in.shape, BF16),
                   jax.ShapeDtypeStruct(nxt_out.shape, BF16)],
        scratch_shapes=[
            pltpu.VMEM((GLA_HEADS, GLA_DK, GLA_DV), F32),
            pltpu.VMEM((tt + 8, CONV_WIDTH), F32),
            pltpu.VMEM((tt, KEY_W), F32),
            pltpu.VMEM((nc, PAD + CHUNK, KEY_W), F32),
            pltpu.VMEM((nc, 4 * CHUNK, KEY_W), BF16),
            pltpu.VMEM((nc, PAD + CHUNK, KEY_W), F32),
            pltpu.VMEM((tt, VAL_W), BF16),
            pltpu.VMEM((tt, VAL_W), F32),
            pltpu.VMEM((nc, CHUNK, KEY_W), F32),
            pltpu.VMEM((nc, GLA_HEADS, GLA_DK, GLA_DV), BF16),
        ],
        compiler_params=pltpu.CompilerParams(vmem_limit_bytes=VMEM_LIMIT),
        name="token_mixer",
    )(x, ada3, nxt_in, nxt_out, norm_w, w_in_t, conv_w, w_gk2, b_gk, gla_norm, w_out, band)


def kernel(x, c, w_ada, b_ada, norm_ffn1, w_ffn1_in, w_ffn1_out, norm_mix, w_mix_in, conv_w,
           w_gk2, b_gk, gla_norm, w_mix_out, norm_ffn2, w_ffn2_in, w_ffn2_out, norm_final):
    bsz, t, d = x.shape
    depth = w_ada.shape[0]
    norm_f = norm_final.reshape(1, d)
    for l in range(depth):
        last = l == depth - 1
        ada3 = _ada_call(c, w_ada[l], b_ada[l].reshape(1, -1)).reshape(bsz, N_ADA, d)
        x, (wm_in_t, wm_out) = _ffn_call(
            x, ada3, norm_ffn1[l].reshape(1, d), w_ffn1_in[l], w_ffn1_out[l], norm_f,
            side=(jnp.transpose(w_mix_in[l]), w_mix_out[l]),
            ada_base=0, final_norm=False, tm=TOKEN_TILE)
        x, w2_in, w2_out = _mix_call(
            x, ada3, norm_mix[l].reshape(1, d), wm_in_t, conv_w[l], w_gk2[l],
            b_gk[l].reshape(1, KEY_W), gla_norm[l].reshape(1, GLA_DV), wm_out,
            w_ffn2_in[l], w_ffn2_out[l], tt=2 * TOKEN_TILE)
        x, _ = _ffn_call(x, ada3, norm_ffn2[l].reshape(1, d), w2_in, w2_out, norm_f,
                         ada_base=6, final_norm=last, tm=2 * TOKEN_TILE)
    return x
```

```python
import functools
import math

import jax
import jax.numpy as jnp
import numpy as np
from jax import lax
from jax.experimental import pallas as pl
from jax.experimental.pallas import tpu as pltpu

F32 = jnp.float32
BF16 = jnp.bfloat16

EPS = 1e-6
CONV_WIDTH = 512
GLA_HEADS = 4
GLA_DK = 64
GLA_DV = 128
KEY_W = GLA_HEADS * GLA_DK
VAL_W = GLA_HEADS * GLA_DV
GATE_RANK = 16
GATE_NORMALIZER = 16.0
CHUNK = 64
SUB = 8
NSUB = CHUNK // SUB
PAD = 8
N_ADA = 9
PAIR_W = 2 * GLA_DK
PAIR_V = 2 * GLA_DV
MIX_MAIN = 3 * CONV_WIDTH + 2 * KEY_W + 2 * VAL_W
VMEM_LIMIT = 58 * 1024 * 1024
LOG2E = math.log2(math.e)
FFN_CHUNK = 256
ADA_K_BLOCK = 128
TOKEN_TILE = 512


def _dot(a, b):
    return jnp.dot(a, b, preferred_element_type=F32)


def _norm_mod(x, gain, shift, scale):
    ms = jnp.mean(x * x, axis=-1, keepdims=True)
    y = x * lax.rsqrt(ms + EPS) * gain
    return y * (1.0 + scale) + shift


def _ada_kernel(c_ref, w_ref, b_ref, o_ref):
    @pl.when(pl.program_id(0) == 0)
    def _():
        o_ref[...] = jnp.broadcast_to(b_ref[...], o_ref.shape)

    c = c_ref[...]
    ca = (c * jax.nn.sigmoid(c)).astype(BF16)
    o_ref[...] += _dot(ca, w_ref[...].astype(BF16))


def _ada_call(c, w_ada, b_ada):
    rows, d = c.shape
    n = w_ada.shape[1]
    tk = ADA_K_BLOCK
    return pl.pallas_call(
        _ada_kernel,
        grid=(d // tk,),
        in_specs=[
            pl.BlockSpec((rows, tk), lambda k: (0, k)),
            pl.BlockSpec((tk, n), lambda k: (k, 0)),
            pl.BlockSpec((1, n), lambda k: (0, 0)),
        ],
        out_specs=pl.BlockSpec((rows, n), lambda k: (0, 0)),
        out_shape=jax.ShapeDtypeStruct((rows, n), F32),
        compiler_params=pltpu.CompilerParams(
            dimension_semantics=("arbitrary",), vmem_limit_bytes=VMEM_LIMIT),
        name="ada_proj",
    )(c, w_ada, b_ada)


def _side_spec(shape, n_rows_grid, n_cols_grid):
    steps = n_rows_grid * n_cols_grid
    for axis, unit in ((0, 16), (1, 128)):
        k = 1
        while k < steps:
            n_blocks = steps // k
            size = shape[axis] // n_blocks
            if shape[axis] % n_blocks == 0 and size % unit == 0:
                block = (size, shape[1]) if axis == 0 else (shape[0], size)
                if axis == 0:
                    return pl.BlockSpec(block, lambda b, i, k=k: ((b * n_cols_grid + i) // k, 0))
                return pl.BlockSpec(block, lambda b, i, k=k: (0, (b * n_cols_grid + i) // k))
            k *= 2
    raise ValueError(f"cannot stream a {shape} matrix over {steps} grid steps")


def _cast_side_streams(src_refs, dst_refs):
    for src, dst in zip(src_refs, dst_refs):
        dst[...] = src[...].astype(BF16)


def _ffn_tile(x_ref, ada_ref, side_in, o_ref, side_out, nw_ref, win_ref, wout_ref, nf_ref, a_ref,
              *, ada_base, chunks, d_ff, final_norm):
    _cast_side_streams(side_in, side_out)
    x = x_ref[0]
    d_model = x_ref.shape[2]
    row = pl.ds(pl.program_id(0), 1)
    shift = ada_ref[row, ada_base * d_model:(ada_base + 1) * d_model]
    scale = ada_ref[row, (ada_base + 1) * d_model:(ada_base + 2) * d_model]
    gate = ada_ref[row, (ada_base + 2) * d_model:(ada_base + 3) * d_model]
    h = _norm_mod(x, nw_ref[...], shift, scale).astype(BF16)
    off = 0
    for w in chunks:
        g = _dot(h, win_ref[:, off:off + w].astype(BF16))
        u = _dot(h, win_ref[:, d_ff + off:d_ff + off + w].astype(BF16))
        a_ref[:, off:off + w] = (g * jax.nn.sigmoid(g) * u).astype(BF16)
        off += w
    n_rows = a_ref.shape[0]
    n_halves = 2 if final_norm else 1
    for r in range(n_halves):
        rs = slice(r * n_rows // n_halves, (r + 1) * n_rows // n_halves)
        y = _dot(a_ref[rs, :], wout_ref[...].astype(BF16))
        out = x[rs] + (0.5 * gate) * y
        if final_norm:
            ms = jnp.mean(out * out, axis=-1, keepdims=True)
            out = out * lax.rsqrt(ms + EPS) * nf_ref[...]
        o_ref[0, rs, :] = out


def _ffn_call(x, ada3, norm_w, w_in, w_out, norm_f, side=(), *, ada_base, final_norm, tm):
    bsz, t, d = x.shape
    d_ff = w_out.shape[0]
    n_t = t // tm
    n_side = len(side)
    chunks = [FFN_CHUNK] * (d_ff // FFN_CHUNK)
    if d_ff % FFN_CHUNK:
        chunks.append(d_ff % FFN_CHUNK)
    side_specs = [_side_spec(w.shape, bsz, n_t) for w in side]
    tile_spec = pl.BlockSpec((1, tm, d), lambda b, i: (b, i, 0))
    ada_spec = pl.BlockSpec(ada3.shape, lambda b, i: (0, 0))

    def outer(*refs):
        x_hbm, ada_hbm = refs[:2]
        side_hbm = refs[2:2 + n_side]
        nw_ref, win_ref, wout_ref, nf_ref = refs[2 + n_side:6 + n_side]
        o_hbm = refs[6 + n_side]
        side_out_hbm = refs[7 + n_side:7 + 2 * n_side]
        a_ref = refs[7 + 2 * n_side]

        def tile(*trefs):
            x_ref, ada_ref = trefs[:2]
            s_in = trefs[2:2 + n_side]
            o_ref = trefs[2 + n_side]
            s_out = trefs[3 + n_side:3 + 2 * n_side]
            _ffn_tile(x_ref, ada_ref, s_in, o_ref, s_out, nw_ref, win_ref, wout_ref, nf_ref, a_ref,
                      ada_base=ada_base, chunks=tuple(chunks), d_ff=d_ff, final_norm=final_norm)

        pltpu.emit_pipeline(
            tile, grid=(bsz, n_t),
            in_specs=[tile_spec, ada_spec] + side_specs,
            out_specs=[tile_spec] + side_specs,
        )(x_hbm, ada_hbm, *side_hbm, o_hbm, *side_out_hbm)

    hbm = pl.BlockSpec(memory_space=pl.ANY)
    vmem = pl.BlockSpec(memory_space=pltpu.VMEM)
    outs = pl.pallas_call(
        outer,
        in_specs=[hbm, hbm] + [hbm] * n_side + [vmem, vmem, vmem, vmem],
        out_specs=[hbm] + [hbm] * n_side,
        out_shape=[jax.ShapeDtypeStruct(x.shape, F32)]
        + [jax.ShapeDtypeStruct(w.shape, BF16) for w in side],
        scratch_shapes=[pltpu.VMEM((tm, d_ff), BF16)],
        compiler_params=pltpu.CompilerParams(vmem_limit_bytes=VMEM_LIMIT),
        name="ffn_final" if final_norm else "ffn",
    )(x, ada3, *side, norm_w, w_in, w_out, norm_f)
    return outs[0], tuple(outs[1:])


def _band_masks():
    i = np.arange(CHUNK)[None, :, None]
    j = (np.arange(KEY_W) % GLA_DK)[None, None, :]
    d = np.arange(SUB)[:, None, None]
    return jnp.asarray(((j == i - d) & (i % SUB >= d)).astype(np.float32))


def _mix_kernel(x_ref, ada_ref, nw_ref, win_ref, convw_ref, wgk2_ref, bgk_ref,
                gn_ref, wout_ref, band_ref, nxt_in_ref, nxt_out_ref,
                o_ref, nxt_in_bf_ref, nxt_out_bf_ref,
                s_ref, u_ref, q_s, k_s, lgs3, b_s, v_s, o_s, sc3, sb_s, *, tt):
    nc = tt // CHUNK
    t_idx = pl.program_id(1)

    @pl.when(t_idx == 0)
    def _():
        s_ref[...] = jnp.zeros(s_ref.shape, F32)
        u_ref[0:8, :] = jnp.zeros((8, CONV_WIDTH), F32)
        k_s[:, 0:PAD, :] = jnp.zeros((nc, PAD, KEY_W), F32)
        b_s[:, 0:PAD, :] = jnp.zeros((nc, PAD, KEY_W), F32)
        lgs3[:, 3 * CHUNK:4 * CHUNK, :] = jnp.zeros((nc, CHUNK, KEY_W), BF16)

    x = x_ref[0]
    d_model = x_ref.shape[2]
    row = pl.ds(pl.program_id(0), 1)
    shift = ada_ref[row, 3 * d_model:4 * d_model]
    scale = ada_ref[row, 4 * d_model:5 * d_model]
    gate = ada_ref[row, 5 * d_model:6 * d_model]
    h = _norm_mod(x, nw_ref[...], shift, scale).astype(BF16)

    def proj(lo, width):
        return lax.dot_general(h, win_ref[lo:lo + width, :].astype(BF16),
                               (((1,), (1,)), ((), ())), preferred_element_type=F32)

    tri_r = lax.broadcasted_iota(jnp.int32, (CHUNK, 4 * CHUNK), 0)
    tri_c = lax.broadcasted_iota(jnp.int32, (CHUNK, 4 * CHUNK), 1)
    lower4 = ((tri_c % CHUNK) <= tri_r).astype(BF16)
    hr = lax.broadcasted_iota(jnp.int32, (KEY_W, KEY_W), 0) // GLA_DK
    hc = lax.broadcasted_iota(jnp.int32, (KEY_W, KEY_W), 1) // GLA_DK
    head_bd = hr == hc
    ones_bd = head_bd.astype(BF16)
    col_blk = (lax.broadcasted_iota(jnp.int32, (1, KEY_W), 1) % GLA_DK) // SUB
    zero_v = jnp.zeros((CHUNK, GLA_DV), BF16)

    base = 3 * CONV_WIDTH
    q_s[...] = proj(base, KEY_W) * (GLA_DK ** -0.5)
    k_s[:, PAD:PAD + CHUNK, :] = proj(base + KEY_W, KEY_W).reshape(nc, CHUNK, KEY_W)
    v_s[...] = proj(base + 2 * KEY_W, VAL_W).astype(BF16)
    gk_t = lax.dot_general(win_ref[MIX_MAIN:MIX_MAIN + GATE_RANK, :].astype(BF16), h,
                           (((1,), (1,)), ((), ())), preferred_element_type=F32)
    z = lax.dot_general(gk_t.astype(BF16), wgk2_ref[...].astype(BF16),
                        (((0,), (0,)), ((), ())), preferred_element_type=F32) + bgk_ref[...]
    log_sig = jnp.minimum(z, 0.0) - jnp.log1p(jnp.exp(-jnp.abs(z)))
    lg = log_sig * (LOG2E / GATE_NORMALIZER)
    lg_hi = lg.astype(BF16)
    rem = lg - lg_hi.astype(F32)
    lg_mid = rem.astype(BF16)
    lg_lo = (rem - lg_mid.astype(F32)).astype(BF16)
    lgs3[:, 0:CHUNK, :] = lg_hi.reshape(nc, CHUNK, KEY_W)
    lgs3[:, CHUNK:2 * CHUNK, :] = lg_mid.reshape(nc, CHUNK, KEY_W)
    lgs3[:, 2 * CHUNK:3 * CHUNK, :] = lg_lo.reshape(nc, CHUNK, KEY_W)
    for c in range(nc):
        b_s[c, PAD:PAD + CHUNK, :] = _dot(lower4, lgs3[c])

    cb = proj(0, CONV_WIDTH)
    u = proj(CONV_WIDTH, CONV_WIDTH) * proj(2 * CONV_WIDTH, CONV_WIDTH)
    u_ref[8:8 + tt, :] = u
    cw = convw_ref[...]
    conv = (u_ref[6:6 + tt, :] * cw[0:1, :] + u_ref[7:7 + tt, :] * cw[1:2, :]
            + u * cw[2:3, :])
    y_conv = cb * conv
    u_ref[0:8, :] = u_ref[tt:tt + 8, :]

    q = q_s[...]
    k = k_s[:, PAD:PAD + CHUNK, :].reshape(tt, KEY_W)
    b = b_s[:, PAD:PAD + CHUNK, :].reshape(tt, KEY_W)
    b3 = b.reshape(nc, CHUNK, KEY_W)
    q3 = q.reshape(nc, CHUNK, KEY_W)
    b_last = jnp.broadcast_to(b3[:, CHUNK - 1:CHUNK, :], (nc, CHUNK, KEY_W)).reshape(tt, KEY_W)
    b8 = b.reshape(tt // SUB, SUB, KEY_W)
    b_end = jnp.broadcast_to(b8[:, SUB - 1:SUB, :], (tt // SUB, SUB, KEY_W)).reshape(tt, KEY_W)
    qg = (q * jnp.exp2(b)).astype(BF16)
    kd = k * jnp.exp2(b_last - b)
    kk = k * jnp.exp2(b_end - b)

    s = None
    for d in range(SUB):
        if d == 0:
            e = q * k
        else:
            k_d = k_s[:, PAD - d:PAD - d + CHUNK, :].reshape(tt, KEY_W)
            b_d = b_s[:, PAD - d:PAD - d + CHUNK, :].reshape(tt, KEY_W)
            e = q * k_d * jnp.exp2(b - b_d)
        term = _dot(e.astype(BF16), ones_bd).reshape(nc, CHUNK, KEY_W) * band_ref[d]
        s = term if s is None else s + term
    sc3[...] = s

    g_out = proj(base + 2 * KEY_W + VAL_W, VAL_W)

    qj_all = []
    for j in range(NSUB - 1):
        r0 = SUB * (j + 1)
        qj_all.append((q3[:, r0:, :] * jnp.exp2(b3[:, r0:, :] - b3[:, r0 - 1:r0, :])).astype(BF16))
    for c in range(nc):
        kc = kk[c * CHUNK:(c + 1) * CHUNK, :]
        kbd = jnp.where(head_bd, jnp.concatenate([kc] * GLA_HEADS, axis=0), 0.0).astype(BF16)
        lhs = jnp.concatenate([qj[c] for qj in qj_all], axis=0)
        out = lax.dot_general(lhs, kbd, (((1,), (1,)), ((), ())), preferred_element_type=F32)
        off = 0
        for j in range(NSUB - 1):
            r0 = SUB * (j + 1)
            n = CHUNK - r0
            sc3[c, r0:, :] = jnp.where(col_blk == j, out[off:off + n, :], sc3[c, r0:, :])
            off += n

    states = [s_ref[hd] for hd in range(GLA_HEADS)]
    for c in range(nc):
        rows = slice(c * CHUNK, (c + 1) * CHUNK)
        bt = jnp.transpose(b[(c + 1) * CHUNK - 8:(c + 1) * CHUNK, :])
        decay_col = jnp.exp2(bt[:, 7:8])
        for hd in range(GLA_HEADS):
            sb_s[c, hd] = states[hd].astype(BF16)
        for p in range(2):
            lanes = slice(p * PAIR_W, (p + 1) * PAIR_W)
            kdt = jnp.transpose(kd[rows, lanes]).astype(BF16)
            ktv = _dot(kdt, v_s[rows, p * PAIR_V:(p + 1) * PAIR_V])
            dec = decay_col[lanes, :]
            h0, h1 = 2 * p, 2 * p + 1
            states[h0] = states[h0] * dec[0:GLA_DK, :] + ktv[0:GLA_DK, 0:GLA_DV]
            states[h1] = states[h1] * dec[GLA_DK:PAIR_W, :] + ktv[GLA_DK:PAIR_W, GLA_DV:PAIR_V]
    for hd in range(GLA_HEADS):
        s_ref[hd] = states[hd]

    for c in range(nc):
        rows = slice(c * CHUNK, (c + 1) * CHUNK)
        for p in range(2):
            lanes = slice(p * PAIR_W, (p + 1) * PAIR_W)
            h0, h1 = 2 * p, 2 * p + 1
            v0 = v_s[rows, h0 * GLA_DV:(h0 + 1) * GLA_DV]
            v1 = v_s[rows, h1 * GLA_DV:(h1 + 1) * GLA_DV]
            lhs = jnp.concatenate([sc3[c, :, lanes].astype(BF16), qg[rows, lanes]], axis=1)
            w = jnp.concatenate([
                jnp.concatenate([v0, zero_v], axis=1),
                jnp.concatenate([zero_v, v1], axis=1),
                jnp.concatenate([sb_s[c, h0], zero_v], axis=1),
                jnp.concatenate([zero_v, sb_s[c, h1]], axis=1)], axis=0)
            o_s[rows, p * PAIR_V:(p + 1) * PAIR_V] = _dot(lhs, w)

    o = o_s[...]
    gn = gn_ref[...]
    parts = []
    for hd in range(GLA_HEADS):
        oh = o[:, hd * GLA_DV:(hd + 1) * GLA_DV]
        ms = jnp.mean(oh * oh, axis=-1, keepdims=True)
        parts.append(oh * lax.rsqrt(ms + EPS) * gn)
    y_gla = jnp.concatenate(parts, axis=1) * (g_out * jax.nn.sigmoid(g_out))
    y = jnp.concatenate([y_conv, y_gla], axis=1).astype(BF16)
    o_ref[0] = x + gate * _dot(y, wout_ref[...].astype(BF16))
    _cast_side_streams((nxt_in_ref, nxt_out_ref), (nxt_in_bf_ref, nxt_out_bf_ref))


def _mix_call(x, ada3, norm_w, w_in_t, conv_w, w_gk2, b_gk, gla_norm, w_out, nxt_in, nxt_out,
              *, tt):
    bsz, t, d = x.shape
    nc = tt // CHUNK
    n_t = t // tt
    band = _band_masks()
    in_spec = _side_spec(nxt_in.shape, bsz, n_t)
    out_spec = _side_spec(nxt_out.shape, bsz, n_t)
    tile_spec = pl.BlockSpec((1, tt, d), lambda b, i: (b, i, 0))
    ada_spec = pl.BlockSpec(ada3.shape, lambda b, i: (0, 0))
    n_const = 8

    def outer(*refs):
        x_hbm, ada_hbm, nin_hbm, nout_hbm = refs[:4]
        consts = refs[4:4 + n_const]
        o_hbm, nin_bf_hbm, nout_bf_hbm = refs[4 + n_const:7 + n_const]
        scratch = refs[7 + n_const:]

        def tile(x_ref, ada_ref, nin_ref, nout_ref, o_ref, nin_bf_ref, nout_bf_ref):
            _mix_kernel(x_ref, ada_ref, *consts, nin_ref, nout_ref, o_ref, nin_bf_ref, nout_bf_ref,
                        *scratch, tt=tt)

        pltpu.emit_pipeline(
            tile, grid=(bsz, n_t),
            in_specs=[tile_spec, ada_spec, in_spec, out_spec],
            out_specs=[tile_spec, in_spec, out_spec],
        )(x_hbm, ada_hbm, nin_hbm, nout_hbm, o_hbm, nin_bf_hbm, nout_bf_hbm)

    hbm = pl.BlockSpec(memory_space=pl.ANY)
    vmem = pl.BlockSpec(memory_space=pltpu.VMEM)
    return pl.pallas_call(
        outer,
        in_specs=[hbm] * 4 + [vmem] * n_const,
        out_specs=[hbm] * 3,
        out_shape=[jax.ShapeDtypeStruct(x.shape, F32),
                   jax.ShapeDtypeStruct(nxt_in.shape, BF16),
                   jax.ShapeDtypeStruct(nxt_out.shape, BF16)],
        scratch_shapes=[
            pltpu.VMEM((GLA_HEADS, GLA_DK, GLA_DV), F32),
            pltpu.VMEM((tt + 8, CONV_WIDTH), F32),
            pltpu.VMEM((tt, KEY_W), F32),
            pltpu.VMEM((nc, PAD + CHUNK, KEY_W), F32),
            pltpu.VMEM((nc, 4 * CHUNK, KEY_W), BF16),
            pltpu.VMEM((nc, PAD + CHUNK, KEY_W), F32),
            pltpu.VMEM((tt, VAL_W), BF16),
            pltpu.VMEM((tt, VAL_W), F32),
            pltpu.VMEM((nc, CHUNK, KEY_W), F32),
            pltpu.VMEM((nc, GLA_HEADS, GLA_DK, GLA_DV), BF16),
        ],
        compiler_params=pltpu.CompilerParams(vmem_limit_bytes=VMEM_LIMIT),
        name="token_mixer",
    )(x, ada3, nxt_in, nxt_out, norm_w, w_in_t, conv_w, w_gk2, b_gk, gla_norm, w_out, band)


def kernel(x, c, w_ada, b_ada, norm_ffn1, w_ffn1_in, w_ffn1_out, norm_mix, w_mix_in, conv_w,
           w_gk2, b_gk, gla_norm, w_mix_out, norm_ffn2, w_ffn2_in, w_ffn2_out, norm_final):
    bsz, t, d = x.shape
    depth = w_ada.shape[0]
    norm_f = norm_final.reshape(1, d)
    for l in range(depth):
        last = l == depth - 1
        ada3 = _ada_call(c, w_ada[l], b_ada[l].reshape(1, -1))
        x, (wm_in_t, wm_out) = _ffn_call(
            x, ada3, norm_ffn1[l].reshape(1, d), w_ffn1_in[l], w_ffn1_out[l], norm_f,
            side=(jnp.transpose(w_mix_in[l]), w_mix_out[l]),
            ada_base=0, final_norm=False, tm=TOKEN_TILE)
        x, w2_in, w2_out = _mix_call(
            x, ada3, norm_mix[l].reshape(1, d), wm_in_t, conv_w[l], w_gk2[l],
            b_gk[l].reshape(1, KEY_W), gla_norm[l].reshape(1, GLA_DV), wm_out,
            w_ffn2_in[l], w_ffn2_out[l], tt=2 * TOKEN_TILE)
        x, _ = _ffn_call(x, ada3, norm_ffn2[l].reshape(1, d), w2_in, w2_out, norm_f,
                         ada_base=6, final_norm=last, tm=2 * TOKEN_TILE)
    return x
```

```python
import functools
import math

import jax
import jax.numpy as jnp
import numpy as np
from jax import lax
from jax.experimental import pallas as pl
from jax.experimental.pallas import tpu as pltpu

F32 = jnp.float32
BF16 = jnp.bfloat16

EPS = 1e-6
CONV_WIDTH = 512
GLA_HEADS = 4
GLA_DK = 64
GLA_DV = 128
KEY_W = GLA_HEADS * GLA_DK
VAL_W = GLA_HEADS * GLA_DV
GATE_RANK = 16
GATE_NORMALIZER = 16.0
CHUNK = 64
SUB = 8
NSUB = CHUNK // SUB
PAD = 8
N_ADA = 9
PAIR_W = 2 * GLA_DK
PAIR_V = 2 * GLA_DV
MIX_MAIN = 3 * CONV_WIDTH + 2 * KEY_W + 2 * VAL_W
VMEM_LIMIT = 58 * 1024 * 1024
LOG2E = math.log2(math.e)
FFN_CHUNK = 256
ADA_K_BLOCK = 128
TOKEN_TILE = 512


def _dot(a, b):
    return jnp.dot(a, b, preferred_element_type=F32)


def _norm_mod(x, gain, shift, scale):
    ms = jnp.mean(x * x, axis=-1, keepdims=True)
    y = x * lax.rsqrt(ms + EPS) * gain
    return y * (1.0 + scale) + shift


def _ada_kernel(c_ref, w_ref, b_ref, o_ref):
    @pl.when(pl.program_id(0) == 0)
    def _():
        o_ref[...] = jnp.broadcast_to(b_ref[...], o_ref.shape)

    c = c_ref[...]
    ca = (c * jax.nn.sigmoid(c)).astype(BF16)
    o_ref[...] += _dot(ca, w_ref[...].astype(BF16))


def _ada_call(c, w_ada, b_ada):
    rows, d = c.shape
    n = w_ada.shape[1]
    tk = ADA_K_BLOCK
    return pl.pallas_call(
        _ada_kernel,
        grid=(d // tk,),
        in_specs=[
            pl.BlockSpec((rows, tk), lambda k: (0, k)),
            pl.BlockSpec((tk, n), lambda k: (k, 0)),
            pl.BlockSpec((1, n), lambda k: (0, 0)),
        ],
        out_specs=pl.BlockSpec((rows, n), lambda k: (0, 0)),
        out_shape=jax.ShapeDtypeStruct((rows, n), F32),
        compiler_params=pltpu.CompilerParams(
            dimension_semantics=("arbitrary",), vmem_limit_bytes=VMEM_LIMIT),
        name="ada_proj",
    )(c, w_ada, b_ada)


def _side_spec(shape, n_rows_grid, n_cols_grid):
    steps = n_rows_grid * n_cols_grid
    for axis, unit in ((0, 16), (1, 128)):
        k = 1
        while k < steps:
            n_blocks = steps // k
            size = shape[axis] // n_blocks
            if shape[axis] % n_blocks == 0 and size % unit == 0:
                block = (size, shape[1]) if axis == 0 else (shape[0], size)
                if axis == 0:
                    return pl.BlockSpec(block, lambda b, i, k=k: ((b * n_cols_grid + i) // k, 0))
                return pl.BlockSpec(block, lambda b, i, k=k: (0, (b * n_cols_grid + i) // k))
            k *= 2
    raise ValueError(f"cannot stream a {shape} matrix over {steps} grid steps")


def _cast_side_streams(src_refs, dst_refs):
    for src, dst in zip(src_refs, dst_refs):
        dst[...] = src[...].astype(BF16)


def _ffn_tile(x_ref, ada_ref, side_in, o_ref, side_out, nw_ref, win_ref, wout_ref, nf_ref, a_ref,
              *, ada_base, chunks, d_ff, final_norm):
    _cast_side_streams(side_in, side_out)
    x = x_ref[0]
    shift = ada_ref[0, ada_base:ada_base + 1, :]
    scale = ada_ref[0, ada_base + 1:ada_base + 2, :]
    gate = ada_ref[0, ada_base + 2:ada_base + 3, :]
    h = _norm_mod(x, nw_ref[...], shift, scale).astype(BF16)
    off = 0
    for w in chunks:
        g = _dot(h, win_ref[:, off:off + w].astype(BF16))
        u = _dot(h, win_ref[:, d_ff + off:d_ff + off + w].astype(BF16))
        a_ref[:, off:off + w] = (g * jax.nn.sigmoid(g) * u).astype(BF16)
        off += w
    n_rows = a_ref.shape[0]
    n_halves = 2 if final_norm else 1
    for r in range(n_halves):
        rs = slice(r * n_rows // n_halves, (r + 1) * n_rows // n_halves)
        y = _dot(a_ref[rs, :], wout_ref[...].astype(BF16))
        out = x[rs] + (0.5 * gate) * y
        if final_norm:
            ms = jnp.mean(out * out, axis=-1, keepdims=True)
            out = out * lax.rsqrt(ms + EPS) * nf_ref[...]
        o_ref[0, rs, :] = out


def _ffn_call(x, ada3, norm_w, w_in, w_out, norm_f, side=(), *, ada_base, final_norm, tm):
    bsz, t, d = x.shape
    d_ff = w_out.shape[0]
    n_t = t // tm
    n_side = len(side)
    chunks = [FFN_CHUNK] * (d_ff // FFN_CHUNK)
    if d_ff % FFN_CHUNK:
        chunks.append(d_ff % FFN_CHUNK)
    side_specs = [_side_spec(w.shape, bsz, n_t) for w in side]
    tile_spec = pl.BlockSpec((1, tm, d), lambda b, i: (b, i, 0))
    ada_spec = pl.BlockSpec((1, N_ADA, d), lambda b, i: (b, 0, 0))

    def outer(*refs):
        x_hbm, ada_hbm = refs[:2]
        side_hbm = refs[2:2 + n_side]
        nw_ref, win_ref, wout_ref, nf_ref = refs[2 + n_side:6 + n_side]
        o_hbm = refs[6 + n_side]
        side_out_hbm = refs[7 + n_side:7 + 2 * n_side]
        a_ref = refs[7 + 2 * n_side]

        def tile(*trefs):
            x_ref, ada_ref = trefs[:2]
            s_in = trefs[2:2 + n_side]
            o_ref = trefs[2 + n_side]
            s_out = trefs[3 + n_side:3 + 2 * n_side]
            _ffn_tile(x_ref, ada_ref, s_in, o_ref, s_out, nw_ref, win_ref, wout_ref, nf_ref, a_ref,
                      ada_base=ada_base, chunks=tuple(chunks), d_ff=d_ff, final_norm=final_norm)

        pltpu.emit_pipeline(
            tile, grid=(bsz, n_t),
            in_specs=[pl.BlockSpec((1, tm, d), lambda b, i: (b, i, 0),
                                   pipeline_mode=pl.Buffered(3)), ada_spec] + side_specs,
            out_specs=[tile_spec] + side_specs,
        )(x_hbm, ada_hbm, *side_hbm, o_hbm, *side_out_hbm)

    hbm = pl.BlockSpec(memory_space=pl.ANY)
    vmem = pl.BlockSpec(memory_space=pltpu.VMEM)
    outs = pl.pallas_call(
        outer,
        in_specs=[hbm, hbm] + [hbm] * n_side + [vmem, vmem, vmem, vmem],
        out_specs=[hbm] + [hbm] * n_side,
        out_shape=[jax.ShapeDtypeStruct(x.shape, F32)]
        + [jax.ShapeDtypeStruct(w.shape, BF16) for w in side],
        scratch_shapes=[pltpu.VMEM((tm, d_ff), BF16)],
        compiler_params=pltpu.CompilerParams(vmem_limit_bytes=VMEM_LIMIT),
        name="ffn_final" if final_norm else "ffn",
    )(x, ada3, *side, norm_w, w_in, w_out, norm_f)
    return outs[0], tuple(outs[1:])


def _band_masks():
    i = np.arange(CHUNK)[None, :, None]
    j = (np.arange(KEY_W) % GLA_DK)[None, None, :]
    d = np.arange(SUB)[:, None, None]
    return jnp.asarray(((j == i - d) & (i % SUB >= d)).astype(np.float32))


def _mix_kernel(x_ref, ada_ref, nw_ref, win_ref, convw_ref, wgk2_ref, bgk_ref,
                gn_ref, wout_ref, band_ref, nxt_in_ref, nxt_out_ref,
                o_ref, nxt_in_bf_ref, nxt_out_bf_ref,
                s_ref, u_ref, q_s, k_s, lgs3, b_s, v_s, o_s, sc3, sb_s, *, tt):
    nc = tt // CHUNK
    t_idx = pl.program_id(1)

    @pl.when(t_idx == 0)
    def _():
        s_ref[...] = jnp.zeros(s_ref.shape, F32)
        u_ref[0:8, :] = jnp.zeros((8, CONV_WIDTH), F32)
        k_s[:, 0:PAD, :] = jnp.zeros((nc, PAD, KEY_W), F32)
        b_s[:, 0:PAD, :] = jnp.zeros((nc, PAD, KEY_W), F32)
        lgs3[:, 3 * CHUNK:4 * CHUNK, :] = jnp.zeros((nc, CHUNK, KEY_W), BF16)

    x = x_ref[0]
    shift = ada_ref[0, 3:4, :]
    scale = ada_ref[0, 4:5, :]
    gate = ada_ref[0, 5:6, :]
    h = _norm_mod(x, nw_ref[...], shift, scale).astype(BF16)

    def proj(lo, width):
        return lax.dot_general(h, win_ref[lo:lo + width, :].astype(BF16),
                               (((1,), (1,)), ((), ())), preferred_element_type=F32)

    tri_r = lax.broadcasted_iota(jnp.int32, (CHUNK, 4 * CHUNK), 0)
    tri_c = lax.broadcasted_iota(jnp.int32, (CHUNK, 4 * CHUNK), 1)
    lower4 = ((tri_c % CHUNK) <= tri_r).astype(BF16)
    hr = lax.broadcasted_iota(jnp.int32, (KEY_W, KEY_W), 0) // GLA_DK
    hc = lax.broadcasted_iota(jnp.int32, (KEY_W, KEY_W), 1) // GLA_DK
    head_bd = hr == hc
    ones_bd = head_bd.astype(BF16)
    col_blk = (lax.broadcasted_iota(jnp.int32, (1, KEY_W), 1) % GLA_DK) // SUB
    zero_v = jnp.zeros((CHUNK, GLA_DV), BF16)

    base = 3 * CONV_WIDTH
    q_s[...] = proj(base, KEY_W) * (GLA_DK ** -0.5)
    k_s[:, PAD:PAD + CHUNK, :] = proj(base + KEY_W, KEY_W).reshape(nc, CHUNK, KEY_W)
    v_s[...] = proj(base + 2 * KEY_W, VAL_W).astype(BF16)
    gk_t = lax.dot_general(win_ref[MIX_MAIN:MIX_MAIN + GATE_RANK, :].astype(BF16), h,
                           (((1,), (1,)), ((), ())), preferred_element_type=F32)
    z = lax.dot_general(gk_t.astype(BF16), wgk2_ref[...].astype(BF16),
                        (((0,), (0,)), ((), ())), preferred_element_type=F32) + bgk_ref[...]
    log_sig = jnp.minimum(z, 0.0) - jnp.log1p(jnp.exp(-jnp.abs(z)))
    lg = log_sig * (LOG2E / GATE_NORMALIZER)
    lg_hi = lg.astype(BF16)
    rem = lg - lg_hi.astype(F32)
    lg_mid = rem.astype(BF16)
    lg_lo = (rem - lg_mid.astype(F32)).astype(BF16)
    lgs3[:, 0:CHUNK, :] = lg_hi.reshape(nc, CHUNK, KEY_W)
    lgs3[:, CHUNK:2 * CHUNK, :] = lg_mid.reshape(nc, CHUNK, KEY_W)
    lgs3[:, 2 * CHUNK:3 * CHUNK, :] = lg_lo.reshape(nc, CHUNK, KEY_W)
    for c in range(nc):
        b_s[c, PAD:PAD + CHUNK, :] = _dot(lower4, lgs3[c])

    cb = proj(0, CONV_WIDTH)
    u = proj(CONV_WIDTH, CONV_WIDTH) * proj(2 * CONV_WIDTH, CONV_WIDTH)
    u_ref[8:8 + tt, :] = u
    cw = convw_ref[...]
    conv = (u_ref[6:6 + tt, :] * cw[0:1, :] + u_ref[7:7 + tt, :] * cw[1:2, :]
            + u * cw[2:3, :])
    y_conv = cb * conv
    u_ref[0:8, :] = u_ref[tt:tt + 8, :]

    q = q_s[...]
    k = k_s[:, PAD:PAD + CHUNK, :].reshape(tt, KEY_W)
    b = b_s[:, PAD:PAD + CHUNK, :].reshape(tt, KEY_W)
    b3 = b.reshape(nc, CHUNK, KEY_W)
    q3 = q.reshape(nc, CHUNK, KEY_W)
    b_last = jnp.broadcast_to(b3[:, CHUNK - 1:CHUNK, :], (nc, CHUNK, KEY_W)).reshape(tt, KEY_W)
    b8 = b.reshape(tt // SUB, SUB, KEY_W)
    b_end = jnp.broadcast_to(b8[:, SUB - 1:SUB, :], (tt // SUB, SUB, KEY_W)).reshape(tt, KEY_W)
    qg = (q * jnp.exp2(b)).astype(BF16)
    kd = k * jnp.exp2(b_last - b)
    kk = k * jnp.exp2(b_end - b)

    s = None
    for d in range(SUB):
        if d == 0:
            e = q * k
        else:
            k_d = k_s[:, PAD - d:PAD - d + CHUNK, :].reshape(tt, KEY_W)
            b_d = b_s[:, PAD - d:PAD - d + CHUNK, :].reshape(tt, KEY_W)
            e = q * k_d * jnp.exp2(b - b_d)
        term = _dot(e.astype(BF16), ones_bd).reshape(nc, CHUNK, KEY_W) * band_ref[d]
        s = term if s is None else s + term
    sc3[...] = s

    g_out = proj(base + 2 * KEY_W + VAL_W, VAL_W)

    qj_all = []
    for j in range(NSUB - 1):
        r0 = SUB * (j + 1)
        qj_all.append((q3[:, r0:, :] * jnp.exp2(b3[:, r0:, :] - b3[:, r0 - 1:r0, :])).astype(BF16))
    for c in range(nc):
        kc = kk[c * CHUNK:(c + 1) * CHUNK, :]
        kbd = jnp.where(head_bd, jnp.concatenate([kc] * GLA_HEADS, axis=0), 0.0).astype(BF16)
        lhs = jnp.concatenate([qj[c] for qj in qj_all], axis=0)
        out = lax.dot_general(lhs, kbd, (((1,), (1,)), ((), ())), preferred_element_type=F32)
        off = 0
        for j in range(NSUB - 1):
            r0 = SUB * (j + 1)
            n = CHUNK - r0
            sc3[c, r0:, :] = jnp.where(col_blk == j, out[off:off + n, :], sc3[c, r0:, :])
            off += n

    states = [s_ref[hd] for hd in range(GLA_HEADS)]
    for c in range(nc):
        rows = slice(c * CHUNK, (c + 1) * CHUNK)
        bt = jnp.transpose(b[(c + 1) * CHUNK - 8:(c + 1) * CHUNK, :])
        decay_col = jnp.exp2(bt[:, 7:8])
        for hd in range(GLA_HEADS):
            sb_s[c, hd] = states[hd].astype(BF16)
        for p in range(2):
            lanes = slice(p * PAIR_W, (p + 1) * PAIR_W)
            kdt = jnp.transpose(kd[rows, lanes]).astype(BF16)
            ktv = _dot(kdt, v_s[rows, p * PAIR_V:(p + 1) * PAIR_V])
            dec = decay_col[lanes, :]
            h0, h1 = 2 * p, 2 * p + 1
            states[h0] = states[h0] * dec[0:GLA_DK, :] + ktv[0:GLA_DK, 0:GLA_DV]
            states[h1] = states[h1] * dec[GLA_DK:PAIR_W, :] + ktv[GLA_DK:PAIR_W, GLA_DV:PAIR_V]
    for hd in range(GLA_HEADS):
        s_ref[hd] = states[hd]

    for c in range(nc):
        rows = slice(c * CHUNK, (c + 1) * CHUNK)
        for p in range(2):
            lanes = slice(p * PAIR_W, (p + 1) * PAIR_W)
            h0, h1 = 2 * p, 2 * p + 1
            v0 = v_s[rows, h0 * GLA_DV:(h0 + 1) * GLA_DV]
            v1 = v_s[rows, h1 * GLA_DV:(h1 + 1) * GLA_DV]
            lhs = jnp.concatenate([sc3[c, :, lanes].astype(BF16), qg[rows, lanes]], axis=1)
            w = jnp.concatenate([
                jnp.concatenate([v0, zero_v], axis=1),
                jnp.concatenate([zero_v, v1], axis=1),
                jnp.concatenate([sb_s[c, h0], zero_v], axis=1),
                jnp.concatenate([zero_v, sb_s[c, h1]], axis=1)], axis=0)
            o_s[rows, p * PAIR_V:(p + 1) * PAIR_V] = _dot(lhs, w)

    o = o_s[...]
    gn = gn_ref[...]
    parts = []
    for hd in range(GLA_HEADS):
        oh = o[:, hd * GLA_DV:(hd + 1) * GLA_DV]
        ms = jnp.mean(oh * oh, axis=-1, keepdims=True)
        parts.append(oh * lax.rsqrt(ms + EPS) * gn)
    y_gla = jnp.concatenate(parts, axis=1) * (g_out * jax.nn.sigmoid(g_out))
    y = jnp.concatenate([y_conv, y_gla], axis=1).astype(BF16)
    o_ref[0] = x + gate * _dot(y, wout_ref[...].astype(BF16))
    _cast_side_streams((nxt_in_ref, nxt_out_ref), (nxt_in_bf_ref, nxt_out_bf_ref))


def _mix_call(x, ada3, norm_w, w_in_t, conv_w, w_gk2, b_gk, gla_norm, w_out, nxt_in, nxt_out,
              *, tt):
    bsz, t, d = x.shape
    nc = tt // CHUNK
    n_t = t // tt
    band = _band_masks()
    in_spec = _side_spec(nxt_in.shape, bsz, n_t)
    out_spec = _side_spec(nxt_out.shape, bsz, n_t)
    tile_spec = pl.BlockSpec((1, tt, d), lambda b, i: (b, i, 0))
    ada_spec = pl.BlockSpec((1, N_ADA, d), lambda b, i: (b, 0, 0))
    n_const = 8

    def outer(*refs):
        x_hbm, ada_hbm, nin_hbm, nout_hbm = refs[:4]
        consts = refs[4:4 + n_const]
        o_hbm, nin_bf_hbm, nout_bf_hbm = refs[4 + n_const:7 + n_const]
        scratch = refs[7 + n_const:]

        def tile(x_ref, ada_ref, nin_ref, nout_ref, o_ref, nin_bf_ref, nout_bf_ref):
            _mix_kernel(x_ref, ada_ref, *consts, nin_ref, nout_ref, o_ref, nin_bf_ref, nout_bf_ref,
                        *scratch, tt=tt)

        pltpu.emit_pipeline(
            tile, grid=(bsz, n_t),
            in_specs=[tile_spec, ada_spec, in_spec, out_spec],
            out_specs=[tile_spec, in_spec, out_spec],
        )(x_hbm, ada_hbm, nin_hbm, nout_hbm, o_hbm, nin_bf_hbm, nout_bf_hbm)

    hbm = pl.BlockSpec(memory_space=pl.ANY)
    vmem = pl.BlockSpec(memory_space=pltpu.VMEM)
    return pl.pallas_call(
        outer,
        in_specs=[hbm] * 4 + [vmem] * n_const,
        out_specs=[hbm] * 3,
        out_shape=[jax.ShapeDtypeStruct(x.shape, F32),
                   jax.ShapeDtypeStruct(nxt_in.shape, BF16),
                   jax.ShapeDtypeStruct(nxt_out.shape, BF16)],
        scratch_shapes=[
            pltpu.VMEM((GLA_HEADS, GLA_DK, GLA_DV), F32),
            pltpu.VMEM((tt + 8, CONV_WIDTH), F32),
            pltpu.VMEM((tt, KEY_W), F32),
            pltpu.VMEM((nc, PAD + CHUNK, KEY_W), F32),
            pltpu.VMEM((nc, 4 * CHUNK, KEY_W), BF16),
            pltpu.VMEM((nc, PAD + CHUNK, KEY_W), F32),
            pltpu.VMEM((tt, VAL_W), BF16),
            pltpu.VMEM((tt, VAL_W), F32),
            pltpu.VMEM((nc, CHUNK, KEY_W), F32),
            pltpu.VMEM((nc, GLA_HEADS, GLA_DK, GLA_DV), BF16),
        ],
        compiler_params=pltpu.CompilerParams(vmem_limit_bytes=VMEM_LIMIT),
        name="token_mixer",
    )(x, ada3, nxt_in, nxt_out, norm_w, w_in_t, conv_w, w_gk2, b_gk, gla_norm, w_out, band)


def kernel(x, c, w_ada, b_ada, norm_ffn1, w_ffn1_in, w_ffn1_out, norm_mix, w_mix_in, conv_w,
           w_gk2, b_gk, gla_norm, w_mix_out, norm_ffn2, w_ffn2_in, w_ffn2_out, norm_final):
    bsz, t, d = x.shape
    depth = w_ada.shape[0]
    norm_f = norm_final.reshape(1, d)
    for l in range(depth):
        last = l == depth - 1
        ada3 = _ada_call(c, w_ada[l], b_ada[l].reshape(1, -1)).reshape(bsz, N_ADA, d)
        x, (wm_in_t, wm_out) = _ffn_call(
            x, ada3, norm_ffn1[l].reshape(1, d), w_ffn1_in[l], w_ffn1_out[l], norm_f,
            side=(jnp.transpose(w_mix_in[l]), w_mix_out[l]),
            ada_base=0, final_norm=False, tm=TOKEN_TILE)
        x, w2_in, w2_out = _mix_call(
            x, ada3, norm_mix[l].reshape(1, d), wm_in_t, conv_w[l], w_gk2[l],
            b_gk[l].reshape(1, KEY_W), gla_norm[l].reshape(1, GLA_DV), wm_out,
            w_ffn2_in[l], w_ffn2_out[l], tt=2 * TOKEN_TILE)
        x, _ = _ffn_call(x, ada3, norm_ffn2[l].reshape(1, d), w2_in, w2_out, norm_f,
                         ada_base=6, final_norm=last, tm=2 * TOKEN_TILE)
    return x
```

```python
import functools
import math

import jax
import jax.numpy as jnp
import numpy as np
from jax import lax
from jax.experimental import pallas as pl
from jax.experimental.pallas import tpu as pltpu

F32 = jnp.float32
BF16 = jnp.bfloat16

EPS = 1e-6
CONV_WIDTH = 512
GLA_HEADS = 4
GLA_DK = 64
GLA_DV = 128
KEY_W = GLA_HEADS * GLA_DK
VAL_W = GLA_HEADS * GLA_DV
GATE_RANK = 16
GATE_NORMALIZER = 16.0
CHUNK = 64
SUB = 8
NSUB = CHUNK // SUB
PAD = 8
N_ADA = 9
PAIR_W = 2 * GLA_DK
PAIR_V = 2 * GLA_DV
MIX_MAIN = 3 * CONV_WIDTH + 2 * KEY_W + 2 * VAL_W
VMEM_LIMIT = 58 * 1024 * 1024
LOG2E = math.log2(math.e)
FFN_CHUNK = 256
ADA_K_BLOCK = 128
TOKEN_TILE = 512


def _dot(a, b):
    return jnp.dot(a, b, preferred_element_type=F32)


def _norm_mod(x, gain, shift, scale):
    ms = jnp.mean(x * x, axis=-1, keepdims=True)
    y = x * lax.rsqrt(ms + EPS) * gain
    return y * (1.0 + scale) + shift


def _ada_kernel(c_ref, w_ref, b_ref, o_ref):
    @pl.when(pl.program_id(0) == 0)
    def _():
        o_ref[...] = jnp.broadcast_to(b_ref[...], o_ref.shape)

    c = c_ref[...]
    ca = (c * jax.nn.sigmoid(c)).astype(BF16)
    o_ref[...] += _dot(ca, w_ref[...].astype(BF16))


def _ada_call(c, w_ada, b_ada):
    rows, d = c.shape
    n = w_ada.shape[1]
    tk = ADA_K_BLOCK
    return pl.pallas_call(
        _ada_kernel,
        grid=(d // tk,),
        in_specs=[
            pl.BlockSpec((rows, tk), lambda k: (0, k)),
            pl.BlockSpec((tk, n), lambda k: (k, 0)),
            pl.BlockSpec((1, n), lambda k: (0, 0)),
        ],
        out_specs=pl.BlockSpec((rows, n), lambda k: (0, 0)),
        out_shape=jax.ShapeDtypeStruct((rows, n), F32),
        compiler_params=pltpu.CompilerParams(
            dimension_semantics=("arbitrary",), vmem_limit_bytes=VMEM_LIMIT),
        name="ada_proj",
    )(c, w_ada, b_ada)


def _side_spec(shape, n_rows_grid, n_cols_grid):
    steps = n_rows_grid * n_cols_grid
    for axis, unit in ((0, 16), (1, 128)):
        k = 1
        while k < steps:
            n_blocks = steps // k
            size = shape[axis] // n_blocks
            if shape[axis] % n_blocks == 0 and size % unit == 0:
                block = (size, shape[1]) if axis == 0 else (shape[0], size)
                if axis == 0:
                    return pl.BlockSpec(block, lambda b, i, k=k: ((b * n_cols_grid + i) // k, 0))
                return pl.BlockSpec(block, lambda b, i, k=k: (0, (b * n_cols_grid + i) // k))
            k *= 2
    raise ValueError(f"cannot stream a {shape} matrix over {steps} grid steps")


def _side_hold(shape, steps):
    for axis, unit in ((0, 16), (1, 128)):
        k = 1
        while k < steps:
            n_blocks = steps // k
            if shape[axis] % n_blocks == 0 and (shape[axis] // n_blocks) % unit == 0:
                return k
            k *= 2
    raise ValueError(f"cannot stream a {shape} matrix over {steps} grid steps")


def _cast_side_streams(src_refs, dst_refs, step=None, holds=None):
    for n, (src, dst) in enumerate(zip(src_refs, dst_refs)):
        hold = 1 if holds is None else holds[n]
        rows = src.shape[0]
        if hold == 1 or rows % 16:
            dst[...] = src[...].astype(BF16)
        else:
            part = -(-(-(-rows // hold)) // 16) * 16
            stride = (rows - part) // (hold - 1) // 16 * 16
            assert stride <= part and stride * (hold - 2) + part >= rows - part
            start = jnp.where(step % hold == hold - 1, rows - part, (step % hold) * stride)
            start = pl.multiple_of(start, 16)
            dst[pl.ds(start, part), :] = src[pl.ds(start, part), :].astype(BF16)


def _ffn_tile(x_ref, ada_ref, side_in, o_ref, side_out, nw_ref, win_ref, wout_ref, nf_ref, a_ref,
              *, ada_base, chunks, d_ff, final_norm, n_t, side_holds):
    _cast_side_streams(side_in, side_out, pl.program_id(0) * n_t + pl.program_id(1), side_holds)
    x = x_ref[0]
    shift = ada_ref[0, ada_base:ada_base + 1, :]
    scale = ada_ref[0, ada_base + 1:ada_base + 2, :]
    gate = ada_ref[0, ada_base + 2:ada_base + 3, :]
    h = _norm_mod(x, nw_ref[...], shift, scale).astype(BF16)
    off = 0
    for w in chunks:
        g = _dot(h, win_ref[:, off:off + w].astype(BF16))
        u = _dot(h, win_ref[:, d_ff + off:d_ff + off + w].astype(BF16))
        a_ref[:, off:off + w] = (g * jax.nn.sigmoid(g) * u).astype(BF16)
        off += w
    n_rows = a_ref.shape[0]
    n_halves = 2 if final_norm else 1
    for r in range(n_halves):
        rs = slice(r * n_rows // n_halves, (r + 1) * n_rows // n_halves)
        y = _dot(a_ref[rs, :], wout_ref[...].astype(BF16))
        out = x[rs] + (0.5 * gate) * y
        if final_norm:
            ms = jnp.mean(out * out, axis=-1, keepdims=True)
            out = out * lax.rsqrt(ms + EPS) * nf_ref[...]
        o_ref[0, rs, :] = out


def _ffn_call(x, ada3, norm_w, w_in, w_out, norm_f, side=(), *, ada_base, final_norm, tm):
    bsz, t, d = x.shape
    d_ff = w_out.shape[0]
    n_t = t // tm
    n_side = len(side)
    chunks = [FFN_CHUNK] * (d_ff // FFN_CHUNK)
    if d_ff % FFN_CHUNK:
        chunks.append(d_ff % FFN_CHUNK)
    side_specs = [_side_spec(w.shape, bsz, n_t) for w in side]
    tile_spec = pl.BlockSpec((1, tm, d), lambda b, i: (b, i, 0))
    ada_spec = pl.BlockSpec((1, N_ADA, d), lambda b, i: (b, 0, 0))

    def outer(*refs):
        x_hbm, ada_hbm = refs[:2]
        side_hbm = refs[2:2 + n_side]
        nw_ref, win_ref, wout_ref, nf_ref = refs[2 + n_side:6 + n_side]
        o_hbm = refs[6 + n_side]
        side_out_hbm = refs[7 + n_side:7 + 2 * n_side]
        a_ref = refs[7 + 2 * n_side]

        def tile(*trefs):
            x_ref, ada_ref = trefs[:2]
            s_in = trefs[2:2 + n_side]
            o_ref = trefs[2 + n_side]
            s_out = trefs[3 + n_side:3 + 2 * n_side]
            _ffn_tile(x_ref, ada_ref, s_in, o_ref, s_out, nw_ref, win_ref, wout_ref, nf_ref, a_ref,
                      ada_base=ada_base, chunks=tuple(chunks), d_ff=d_ff, final_norm=final_norm,
                      n_t=n_t, side_holds=tuple(_side_hold(w.shape, bsz * n_t) for w in side))

        pltpu.emit_pipeline(
            tile, grid=(bsz, n_t),
            in_specs=[tile_spec, ada_spec] + side_specs,
            out_specs=[tile_spec] + side_specs,
        )(x_hbm, ada_hbm, *side_hbm, o_hbm, *side_out_hbm)

    hbm = pl.BlockSpec(memory_space=pl.ANY)
    vmem = pl.BlockSpec(memory_space=pltpu.VMEM)
    outs = pl.pallas_call(
        outer,
        in_specs=[hbm, hbm] + [hbm] * n_side + [vmem, vmem, vmem, vmem],
        out_specs=[hbm] + [hbm] * n_side,
        out_shape=[jax.ShapeDtypeStruct(x.shape, F32)]
        + [jax.ShapeDtypeStruct(w.shape, BF16) for w in side],
        scratch_shapes=[pltpu.VMEM((tm, d_ff), BF16)],
        compiler_params=pltpu.CompilerParams(vmem_limit_bytes=VMEM_LIMIT),
        name="ffn_final" if final_norm else "ffn",
    )(x, ada3, *side, norm_w, w_in, w_out, norm_f)
    return outs[0], tuple(outs[1:])


def _band_masks():
    i = np.arange(CHUNK)[None, :, None]
    j = (np.arange(KEY_W) % GLA_DK)[None, None, :]
    d = np.arange(SUB)[:, None, None]
    return jnp.asarray(((j == i - d) & (i % SUB >= d)).astype(np.float32))


def _mix_kernel(x_ref, ada_ref, nw_ref, win_ref, convw_ref, wgk2_ref, bgk_ref,
                gn_ref, wout_ref, band_ref, nxt_in_ref, nxt_out_ref,
                o_ref, nxt_in_bf_ref, nxt_out_bf_ref,
                s_ref, u_ref, q_s, k_s, lgs3, b_s, v_s, o_s, sc3, sb_s, *, tt):
    nc = tt // CHUNK
    t_idx = pl.program_id(1)

    @pl.when(t_idx == 0)
    def _():
        s_ref[...] = jnp.zeros(s_ref.shape, F32)
        u_ref[0:8, :] = jnp.zeros((8, CONV_WIDTH), F32)
        k_s[:, 0:PAD, :] = jnp.zeros((nc, PAD, KEY_W), F32)
        b_s[:, 0:PAD, :] = jnp.zeros((nc, PAD, KEY_W), F32)
        lgs3[:, 3 * CHUNK:4 * CHUNK, :] = jnp.zeros((nc, CHUNK, KEY_W), BF16)

    x = x_ref[0]
    shift = ada_ref[0, 3:4, :]
    scale = ada_ref[0, 4:5, :]
    gate = ada_ref[0, 5:6, :]
    h = _norm_mod(x, nw_ref[...], shift, scale).astype(BF16)

    def proj(lo, width):
        return lax.dot_general(h, win_ref[lo:lo + width, :].astype(BF16),
                               (((1,), (1,)), ((), ())), preferred_element_type=F32)

    tri_r = lax.broadcasted_iota(jnp.int32, (CHUNK, 4 * CHUNK), 0)
    tri_c = lax.broadcasted_iota(jnp.int32, (CHUNK, 4 * CHUNK), 1)
    lower4 = ((tri_c % CHUNK) <= tri_r).astype(BF16)
    hr = lax.broadcasted_iota(jnp.int32, (KEY_W, KEY_W), 0) // GLA_DK
    hc = lax.broadcasted_iota(jnp.int32, (KEY_W, KEY_W), 1) // GLA_DK
    head_bd = hr == hc
    ones_bd = head_bd.astype(BF16)
    col_blk = (lax.broadcasted_iota(jnp.int32, (1, KEY_W), 1) % GLA_DK) // SUB
    zero_v = jnp.zeros((CHUNK, GLA_DV), BF16)

    base = 3 * CONV_WIDTH
    q_s[...] = proj(base, KEY_W) * (GLA_DK ** -0.5)
    k_s[:, PAD:PAD + CHUNK, :] = proj(base + KEY_W, KEY_W).reshape(nc, CHUNK, KEY_W)
    v_s[...] = proj(base + 2 * KEY_W, VAL_W).astype(BF16)
    gk_t = lax.dot_general(win_ref[MIX_MAIN:MIX_MAIN + GATE_RANK, :].astype(BF16), h,
                           (((1,), (1,)), ((), ())), preferred_element_type=F32)
    z = lax.dot_general(gk_t.astype(BF16), wgk2_ref[...].astype(BF16),
                        (((0,), (0,)), ((), ())), preferred_element_type=F32) + bgk_ref[...]
    log_sig = jnp.minimum(z, 0.0) - jnp.log1p(jnp.exp(-jnp.abs(z)))
    lg = log_sig * (LOG2E / GATE_NORMALIZER)
    lg_hi = lg.astype(BF16)
    rem = lg - lg_hi.astype(F32)
    lg_mid = rem.astype(BF16)
    lg_lo = (rem - lg_mid.astype(F32)).astype(BF16)
    lgs3[:, 0:CHUNK, :] = lg_hi.reshape(nc, CHUNK, KEY_W)
    lgs3[:, CHUNK:2 * CHUNK, :] = lg_mid.reshape(nc, CHUNK, KEY_W)
    lgs3[:, 2 * CHUNK:3 * CHUNK, :] = lg_lo.reshape(nc, CHUNK, KEY_W)
    for c in range(nc):
        b_s[c, PAD:PAD + CHUNK, :] = _dot(lower4, lgs3[c])

    cb = proj(0, CONV_WIDTH)
    u = proj(CONV_WIDTH, CONV_WIDTH) * proj(2 * CONV_WIDTH, CONV_WIDTH)
    u_ref[8:8 + tt, :] = u
    cw = convw_ref[...]
    conv = (u_ref[6:6 + tt, :] * cw[0:1, :] + u_ref[7:7 + tt, :] * cw[1:2, :]
            + u * cw[2:3, :])
    y_conv = cb * conv
    u_ref[0:8, :] = u_ref[tt:tt + 8, :]

    q = q_s[...]
    k = k_s[:, PAD:PAD + CHUNK, :].reshape(tt, KEY_W)
    b = b_s[:, PAD:PAD + CHUNK, :].reshape(tt, KEY_W)
    b3 = b.reshape(nc, CHUNK, KEY_W)
    q3 = q.reshape(nc, CHUNK, KEY_W)
    b_last = jnp.broadcast_to(b3[:, CHUNK - 1:CHUNK, :], (nc, CHUNK, KEY_W)).reshape(tt, KEY_W)
    b8 = b.reshape(tt // SUB, SUB, KEY_W)
    b_end = jnp.broadcast_to(b8[:, SUB - 1:SUB, :], (tt // SUB, SUB, KEY_W)).reshape(tt, KEY_W)
    qg = (q * jnp.exp2(b)).astype(BF16)
    kd = k * jnp.exp2(b_last - b)
    kk = k * jnp.exp2(b_end - b)

    s = None
    for d in range(SUB):
        if d == 0:
            e = q * k
        else:
            k_d = k_s[:, PAD - d:PAD - d + CHUNK, :].reshape(tt, KEY_W)
            b_d = b_s[:, PAD - d:PAD - d + CHUNK, :].reshape(tt, KEY_W)
            e = q * k_d * jnp.exp2(b - b_d)
        term = _dot(e.astype(BF16), ones_bd).reshape(nc, CHUNK, KEY_W) * band_ref[d]
        s = term if s is None else s + term
    sc3[...] = s

    g_out = proj(base + 2 * KEY_W + VAL_W, VAL_W)

    qj_all = []
    for j in range(NSUB - 1):
        r0 = SUB * (j + 1)
        qj_all.append((q3[:, r0:, :] * jnp.exp2(b3[:, r0:, :] - b3[:, r0 - 1:r0, :])).astype(BF16))
    for c in range(nc):
        kc = kk[c * CHUNK:(c + 1) * CHUNK, :]
        kbd = jnp.where(head_bd, jnp.concatenate([kc] * GLA_HEADS, axis=0), 0.0).astype(BF16)
        lhs = jnp.concatenate([qj[c] for qj in qj_all], axis=0)
        out = lax.dot_general(lhs, kbd, (((1,), (1,)), ((), ())), preferred_element_type=F32)
        off = 0
        for j in range(NSUB - 1):
            r0 = SUB * (j + 1)
            n = CHUNK - r0
            sc3[c, r0:, :] = jnp.where(col_blk == j, out[off:off + n, :], sc3[c, r0:, :])
            off += n

    states = [s_ref[hd] for hd in range(GLA_HEADS)]
    for c in range(nc):
        rows = slice(c * CHUNK, (c + 1) * CHUNK)
        bt = jnp.transpose(b[(c + 1) * CHUNK - 8:(c + 1) * CHUNK, :])
        decay_col = jnp.exp2(bt[:, 7:8])
        for hd in range(GLA_HEADS):
            sb_s[c, hd] = states[hd].astype(BF16)
        for p in range(2):
            lanes = slice(p * PAIR_W, (p + 1) * PAIR_W)
            kdt = jnp.transpose(kd[rows, lanes]).astype(BF16)
            ktv = _dot(kdt, v_s[rows, p * PAIR_V:(p + 1) * PAIR_V])
            dec = decay_col[lanes, :]
            h0, h1 = 2 * p, 2 * p + 1
            states[h0] = states[h0] * dec[0:GLA_DK, :] + ktv[0:GLA_DK, 0:GLA_DV]
            states[h1] = states[h1] * dec[GLA_DK:PAIR_W, :] + ktv[GLA_DK:PAIR_W, GLA_DV:PAIR_V]
    for hd in range(GLA_HEADS):
        s_ref[hd] = states[hd]

    for c in range(nc):
        rows = slice(c * CHUNK, (c + 1) * CHUNK)
        for p in range(2):
            lanes = slice(p * PAIR_W, (p + 1) * PAIR_W)
            h0, h1 = 2 * p, 2 * p + 1
            v0 = v_s[rows, h0 * GLA_DV:(h0 + 1) * GLA_DV]
            v1 = v_s[rows, h1 * GLA_DV:(h1 + 1) * GLA_DV]
            lhs = jnp.concatenate([sc3[c, :, lanes].astype(BF16), qg[rows, lanes]], axis=1)
            w = jnp.concatenate([
                jnp.concatenate([v0, zero_v], axis=1),
                jnp.concatenate([zero_v, v1], axis=1),
                jnp.concatenate([sb_s[c, h0], zero_v], axis=1),
                jnp.concatenate([zero_v, sb_s[c, h1]], axis=1)], axis=0)
            o_s[rows, p * PAIR_V:(p + 1) * PAIR_V] = _dot(lhs, w)

    o = o_s[...]
    gn = gn_ref[...]
    parts = []
    for hd in range(GLA_HEADS):
        oh = o[:, hd * GLA_DV:(hd + 1) * GLA_DV]
        ms = jnp.mean(oh * oh, axis=-1, keepdims=True)
        parts.append(oh * lax.rsqrt(ms + EPS) * gn)
    y_gla = jnp.concatenate(parts, axis=1) * (g_out * jax.nn.sigmoid(g_out))
    y = jnp.concatenate([y_conv, y_gla], axis=1).astype(BF16)
    o_ref[0] = x + gate * _dot(y, wout_ref[...].astype(BF16))
    _cast_side_streams((nxt_in_ref, nxt_out_ref), (nxt_in_bf_ref, nxt_out_bf_ref))


def _mix_call(x, ada3, norm_w, w_in_t, conv_w, w_gk2, b_gk, gla_norm, w_out, nxt_in, nxt_out,
              *, tt):
    bsz, t, d = x.shape
    nc = tt // CHUNK
    n_t = t // tt
    band = _band_masks()
    in_spec = _side_spec(nxt_in.shape, bsz, n_t)
    out_spec = _side_spec(nxt_out.shape, bsz, n_t)
    tile_spec = pl.BlockSpec((1, tt, d), lambda b, i: (b, i, 0))
    ada_spec = pl.BlockSpec((1, N_ADA, d), lambda b, i: (b, 0, 0))
    n_const = 8

    def outer(*refs):
        x_hbm, ada_hbm, nin_hbm, nout_hbm = refs[:4]
        consts = refs[4:4 + n_const]
        o_hbm, nin_bf_hbm, nout_bf_hbm = refs[4 + n_const:7 + n_const]
        scratch = refs[7 + n_const:]

        def tile(x_ref, ada_ref, nin_ref, nout_ref, o_ref, nin_bf_ref, nout_bf_ref):
            _mix_kernel(x_ref, ada_ref, *consts, nin_ref, nout_ref, o_ref, nin_bf_ref, nout_bf_ref,
                        *scratch, tt=tt)

        pltpu.emit_pipeline(
            tile, grid=(bsz, n_t),
            in_specs=[tile_spec, ada_spec, in_spec, out_spec],
            out_specs=[tile_spec, in_spec, out_spec],
        )(x_hbm, ada_hbm, nin_hbm, nout_hbm, o_hbm, nin_bf_hbm, nout_bf_hbm)

    hbm = pl.BlockSpec(memory_space=pl.ANY)
    vmem = pl.BlockSpec(memory_space=pltpu.VMEM)
    return pl.pallas_call(
        outer,
        in_specs=[hbm] * 4 + [vmem] * n_const,
        out_specs=[hbm] * 3,
        out_shape=[jax.ShapeDtypeStruct(x.shape, F32),
                   jax.ShapeDtypeStruct(nxt_in.shape, BF16),
                   jax.ShapeDtypeStruct(nxt_out.shape, BF16)],
        scratch_shapes=[
            pltpu.VMEM((GLA_HEADS, GLA_DK, GLA_DV), F32),
            pltpu.VMEM((tt + 8, CONV_WIDTH), F32),
            pltpu.VMEM((tt, KEY_W), F32),
            pltpu.VMEM((nc, PAD + CHUNK, KEY_W), F32),
            pltpu.VMEM((nc, 4 * CHUNK, KEY_W), BF16),
            pltpu.VMEM((nc, PAD + CHUNK, KEY_W), F32),
            pltpu.VMEM((tt, VAL_W), BF16),
            pltpu.VMEM((tt, VAL_W), F32),
            pltpu.VMEM((nc, CHUNK, KEY_W), F32),
            pltpu.VMEM((nc, GLA_HEADS, GLA_DK, GLA_DV), BF16),
        ],
        compiler_params=pltpu.CompilerParams(vmem_limit_bytes=VMEM_LIMIT),
        name="token_mixer",
    )(x, ada3, nxt_in, nxt_out, norm_w, w_in_t, conv_w, w_gk2, b_gk, gla_norm, w_out, band)


def kernel(x, c, w_ada, b_ada, norm_ffn1, w_ffn1_in, w_ffn1_out, norm_mix, w_mix_in, conv_w,
           w_gk2, b_gk, gla_norm, w_mix_out, norm_ffn2, w_ffn2_in, w_ffn2_out, norm_final):
    bsz, t, d = x.shape
    depth = w_ada.shape[0]
    norm_f = norm_final.reshape(1, d)
    for l in range(depth):
        last = l == depth - 1
        ada3 = _ada_call(c, w_ada[l], b_ada[l].reshape(1, -1)).reshape(bsz, N_ADA, d)
        x, (wm_in_t, wm_out) = _ffn_call(
            x, ada3, norm_ffn1[l].reshape(1, d), w_ffn1_in[l], w_ffn1_out[l], norm_f,
            side=(jnp.transpose(w_mix_in[l]), w_mix_out[l]),
            ada_base=0, final_norm=False, tm=TOKEN_TILE)
        x, w2_in, w2_out = _mix_call(
            x, ada3, norm_mix[l].reshape(1, d), wm_in_t, conv_w[l], w_gk2[l],
            b_gk[l].reshape(1, KEY_W), gla_norm[l].reshape(1, GLA_DV), wm_out,
            w_ffn2_in[l], w_ffn2_out[l], tt=2 * TOKEN_TILE)
        x, _ = _ffn_call(x, ada3, norm_ffn2[l].reshape(1, d), w2_in, w2_out, norm_f,
                         ada_base=6, final_norm=last, tm=2 * TOKEN_TILE)
    return x
```
